```python
import math
import jax, jax.numpy as jnp
from jax import lax
import numpy as np

D_MODEL = 1024
BATCH = 8
SEQ = 2048
DEPTH = 4
DEC_BATCH = 128
DEC_SEQ = 4
PAST_LEN = 16384
PAGE_SIZE = 128

SSD_INNER = D_MODEL
SSD_HEADDIM = 64
SSD_HEADS = SSD_INNER // SSD_HEADDIM
SSD_GROUPS = 2
SSD_REP = SSD_HEADS // SSD_GROUPS
SSD_STATE = 128
SSD_CONV = 4
SSD_CONV_DIM = SSD_INNER + 2 * SSD_GROUPS * SSD_STATE
CHUNK = 128
M_HEADS = 4
M_INNER = D_MODEL
M_HEADDIM = M_INNER // M_HEADS
POOL_WINDOWS = (2, 4, 8, 16)
POOL_GROUPS = len(POOL_WINDOWS)
POOL_WIDTH = D_MODEL
POOL_GW = POOL_WIDTH // POOL_GROUPS
POOL_BUF = max(POOL_WINDOWS) - 1
N_BRANCH = 3
N_EGROUPS = 4
EXP_PER_GROUP = 4
N_EXPERTS = N_EGROUPS * EXP_PER_GROUP
TOP_K = 2
D_FF_E = D_MODEL // 4
ALPHA = (2 * DEPTH) ** 0.25
BETA = (8 * DEPTH) ** -0.25
LN_EPS = 1e-5
RMS_EPS = 1e-6

_IN_SIZES = (SSD_INNER, SSD_CONV_DIM, SSD_HEADS,
             M_INNER, M_INNER, M_INNER, M_HEADS, M_HEADS, M_INNER,
             POOL_WIDTH,
             N_BRANCH * D_MODEL)
IN_DIM = sum(_IN_SIZES)
_IN_OFFSETS = tuple(int(v) for v in np.cumsum(_IN_SIZES)[:-1])

kernel_name = "hybrid_ssd_mlstm_pool_hmoe_decoder_step"


def layer_norm(x, g, b):
    xf = x.astype(jnp.float32)
    mu = xf.mean(-1, keepdims=True)
    var = jnp.square(xf - mu).mean(-1, keepdims=True)
    return ((xf - mu) * lax.rsqrt(var + LN_EPS) * g + b).astype(x.dtype)


def rms_norm(x, g):
    xf = x.astype(jnp.float32)
    return (xf * lax.rsqrt(jnp.square(xf).mean(-1, keepdims=True) + RMS_EPS) * g).astype(x.dtype)


def head_norm(h):
    mu = h.mean(-1, keepdims=True)
    var = jnp.square(h - mu).mean(-1, keepdims=True)
    return (h - mu) * lax.rsqrt(var + LN_EPS)


def ada_mod(c, w, b):
    m = c @ w + b
    shift, scale, gate = jnp.split(m, 3, axis=-1)
    return shift[:, None], scale[:, None], gate[:, None]


def _chunks(a, nc, L):
    return jnp.moveaxis(a.reshape(a.shape[0], nc, L, *a.shape[2:]), 1, 0)


def causal_conv(u, buf, w, b):
    t = u.shape[1]
    z = jnp.concatenate([buf.astype(u.dtype), u], axis=1)
    y = sum(z[:, k:k + t] * w[k] for k in range(SSD_CONV)) + b
    return jax.nn.silu(y), z[:, -(SSD_CONV - 1):]


def ssd_scan(x, dt, A, Bm, Cm, s0):
    t = x.shape[1]
    L = math.gcd(t, CHUNK)
    nc = t // L
    causal = jnp.tril(jnp.ones((L, L), bool))

    def step(s, inp):
        xc, dtc, Bc, Cc = inp
        cum = jnp.cumsum(dtc * A, axis=1)
        cum_h = jnp.moveaxis(cum, 1, -1)
        seg = jnp.where(causal, cum_h[..., :, None] - cum_h[..., None, :], -jnp.inf)
        cb = jnp.einsum('blgn,bsgn->bgls', Cc, Bc)
        w = cb[:, :, None] * jnp.exp(seg) * jnp.moveaxis(dtc, 1, -1)[..., None, :]
        y = jnp.einsum('bgrls,bsgrp->blgrp', w, xc)
        y = y + jnp.einsum('blgn,bgrpn->blgrp', Cc, s) * jnp.exp(cum)[..., None]
        last = cum[:, -1]
        ws = jnp.exp(last[:, None] - cum) * dtc
        s = jnp.exp(last)[..., None, None] * s + jnp.einsum('bsgr,bsgn,bsgrp->bgrpn', ws, Bc, xc)
        return s, y

    s, ys = lax.scan(step, s0, (_chunks(x, nc, L), _chunks(dt, nc, L), _chunks(Bm, nc, L), _chunks(Cm, nc, L)))
    return jnp.moveaxis(ys, 0, 1).reshape(x.shape), s


def mlstm_scan(q, k, v, logi, logf, C0, n0, m0):
    t = q.shape[1]
    L = math.gcd(t, CHUNK)
    nc = t // L
    causal = jnp.tril(jnp.ones((L, L), bool))

    def step(carry, inp):
        C, n, mp = carry
        qc, kc, vc, ic, fc = inp
        bcum = jnp.moveaxis(jnp.cumsum(fc, axis=1), 1, -1)
        ih = jnp.moveaxis(ic, 1, -1)
        dmat = jnp.where(causal, bcum[..., :, None] - bcum[..., None, :] + ih[..., None, :], -jnp.inf)
        m_st = bcum + mp[..., None]
        m = jnp.maximum(m_st, dmat.max(-1))
        wts = jnp.exp(dmat - m[..., None]) * jnp.einsum('blhk,bshk->bhls', qc, kc)
        ws = jnp.exp(m_st - m)
        num = jnp.einsum('bhls,bshv->bhlv', wts, vc) + ws[..., None] * jnp.einsum('bhvk,blhk->bhlv', C, qc)
        den = wts.sum(-1) + ws * jnp.einsum('bhk,blhk->bhl', n, qc)
        hc = num / jnp.maximum(jnp.abs(den), jnp.exp(-m))[..., None]
        last_b = bcum[..., -1]
        m_new = m[..., -1]
        wsrc = jnp.exp(last_b[..., None] - bcum + ih - m_new[..., None])
        wprev = jnp.exp(last_b + mp - m_new)
        C = wprev[..., None, None] * C + jnp.einsum('bhs,bshv,bshk->bhvk', wsrc, vc, kc)
        n = wprev[..., None] * n + jnp.einsum('bhs,bshk->bhk', wsrc, kc)
        return (C, n, m_new), jnp.moveaxis(hc, 2, 1)

    (C, n, m), hs = lax.scan(step, (C0, n0, m0),
                             (_chunks(q, nc, L), _chunks(k, nc, L), _chunks(v, nc, L),
                              _chunks(logi, nc, L), _chunks(logf, nc, L)))
    return jnp.moveaxis(hs, 0, 1).reshape(q.shape[0], t, q.shape[2], v.shape[3]), C, n, m


def pool_mix(u, buf, pool_valid, pool_w, pool_scale):
    b, t, _ = u.shape
    z = jnp.concatenate([buf.astype(u.dtype), u], axis=1)
    zf = z.astype(jnp.float32)
    cs = jnp.concatenate([jnp.zeros_like(zf[:, :1]), jnp.cumsum(zf, axis=1)], axis=1)
    n_avail = jnp.arange(t) + 1 + pool_valid
    means = []
    for g, w in enumerate(POOL_WINDOWS):
        sl = slice(g * POOL_GW, (g + 1) * POOL_GW)
        wsum = cs[:, POOL_BUF + 1:, sl] - cs[:, POOL_BUF + 1 - w:POOL_BUF + 1 - w + t, sl]
        cnt = jnp.minimum(n_avail, w).astype(jnp.float32)[None, :, None]
        means.append(wsum / cnt)
    mean = jnp.stack(means, axis=2)
    d = (mean - u.astype(jnp.float32).reshape(b, t, POOL_GROUPS, POOL_GW)).astype(u.dtype)
    y = jnp.einsum('btgc,gcd->btgd', d, pool_w).reshape(b, t, POOL_WIDTH) * pool_scale
    return y, z[:, -POOL_BUF:]


def token_mixers(u, states, pool_valid, lw):
    ssd_s, conv_s, C_s, n_s, m_s, pool_s = states
    b, t, _ = u.shape
    f32 = jnp.float32
    z, xbc, dt, q, k, v, ig, fg, og, up, gl = jnp.split(u @ lw['w_in'], _IN_OFFSETS, axis=-1)
    xbc, conv_new = causal_conv(xbc, conv_s, lw['conv_w'], lw['conv_b'])
    xs, bm, cm = jnp.split(xbc, [SSD_INNER, SSD_INNER + SSD_GROUPS * SSD_STATE], axis=-1)
    dtp = jax.nn.softplus((dt + lw['ssd_dt_bias']).astype(f32)).reshape(b, t, SSD_GROUPS, SSD_REP)
    A = -jnp.exp(lw['ssd_A_log'].astype(f32)).reshape(SSD_GROUPS, SSD_REP)
    x5 = xs.astype(f32).reshape(b, t, SSD_GROUPS, SSD_REP, SSD_HEADDIM)
    y, ssd_new = ssd_scan(x5, dtp, A,
                          bm.astype(f32).reshape(b, t, SSD_GROUPS, SSD_STATE),
                          cm.astype(f32).reshape(b, t, SSD_GROUPS, SSD_STATE),
                          ssd_s.astype(f32).reshape(b, SSD_GROUPS, SSD_REP, SSD_HEADDIM, SSD_STATE))
    y = y + lw['ssd_D'].astype(f32).reshape(SSD_GROUPS, SSD_REP)[..., None] * x5
    y_ssd = rms_norm(y.reshape(b, t, SSD_INNER).astype(u.dtype) * jax.nn.silu(z), lw['ssd_norm_w'])
    qh = q.astype(f32).reshape(b, t, M_HEADS, M_HEADDIM)
    kh = k.astype(f32).reshape(b, t, M_HEADS, M_HEADDIM) * (M_HEADDIM ** -0.5)
    vh = v.astype(f32).reshape(b, t, M_HEADS, M_HEADDIM)
    logi = (ig + lw['mlstm_gate_b'][:M_HEADS]).astype(f32)
    logf = jax.nn.log_sigmoid((fg + lw['mlstm_gate_b'][M_HEADS:]).astype(f32))
    h, C_new, n_new, m_new = mlstm_scan(qh, kh, vh, logi, logf,
                                        C_s.astype(f32), n_s.astype(f32), m_s.astype(f32))
    hn = head_norm(h) * lw['mlstm_norm_w'].reshape(M_HEADS, M_HEADDIM)
    y_m = hn.reshape(b, t, M_INNER).astype(u.dtype) * jax.nn.sigmoid(og)
    y_pool, pool_new = pool_mix(up, pool_s, pool_valid, lw['pool_w'], lw['pool_scale'])
    g = jax.nn.sigmoid(gl + lw['gate_b']).reshape(b, t, N_BRANCH, D_MODEL)
    merged = (g[:, :, 0] * (y_ssd @ lw['w_br_ssd'])
              + g[:, :, 1] * (y_m @ lw['w_br_mlstm'])
              + g[:, :, 2] * (y_pool @ lw['w_br_pool']))
    out = merged @ lw['w_out']
    new_states = (ssd_new.reshape(b, SSD_HEADS, SSD_HEADDIM, SSD_STATE).astype(ssd_s.dtype),
                  conv_new.astype(conv_s.dtype), C_new.astype(C_s.dtype), n_new.astype(n_s.dtype),
                  m_new.astype(m_s.dtype), pool_new.astype(pool_s.dtype))
    return out, new_states


def hier_moe(u, lw):
    f32 = jnp.float32
    p_grp = jax.nn.softmax((u @ lw['w_rt_group'] + lw['b_rt_group']).astype(f32), axis=-1)
    gp, gi = lax.top_k(p_grp, 1)
    le = (u @ lw['w_rt_expert'] + lw['b_rt_expert']).astype(f32)
    le = le.reshape(*u.shape[:-1], N_EGROUPS, EXP_PER_GROUP)
    le_g = jnp.take_along_axis(le, gi[..., None], axis=-2)[..., 0, :]
    lk, ik = lax.top_k(le_g, TOP_K)
    pk = jax.nn.softmax(lk, axis=-1) * gp
    eidx = gi * EXP_PER_GROUP + ik
    wts = jnp.sum(jax.nn.one_hot(eidx, N_EXPERTS, dtype=f32) * pk[..., None], axis=-2)
    h = jax.nn.silu(jnp.einsum('btd,edf->btef', u, lw['w_e_gate'])) * jnp.einsum('btd,edf->btef', u, lw['w_e_up'])
    h = h * wts.astype(h.dtype)[..., None]
    return jnp.einsum('btef,efd->btd', h, lw['w_e_down'])


def decoder_layer(x, c, states, pool_valid, lw):
    shift, scale, gate = ada_mod(c, lw['w_ada_mix'], lw['b_ada_mix'])
    mix, new_states = token_mixers(x * (1 + scale) + shift, states, pool_valid, lw)
    x = layer_norm(ALPHA * x + (1 + gate) * mix, lw['ln1_g'], lw['ln1_b'])
    shift, scale, gate = ada_mod(c, lw['w_ada_ffn'], lw['b_ada_ffn'])
    ffn = hier_moe(x * (1 + scale) + shift, lw)
    x = layer_norm(ALPHA * x + (1 + gate) * ffn, lw['ln2_g'], lw['ln2_b'])
    return x, new_states


def setup_inputs(seed: int = 0) -> dict:
    key = jax.random.key(seed)
    ks = jax.random.split(key, 40)
    f32 = jnp.float32

    def nrm(i, shape, scale):
        return jax.random.normal(ks[i], shape, f32) * scale

    dt0 = jnp.exp(jax.random.uniform(ks[30], (DEPTH, SSD_HEADS), f32, math.log(1e-3), math.log(1e-1)))
    mlstm_gate_b = jnp.concatenate(
        [nrm(31, (DEPTH, M_HEADS), 0.1),
         jnp.linspace(3.0, 6.0, M_HEADS, dtype=f32)[None] + nrm(32, (DEPTH, M_HEADS), 0.1)], axis=-1)
    return {
        "x_prompt": nrm(0, (BATCH, SEQ, D_MODEL), 1.0),
        "x_sample": nrm(1, (DEC_BATCH, DEC_SEQ, D_MODEL), 1.0),
        "state_ssd": nrm(2, (DEPTH, DEC_BATCH, SSD_HEADS, SSD_HEADDIM, SSD_STATE), 0.1),
        "state_conv": nrm(3, (DEPTH, DEC_BATCH, SSD_CONV - 1, SSD_CONV_DIM), 1.0),
        "state_mlstm_C": nrm(4, (DEPTH, DEC_BATCH, M_HEADS, M_HEADDIM, M_HEADDIM), 0.1),
        "state_mlstm_n": jnp.abs(nrm(5, (DEPTH, DEC_BATCH, M_HEADS, M_HEADDIM), 1.0)),
        "state_mlstm_m": nrm(6, (DEPTH, DEC_BATCH, M_HEADS), 1.0),
        "state_pool": nrm(7, (DEPTH, DEC_BATCH, POOL_BUF, POOL_WIDTH), 1.0),
        "c_prompt": nrm(8, (BATCH, D_MODEL), 1.0),
        "c_sample": nrm(9, (DEC_BATCH, D_MODEL), 1.0),
        "w_ada_mix": nrm(10, (DEPTH, D_MODEL, 3 * D_MODEL), 0.1 * D_MODEL ** -0.5),
        "b_ada_mix": nrm(11, (DEPTH, 3 * D_MODEL), 0.01),
        "w_in": nrm(12, (DEPTH, D_MODEL, IN_DIM), D_MODEL ** -0.5),
        "conv_w": nrm(13, (DEPTH, SSD_CONV, SSD_CONV_DIM), SSD_CONV ** -0.5),
        "conv_b": nrm(14, (DEPTH, SSD_CONV_DIM), 0.01),
        "ssd_A_log": jnp.log(jax.random.uniform(ks[15], (DEPTH, SSD_HEADS), f32, 1.0, 16.0)),
        "ssd_dt_bias": dt0 + jnp.log(-jnp.expm1(-dt0)),
        "ssd_D": 1.0 + nrm(16, (DEPTH, SSD_HEADS), 0.1),
        "ssd_norm_w": 1.0 + nrm(17, (DEPTH, SSD_INNER), 0.1),
        "mlstm_gate_b": mlstm_gate_b,
        "mlstm_norm_w": 1.0 + nrm(18, (DEPTH, M_INNER), 0.1),
        "pool_w": nrm(19, (DEPTH, POOL_GROUPS, POOL_GW, POOL_GW), POOL_GW ** -0.5),
        "pool_scale": 1.0 + nrm(20, (DEPTH, POOL_WIDTH), 0.1),
        "gate_b": nrm(21, (DEPTH, N_BRANCH * D_MODEL), 0.01),
        "w_br_ssd": nrm(22, (DEPTH, SSD_INNER, D_MODEL), SSD_INNER ** -0.5),
        "w_br_mlstm": nrm(23, (DEPTH, M_INNER, D_MODEL), M_INNER ** -0.5),
        "w_br_pool": nrm(24, (DEPTH, POOL_WIDTH, D_MODEL), POOL_WIDTH ** -0.5),
        "w_out": nrm(25, (DEPTH, D_MODEL, D_MODEL), BETA * D_MODEL ** -0.5),
        "ln1_g": 1.0 + nrm(26, (DEPTH, D_MODEL), 0.1),
        "ln1_b": nrm(27, (DEPTH, D_MODEL), 0.01),
        "w_ada_ffn": nrm(28, (DEPTH, D_MODEL, 3 * D_MODEL), 0.1 * D_MODEL ** -0.5),
        "b_ada_ffn": nrm(29, (DEPTH, 3 * D_MODEL), 0.01),
        "w_rt_group": nrm(33, (DEPTH, D_MODEL, N_EGROUPS), D_MODEL ** -0.5),
        "b_rt_group": nrm(34, (DEPTH, N_EGROUPS), 0.01),
        "w_rt_expert": nrm(35, (DEPTH, D_MODEL, N_EXPERTS), D_MODEL ** -0.5),
        "b_rt_expert": nrm(36, (DEPTH, N_EXPERTS), 0.01),
        "w_e_gate": nrm(37, (DEPTH, N_EXPERTS, D_MODEL, D_FF_E), D_MODEL ** -0.5),
        "w_e_up": nrm(38, (DEPTH, N_EXPERTS, D_MODEL, D_FF_E), D_MODEL ** -0.5),
        "w_e_down": nrm(39, (DEPTH, N_EXPERTS, D_FF_E, D_MODEL), BETA * D_FF_E ** -0.5),
        "ln2_g": 1.0 + jax.random.normal(jax.random.fold_in(key, 101), (DEPTH, D_MODEL), f32) * 0.1,
        "ln2_b": jax.random.normal(jax.random.fold_in(key, 102), (DEPTH, D_MODEL), f32) * 0.01,
    }


def reference(x_prompt, x_sample, state_ssd, state_conv, state_mlstm_C, state_mlstm_n, state_mlstm_m,
              state_pool, c_prompt, c_sample, w_ada_mix, b_ada_mix, w_in, conv_w, conv_b, ssd_A_log,
              ssd_dt_bias, ssd_D, ssd_norm_w, mlstm_gate_b, mlstm_norm_w, pool_w, pool_scale, gate_b,
              w_br_ssd, w_br_mlstm, w_br_pool, w_out, ln1_g, ln1_b, w_ada_ffn, b_ada_ffn, w_rt_group,
              b_rt_group, w_rt_expert, b_rt_expert, w_e_gate, w_e_up, w_e_down, ln2_g, ln2_b):
    bp = x_prompt.shape[0]
    dt_p = x_prompt.dtype
    st_dt = state_ssd.dtype
    prompt_init = (jnp.zeros((bp, SSD_HEADS, SSD_HEADDIM, SSD_STATE), st_dt),
                   jnp.zeros((bp, SSD_CONV - 1, SSD_CONV_DIM), dt_p),
                   jnp.zeros((bp, M_HEADS, M_HEADDIM, M_HEADDIM), st_dt),
                   jnp.zeros((bp, M_HEADS, M_HEADDIM), st_dt),
                   jnp.zeros((bp, M_HEADS), st_dt),
                   jnp.zeros((bp, POOL_BUF, POOL_WIDTH), dt_p))
    xp, xs = x_prompt, x_sample
    p_new, s_new = [], []
    for l in range(DEPTH):
        lw = dict(w_ada_mix=w_ada_mix[l], b_ada_mix=b_ada_mix[l], w_in=w_in[l], conv_w=conv_w[l],
                  conv_b=conv_b[l], ssd_A_log=ssd_A_log[l], ssd_dt_bias=ssd_dt_bias[l], ssd_D=ssd_D[l],
                  ssd_norm_w=ssd_norm_w[l], mlstm_gate_b=mlstm_gate_b[l], mlstm_norm_w=mlstm_norm_w[l],
                  pool_w=pool_w[l], pool_scale=pool_scale[l], gate_b=gate_b[l], w_br_ssd=w_br_ssd[l],
                  w_br_mlstm=w_br_mlstm[l], w_br_pool=w_br_pool[l], w_out=w_out[l], ln1_g=ln1_g[l],
                  ln1_b=ln1_b[l], w_ada_ffn=w_ada_ffn[l], b_ada_ffn=b_ada_ffn[l], w_rt_group=w_rt_group[l],
                  b_rt_group=b_rt_group[l], w_rt_expert=w_rt_expert[l], b_rt_expert=b_rt_expert[l],
                  w_e_gate=w_e_gate[l], w_e_up=w_e_up[l], w_e_down=w_e_down[l], ln2_g=ln2_g[l],
                  ln2_b=ln2_b[l])
        xp, sp = decoder_layer(xp, c_prompt, prompt_init, 0, lw)
        sample_states = (state_ssd[l], state_conv[l], state_mlstm_C[l], state_mlstm_n[l],
                         state_mlstm_m[l], state_pool[l])
        xs, ss = decoder_layer(xs, c_sample, sample_states, POOL_BUF, lw)
        p_new.append(sp)
        s_new.append(ss)
    ssd_p, conv_p, mC_p, mn_p, mm_p, pool_p = [jnp.stack(a) for a in zip(*p_new)]
    ssd_s, conv_s, mC_s, mn_s, mm_s, pool_s = [jnp.stack(a) for a in zip(*s_new)]
    return (xp, xs, ssd_p, conv_p, mC_p, mn_p, mm_p, pool_p, ssd_s, conv_s, mC_s, mn_s, mm_s, pool_s)
```

```python
import functools
import math

import jax
import jax.numpy as jnp
from jax import lax
from jax.experimental import pallas as pl
from jax.experimental.pallas import tpu as pltpu

F32 = jnp.float32
BF16 = jnp.bfloat16

D_MODEL = 1024
DEPTH = 4
SSD_HEADS = 16
SSD_HEADDIM = 64
SSD_GROUPS = 2
SSD_REP = SSD_HEADS // SSD_GROUPS
SSD_STATE = 128
SSD_CONV = 4
SSD_CONV_DIM = D_MODEL + 2 * SSD_GROUPS * SSD_STATE
CHUNK = 128
M_HEADS = 4
M_HEADDIM = D_MODEL // M_HEADS
POOL_WINDOWS = (2, 4, 8, 16)
POOL_GW = D_MODEL // len(POOL_WINDOWS)
POOL_BUF = max(POOL_WINDOWS) - 1
N_EGROUPS = 4
EXP_PER_GROUP = 4
N_EXPERTS = N_EGROUPS * EXP_PER_GROUP
D_FF_E = D_MODEL // 4
ALPHA = (2 * DEPTH) ** 0.25
LN_EPS = 1e-5
RMS_EPS = 1e-6

SUBLANES = 8
LANES = 128
VMEM_LIMIT_BYTES = 56 * 1024 * 1024

OFF_Z = 0
OFF_XBC = OFF_Z + D_MODEL
OFF_SMALL = OFF_XBC + SSD_CONV_DIM
OFF_Q = OFF_SMALL + LANES
OFF_K = OFF_Q + D_MODEL
OFF_V = OFF_K + D_MODEL
OFF_O = OFF_V + D_MODEL
OFF_UP = OFF_O + D_MODEL
N_PROJ = OFF_UP + D_MODEL
SM_A, SM_DT, SM_F, SM_I, SM_END = 0, SSD_HEADS, 2 * SSD_HEADS, 2 * SSD_HEADS + M_HEADS, 2 * SSD_HEADS + 2 * M_HEADS
RT_G, RT_E = 0, 16
NEG_BIG = -1e30

ROW_TILE = 256
SAMPLE_ROWS = 8


def _dot(a, b):
    return jnp.dot(a.astype(BF16), b.astype(BF16), preferred_element_type=F32)


def _dot_nt(a, b):
    return lax.dot_general(a.astype(BF16), b.astype(BF16), (((1,), (1,)), ((), ())),
                           preferred_element_type=F32)


def _dot_tn(a, b):
    return lax.dot_general(a.astype(BF16), b.astype(BF16), (((0,), (0,)), ((), ())),
                           preferred_element_type=F32)


def _dot_exact(a, b):
    return jnp.dot(a, b, precision=lax.Precision.HIGHEST, preferred_element_type=F32)


def _dot_nt_exact(a, b):
    return lax.dot_general(a, b, (((1,), (1,)), ((), ())), precision=lax.Precision.HIGHEST,
                           preferred_element_type=F32)


def _sigmoid(x):
    return 1.0 / (1.0 + jnp.exp(-x))


def _silu(x):
    return x * _sigmoid(x)


def _softplus(x):
    return jnp.maximum(x, 0.0) + jnp.log1p(jnp.exp(-jnp.abs(x)))


def _layer_norm(x, g, b):
    mu = jnp.mean(x, axis=-1, keepdims=True)
    xc = x - mu
    var = jnp.mean(xc * xc, axis=-1, keepdims=True)
    return xc * lax.rsqrt(var + LN_EPS) * g + b


def _compiler_params(n_grid):
    return pltpu.CompilerParams(dimension_semantics=("arbitrary",) * n_grid,
                                vmem_limit_bytes=VMEM_LIMIT_BYTES)


def _vmem_full():
    return pl.BlockSpec(memory_space=pltpu.VMEM)


def _ada_kernel(c_ref, w_ref, b_ref, o_ref):
    o_ref[...] = _dot(c_ref[...], w_ref[...]) + b_ref[...]


def _ada_call(c_all, w, b):
    n = c_all.shape[0]
    nb = 3
    return pl.pallas_call(
        _ada_kernel,
        grid=(DEPTH, nb),
        in_specs=[pl.BlockSpec((n, D_MODEL), lambda l, j: (0, 0)),
                  pl.BlockSpec((None, D_MODEL, D_MODEL), lambda l, j: (l, 0, j)),
                  pl.BlockSpec((None, 1, D_MODEL), lambda l, j: (l, 0, j))],
        out_specs=pl.BlockSpec((None, n, D_MODEL), lambda l, j: (l, 0, j)),
        out_shape=jax.ShapeDtypeStruct((DEPTH, n, 3 * D_MODEL), F32),
        compiler_params=_compiler_params(2),
        name="ada_mod",
    )(c_all, w, b)


def _modulate(x_ref, mod_ref):
    x = x_ref[...]
    shift = mod_ref[:, :, 0:D_MODEL]
    scale = mod_ref[:, :, D_MODEL:2 * D_MODEL]
    u = x * (1.0 + scale) + shift
    return u.reshape(x.shape[0] * x.shape[1], D_MODEL)


def _proj_kernel(x_ref, mod_ref, w_ref, o_ref):
    sb, rb, _ = x_ref.shape
    u = _modulate(x_ref, mod_ref).astype(BF16)
    col = 0
    while col < N_PROJ:
        width = min(D_MODEL, N_PROJ - col)
        o_ref[:, :, col:col + width] = _dot(u, w_ref[:, col:col + width]).reshape(sb, rb, width)
        col += width


def _row_blocks(n_seq, rows):
    if rows >= ROW_TILE:
        return 1, ROW_TILE
    return ROW_TILE // rows, rows


def _proj_call(x3, mods, wcat):
    n_seq, rows, _ = x3.shape
    sb, rb = _row_blocks(n_seq, rows)
    return pl.pallas_call(
        _proj_kernel,
        grid=(n_seq // sb, rows // rb),
        in_specs=[pl.BlockSpec((sb, rb, D_MODEL), lambda i, j: (i, j, 0)),
                  pl.BlockSpec((sb, 1, 3 * D_MODEL), lambda i, j: (i, 0, 0)),
                  _vmem_full()],
        out_specs=pl.BlockSpec((sb, rb, N_PROJ), lambda i, j: (i, j, 0)),
        out_shape=jax.ShapeDtypeStruct((n_seq, rows, N_PROJ), F32),
        compiler_params=_compiler_params(2),
        name="in_proj",
    )(x3, mods, wcat)


def _mixer_kernel(proj_ref, ssd0_ref, conv0_ref, mc0_ref, mn0_ref, mm0_ref, pool0_ref,
                  convw_ref, convb_ref, hp_ref, snw_ref, mnw_ref, poolw_ref, pscale_ref,
                  yssd_ref, ym_ref, ypool_ref,
                  ssd_ref, conv_ref, mc_ref, mn_ref, mm_ref, pool_ref,
                  xext_ref, pext_ref, yacc_ref, *, L, tv, pool_valid):
    c = pl.program_id(1)

    @pl.when(c == 0)
    def _load_states():
        ssd_ref[...] = ssd0_ref[...]
        conv_ref[...] = conv0_ref[...]
        mc_ref[...] = mc0_ref[...]
        mn_ref[...] = mn0_ref[...]
        mm_ref[...] = mm0_ref[...]
        pool_ref[...] = pool0_ref[...]

    row_l = lax.broadcasted_iota(jnp.int32, (L, L), 0)
    col_l = lax.broadcasted_iota(jnp.int32, (L, L), 1)
    causal = row_l >= col_l

    lane = lax.broadcasted_iota(jnp.int32, (L, LANES), 1)
    pre = proj_ref[:, OFF_SMALL:OFF_SMALL + LANES] + hp_ref[0:1, :]
    sp = _softplus(pre)
    a_row = -jnp.exp(hp_ref[1:2, :])
    pmat = jnp.where(lane < SM_DT, sp * a_row,
                     jnp.where(lane < SM_F, sp,
                               jnp.where(lane < SM_I, -_softplus(-pre),
                                         jnp.where(lane < SM_END, pre, 0.0))))
    if tv < L:
        row = lax.broadcasted_iota(jnp.int32, (L, LANES), 0)
        pad = jnp.where(lane < SM_I, 0.0, jnp.where(lane < SM_END, NEG_BIG, 0.0))
        pmat = jnp.where(row < tv, pmat, pad)
    tri = causal.astype(F32)
    eye = (lax.broadcasted_iota(jnp.int32, (LANES, LANES), 0)
           == lax.broadcasted_iota(jnp.int32, (LANES, LANES), 1)).astype(F32)
    cum = _dot_exact(tri, pmat)
    pmat_t = _dot_nt_exact(eye, pmat)
    cum_t = _dot_nt_exact(eye, cum)

    xext_ref[SUBLANES - (SSD_CONV - 1):SUBLANES, :] = conv_ref[...]
    xext_ref[SUBLANES:SUBLANES + L, :] = proj_ref[:, OFF_XBC:OFF_XBC + SSD_CONV_DIM]
    acc = convb_ref[...]
    for k in range(SSD_CONV):
        start = SUBLANES - (SSD_CONV - 1) + k
        acc = acc + xext_ref[start:start + L, :] * convw_ref[k:k + 1, :]
    conv_ref[...] = xext_ref[SUBLANES + tv - (SSD_CONV - 1):SUBLANES + tv, :]
    xbc = _silu(acc)
    xs = xbc[:, 0:D_MODEL]
    d_row = hp_ref[2:3, :]
    for g in range(SSD_GROUPS):
        bm = xbc[:, D_MODEL + g * SSD_STATE:D_MODEL + (g + 1) * SSD_STATE]
        cm = xbc[:, D_MODEL + (SSD_GROUPS + g) * SSD_STATE:D_MODEL + (SSD_GROUPS + g + 1) * SSD_STATE]
        cb = _dot_nt(cm, bm)
        for r in range(SSD_REP):
            h = g * SSD_REP + r
            hs = slice(h * SSD_HEADDIM, (h + 1) * SSD_HEADDIM)
            cum_c = cum[:, SM_A + h:SM_A + h + 1]
            cum_r = cum_t[SM_A + h:SM_A + h + 1, :]
            dt_c = pmat[:, SM_DT + h:SM_DT + h + 1]
            dt_r = pmat_t[SM_DT + h:SM_DT + h + 1, :]
            last = cum[L - 1:L, SM_A + h:SM_A + h + 1]
            seg = jnp.where(causal, cum_c - cum_r, -jnp.inf)
            wmat = cb * jnp.exp(seg) * dt_r
            x_h = xs[:, hs]
            s_h = ssd_ref[hs, :]
            y = _dot(wmat, x_h) + _dot_nt(cm, s_h) * jnp.exp(cum_c)
            yacc_ref[:, hs] = y + d_row[:, h:h + 1] * x_h
            ws = jnp.exp(last - cum_c) * dt_c
            ssd_ref[hs, :] = jnp.exp(last) * s_h + _dot_tn(x_h * ws, bm)
    yz = yacc_ref[...] * _silu(proj_ref[:, OFF_Z:OFF_Z + D_MODEL])
    yssd_ref[...] = yz * lax.rsqrt(jnp.mean(yz * yz, axis=-1, keepdims=True) + RMS_EPS) * snw_ref[...]

    for h in range(M_HEADS):
        hs = slice(h * M_HEADDIM, (h + 1) * M_HEADDIM)
        q_h = proj_ref[:, OFF_Q + h * M_HEADDIM:OFF_Q + (h + 1) * M_HEADDIM]
        k_h = proj_ref[:, OFF_K + h * M_HEADDIM:OFF_K + (h + 1) * M_HEADDIM] * (M_HEADDIM ** -0.5)
        v_h = proj_ref[:, OFF_V + h * M_HEADDIM:OFF_V + (h + 1) * M_HEADDIM]
        o_h = proj_ref[:, OFF_O + h * M_HEADDIM:OFF_O + (h + 1) * M_HEADDIM]
        b_c = cum[:, SM_F + h:SM_F + h + 1]
        b_r = cum_t[SM_F + h:SM_F + h + 1, :]
        i_c = pmat[:, SM_I + h:SM_I + h + 1]
        i_r = pmat_t[SM_I + h:SM_I + h + 1, :]
        m_prev = mm_ref[:, h:h + 1]
        c_h = mc_ref[hs, :]
        n_h = mn_ref[h:h + 1, :]
        dmat = jnp.where(causal, b_c - b_r + i_r, -jnp.inf)
        m_st = b_c + m_prev
        m = jnp.maximum(m_st, jnp.max(dmat, axis=-1, keepdims=True))
        wts = jnp.exp(dmat - m) * _dot_nt(q_h, k_h)
        ws = jnp.exp(m_st - m)
        num = _dot(wts, v_h) + ws * _dot_nt(q_h, c_h)
        den = jnp.sum(wts, axis=-1, keepdims=True) + ws * jnp.sum(q_h * n_h, axis=-1, keepdims=True)
        hc = num / jnp.maximum(jnp.abs(den), jnp.exp(-m))
        last_b = cum[L - 1:L, SM_F + h:SM_F + h + 1]
        m_new = m[L - 1:L, :]
        wsrc = jnp.exp(last_b - b_c + i_c - m_new)
        wprev = jnp.exp(last_b + m_prev - m_new)
        mc_ref[hs, :] = wprev * c_h + _dot_tn(v_h * wsrc, k_h)
        mn_ref[h:h + 1, :] = wprev * n_h + jnp.sum(k_h * wsrc, axis=0, keepdims=True)
        mm_ref[:, h:h + 1] = m_new
        mu = jnp.mean(hc, axis=-1, keepdims=True)
        hd = hc - mu
        var = jnp.mean(hd * hd, axis=-1, keepdims=True)
        ym_ref[:, hs] = hd * lax.rsqrt(var + LN_EPS) * mnw_ref[:, hs] * _sigmoid(o_h)

    pext_ref[1:POOL_BUF + 1, :] = pool_ref[...]
    up = proj_ref[:, OFF_UP:OFF_UP + D_MODEL]
    pext_ref[POOL_BUF + 1:POOL_BUF + 1 + L, :] = up
    pos = lax.broadcasted_iota(jnp.int32, (L, 1), 0) + c * L + 1 + pool_valid
    for g, w in enumerate(POOL_WINDOWS):
        gs = slice(g * POOL_GW, (g + 1) * POOL_GW)
        wsum = up[:, gs]
        for j in range(1, w):
            wsum = wsum + pext_ref[POOL_BUF + 1 - j:POOL_BUF + 1 - j + L, gs]
        cnt = jnp.minimum(pos, w).astype(F32)
        dlt = wsum / cnt - up[:, gs]
        ypool_ref[:, gs] = _dot(dlt, poolw_ref[g]) * pscale_ref[:, gs]
    pool_ref[...] = pext_ref[1 + tv:1 + tv + POOL_BUF, :]


def _mixer_call(proj, states, lw, *, L, tv, pool_valid):
    n_seq, rows, _ = proj.shape
    ssd0, conv0, mc0, mn0, mm0, pool0 = states
    state_specs = [pl.BlockSpec((None,) + s.shape[1:], lambda b, c: (b, 0, 0))
                   for s in (ssd0, conv0, mc0, mn0, mm0, pool0)]
    const2 = lambda b, c: (0, 0)
    y_spec = pl.BlockSpec((None, L, D_MODEL), lambda b, c: (b, c, 0))
    y_shape = jax.ShapeDtypeStruct((n_seq, rows, D_MODEL), F32)
    return pl.pallas_call(
        functools.partial(_mixer_kernel, L=L, tv=tv, pool_valid=pool_valid),
        grid=(n_seq, rows // L),
        in_specs=[pl.BlockSpec((None, L, N_PROJ), lambda b, c: (b, c, 0))] + state_specs + [
            pl.BlockSpec((SSD_CONV, SSD_CONV_DIM), const2),
            pl.BlockSpec((1, SSD_CONV_DIM), const2),
            pl.BlockSpec((SUBLANES, LANES), const2),
            pl.BlockSpec((1, D_MODEL), const2),
            pl.BlockSpec((1, D_MODEL), const2),
            pl.BlockSpec((len(POOL_WINDOWS), POOL_GW, POOL_GW), lambda b, c: (0, 0, 0)),
            pl.BlockSpec((1, D_MODEL), const2)],
        out_specs=[y_spec, y_spec, y_spec] + state_specs,
        out_shape=[y_shape, y_shape, y_shape] + [jax.ShapeDtypeStruct(s.shape, s.dtype)
                                                 for s in (ssd0, conv0, mc0, mn0, mm0, pool0)],
        scratch_shapes=[pltpu.VMEM((SUBLANES + L, SSD_CONV_DIM), F32),
                        pltpu.VMEM((POOL_BUF + 1 + L, D_MODEL), F32),
                        pltpu.VMEM((L, D_MODEL), F32)],
        compiler_params=_compiler_params(2),
        name="token_mixers",
    )(proj, ssd0, conv0, mc0, mn0, mm0, pool0,
      lw["conv_w"], lw["conv_b"], lw["head_params"], lw["ssd_norm_w"], lw["mlstm_norm_w"],
      lw["pool_w"], lw["pool_scale"])


def _merge_kernel(x_ref, mod_ref, yssd_ref, ym_ref, ypool_ref, wgl_ref, gb_ref,
                  wbs_ref, wbm_ref, wbp_ref, wout_ref, lng_ref, lnb_ref, o_ref):
    sb, rb, _ = x_ref.shape
    n = sb * rb
    u = _modulate(x_ref, mod_ref).astype(BF16)
    merged = None
    for i, (y_ref, wb_ref) in enumerate(((yssd_ref, wbs_ref), (ym_ref, wbm_ref), (ypool_ref, wbp_ref))):
        cs = slice(i * D_MODEL, (i + 1) * D_MODEL)
        gate = _sigmoid(_dot(u, wgl_ref[:, cs]) + gb_ref[:, cs])
        term = gate * _dot(y_ref[...].reshape(n, D_MODEL), wb_ref[...])
        merged = term if merged is None else merged + term
    mix = _dot(merged, wout_ref[...]).reshape(sb, rb, D_MODEL)
    gate_a = mod_ref[:, :, 2 * D_MODEL:3 * D_MODEL]
    o_ref[...] = _layer_norm(ALPHA * x_ref[...] + (1.0 + gate_a) * mix, lng_ref[...], lnb_ref[...])


def _merge_call(x3, mods, ys, lw):
    n_seq, rows, _ = x3.shape
    sb, rb = _row_blocks(n_seq, rows)
    row_spec = pl.BlockSpec((sb, rb, D_MODEL), lambda i, j: (i, j, 0))
    vec_spec = pl.BlockSpec((1, D_MODEL), lambda i, j: (0, 0))
    return pl.pallas_call(
        _merge_kernel,
        grid=(n_seq // sb, rows // rb),
        in_specs=[row_spec, pl.BlockSpec((sb, 1, 3 * D_MODEL), lambda i, j: (i, 0, 0)),
                  row_spec, row_spec, row_spec,
                  _vmem_full(), pl.BlockSpec((1, 3 * D_MODEL), lambda i, j: (0, 0)),
                  _vmem_full(), _vmem_full(), _vmem_full(), _vmem_full(), vec_spec, vec_spec],
        out_specs=row_spec,
        out_shape=jax.ShapeDtypeStruct(x3.shape, F32),
        compiler_params=_compiler_params(2),
        name="merge_norm",
    )(x3, mods, ys[0], ys[1], ys[2], lw["w_gl"], lw["gate_b"], lw["w_br_ssd"], lw["w_br_mlstm"],
      lw["w_br_pool"], lw["w_out"], lw["ln1_g"], lw["ln1_b"])


def _route(logits):
    n = logits.shape[0]
    lane = lax.broadcasted_iota(jnp.int32, (n, LANES), 1)
    is_g = (lane >= RT_G) & (lane < RT_G + N_EGROUPS)
    lg = jnp.where(is_g, logits, -jnp.inf)
    g_max = jnp.max(lg, axis=-1, keepdims=True)
    g_idx = jnp.min(jnp.where(lg == g_max, lane - RT_G, LANES), axis=-1, keepdims=True)
    g_prob = 1.0 / jnp.sum(jnp.exp(lg - g_max), axis=-1, keepdims=True)
    lo = RT_E + g_idx * EXP_PER_GROUP
    le = jnp.where((lane >= lo) & (lane < lo + EXP_PER_GROUP), logits, -jnp.inf)
    v1 = jnp.max(le, axis=-1, keepdims=True)
    i1 = jnp.min(jnp.where(le == v1, lane, LANES), axis=-1, keepdims=True)
    le2 = jnp.where(lane == i1, -jnp.inf, le)
    v2 = jnp.max(le2, axis=-1, keepdims=True)
    i2 = jnp.min(jnp.where(le2 == v2, lane, LANES), axis=-1, keepdims=True)
    e2 = jnp.exp(v2 - v1)
    p1 = g_prob / (1.0 + e2)
    p2 = g_prob * e2 / (1.0 + e2)
    return jnp.where(lane == i1, p1, 0.0) + jnp.where(lane == i2, p2, 0.0)


def _moe_kernel(x_ref, mod_ref, wrt_ref, brt_ref, wg_ref, wu_ref, wd_ref, lng_ref, lnb_ref, o_ref):
    sb, rb, _ = x_ref.shape
    u = _modulate(x_ref, mod_ref)
    wts = _route(_dot_exact(u, wrt_ref[...]) + brt_ref[...])
    ub = u.astype(BF16)
    ffn = None
    for g in range(N_EGROUPS):
        parts = []
        for r in range(EXP_PER_GROUP):
            e = g * EXP_PER_GROUP + r
            es = slice(e * D_FF_E, (e + 1) * D_FF_E)
            hid = _silu(_dot(ub, wg_ref[:, es])) * _dot(ub, wu_ref[:, es])
            parts.append(hid * wts[:, RT_E + e:RT_E + e + 1])
        hid_g = jnp.concatenate(parts, axis=-1)
        gs = slice(g * EXP_PER_GROUP * D_FF_E, (g + 1) * EXP_PER_GROUP * D_FF_E)
        term = _dot(hid_g, wd_ref[gs, :])
        ffn = term if ffn is None else ffn + term
    gate_f = mod_ref[:, :, 2 * D_MODEL:3 * D_MODEL]
    o_ref[...] = _layer_norm(ALPHA * x_ref[...] + (1.0 + gate_f) * ffn.reshape(sb, rb, D_MODEL),
                             lng_ref[...], lnb_ref[...])


def _moe_call(x3, mods, lw):
    n_seq, rows, _ = x3.shape
    sb, rb = _row_blocks(n_seq, rows)
    row_spec = pl.BlockSpec((sb, rb, D_MODEL), lambda i, j: (i, j, 0))
    vec_spec = pl.BlockSpec((1, D_MODEL), lambda i, j: (0, 0))
    return pl.pallas_call(
        _moe_kernel,
        grid=(n_seq // sb, rows // rb),
        in_specs=[row_spec, pl.BlockSpec((sb, 1, 3 * D_MODEL), lambda i, j: (i, 0, 0)),
                  _vmem_full(), pl.BlockSpec((1, LANES), lambda i, j: (0, 0)),
                  _vmem_full(), _vmem_full(), _vmem_full(), vec_spec, vec_spec],
        out_specs=row_spec,
        out_shape=jax.ShapeDtypeStruct(x3.shape, F32),
        compiler_params=_compiler_params(2),
        name="moe_norm",
    )(x3, mods, lw["w_rt"], lw["b_rt"], lw["w_e_gate"], lw["w_e_up"], lw["w_e_down"],
      lw["ln2_g"], lw["ln2_b"])


def _split_w_in(w):
    sizes = (D_MODEL, SSD_CONV_DIM, SSD_HEADS, D_MODEL, D_MODEL, D_MODEL, M_HEADS, M_HEADS, D_MODEL,
             D_MODEL, 3 * D_MODEL)
    out, off = [], 0
    for s in sizes:
        out.append(w[:, off:off + s])
        off += s
    return out


def _pad_lanes(v, width=LANES):
    return jnp.pad(v, ((0, 0), (0, width - v.shape[-1])))


def _layer_weights(l, p):
    wz, wxbc, wdt, wq, wk, wv, wi, wf, wo, wup, wgl = _split_w_in(p["w_in"][l])
    w_small = _pad_lanes(jnp.concatenate([wdt, wdt, wf, wi], axis=1))
    gate_b = p["mlstm_gate_b"][l]
    bias_row = jnp.concatenate([p["ssd_dt_bias"][l], p["ssd_dt_bias"][l], gate_b[M_HEADS:], gate_b[:M_HEADS]])
    head_params = jnp.concatenate([
        _pad_lanes(bias_row[None]), _pad_lanes(p["ssd_A_log"][l][None]), _pad_lanes(p["ssd_D"][l][None]),
        jnp.zeros((SUBLANES - 3, LANES), F32)], axis=0)
    w_rt = jnp.concatenate([_pad_lanes(p["w_rt_group"][l], RT_E), _pad_lanes(p["w_rt_expert"][l], LANES - RT_E)],
                           axis=1)
    b_rt = jnp.concatenate([_pad_lanes(p["b_rt_group"][l][None], RT_E),
                            _pad_lanes(p["b_rt_expert"][l][None], LANES - RT_E)], axis=1)
    stack_cols = lambda w: jnp.transpose(w, (1, 0, 2)).reshape(D_MODEL, N_EXPERTS * D_FF_E)
    row = lambda v: v[None]
    return dict(
        w_cat=jnp.concatenate([wz, wxbc, w_small, wq, wk, wv, wo, wup], axis=1).astype(BF16),
        w_gl=wgl.astype(BF16),
        conv_w=p["conv_w"][l], conv_b=row(p["conv_b"][l]), head_params=head_params,
        ssd_norm_w=row(p["ssd_norm_w"][l]), mlstm_norm_w=row(p["mlstm_norm_w"][l]),
        pool_w=p["pool_w"][l].astype(BF16), pool_scale=row(p["pool_scale"][l]),
        gate_b=row(p["gate_b"][l]),
        w_br_ssd=p["w_br_ssd"][l].astype(BF16), w_br_mlstm=p["w_br_mlstm"][l].astype(BF16),
        w_br_pool=p["w_br_pool"][l].astype(BF16), w_out=p["w_out"][l].astype(BF16),
        ln1_g=row(p["ln1_g"][l]), ln1_b=row(p["ln1_b"][l]),
        w_rt=w_rt, b_rt=b_rt,
        w_e_gate=stack_cols(p["w_e_gate"][l]).astype(BF16), w_e_up=stack_cols(p["w_e_up"][l]).astype(BF16),
        w_e_down=p["w_e_down"][l].reshape(N_EXPERTS * D_FF_E, D_MODEL).astype(BF16),
        ln2_g=row(p["ln2_g"][l]), ln2_b=row(p["ln2_b"][l]),
    )


def _decoder_layer(x3, mods_mix, mods_ffn, states, lw, *, L, tv, pool_valid):
    proj = _proj_call(x3, mods_mix, lw["w_cat"])
    outs = _mixer_call(proj, states, lw, L=L, tv=tv, pool_valid=pool_valid)
    x3 = _merge_call(x3, mods_mix, outs[:3], lw)
    x3 = _moe_call(x3, mods_ffn, lw)
    return x3, outs[3:]


def _flat_states(ssd, conv, mc, mn, mm, pool):
    n = ssd.shape[0]
    return (ssd.reshape(n, SSD_HEADS * SSD_HEADDIM, SSD_STATE), conv,
            mc.reshape(n, M_HEADS * M_HEADDIM, M_HEADDIM), mn, mm.reshape(n, 1, M_HEADS), pool)


def _unflat_states(ssd, conv, mc, mn, mm, pool):
    n = ssd.shape[0]
    return (ssd.reshape(n, SSD_HEADS, SSD_HEADDIM, SSD_STATE), conv,
            mc.reshape(n, M_HEADS, M_HEADDIM, M_HEADDIM), mn, mm.reshape(n, M_HEADS), pool)


def kernel(x_prompt, x_sample, state_ssd, state_conv, state_mlstm_C, state_mlstm_n, state_mlstm_m, state_pool, c_prompt, c_sample, w_ada_mix, b_ada_mix, w_in, conv_w, conv_b, ssd_A_log, ssd_dt_bias, ssd_D, ssd_norm_w, mlstm_gate_b, mlstm_norm_w, pool_w, pool_scale, gate_b, w_br_ssd, w_br_mlstm, w_br_pool, w_out, ln1_g, ln1_b, w_ada_ffn, b_ada_ffn, w_rt_group, b_rt_group, w_rt_expert, b_rt_expert, w_e_gate, w_e_up, w_e_down, ln2_g, ln2_b):
    params = dict(w_in=w_in, conv_w=conv_w, conv_b=conv_b, ssd_A_log=ssd_A_log, ssd_dt_bias=ssd_dt_bias,
                  ssd_D=ssd_D, ssd_norm_w=ssd_norm_w, mlstm_gate_b=mlstm_gate_b, mlstm_norm_w=mlstm_norm_w,
                  pool_w=pool_w, pool_scale=pool_scale, gate_b=gate_b, w_br_ssd=w_br_ssd,
                  w_br_mlstm=w_br_mlstm, w_br_pool=w_br_pool, w_out=w_out, ln1_g=ln1_g, ln1_b=ln1_b,
                  w_rt_group=w_rt_group, b_rt_group=b_rt_group, w_rt_expert=w_rt_expert,
                  b_rt_expert=b_rt_expert, w_e_gate=w_e_gate, w_e_up=w_e_up, w_e_down=w_e_down,
                  ln2_g=ln2_g, ln2_b=ln2_b)
    bp, seq, _ = x_prompt.shape
    bs, dec_seq, _ = x_sample.shape
    assert seq % CHUNK == 0 and 1 <= dec_seq <= SAMPLE_ROWS

    c_all = jnp.concatenate([c_prompt, c_sample], axis=0)
    mods_mix = _ada_call(c_all, w_ada_mix, b_ada_mix[:, None, :])
    mods_ffn = _ada_call(c_all, w_ada_ffn, b_ada_ffn[:, None, :])

    st_dt = state_ssd.dtype
    prompt_init = _flat_states(
        jnp.zeros((bp, SSD_HEADS, SSD_HEADDIM, SSD_STATE), st_dt),
        jnp.zeros((bp, SSD_CONV - 1, SSD_CONV_DIM), x_prompt.dtype),
        jnp.zeros((bp, M_HEADS, M_HEADDIM, M_HEADDIM), st_dt),
        jnp.zeros((bp, M_HEADS, M_HEADDIM), st_dt),
        jnp.zeros((bp, M_HEADS), st_dt),
        jnp.zeros((bp, POOL_BUF, D_MODEL), x_prompt.dtype))

    xp = x_prompt
    xs = jnp.pad(x_sample, ((0, 0), (0, SAMPLE_ROWS - dec_seq), (0, 0)))
    p_new, s_new = [], []
    for l in range(DEPTH):
        lw = _layer_weights(l, params)
        xp, sp = _decoder_layer(xp, mods_mix[l, :bp, None, :], mods_ffn[l, :bp, None, :], prompt_init, lw,
                                L=CHUNK, tv=CHUNK, pool_valid=0)
        sample_states = _flat_states(state_ssd[l], state_conv[l], state_mlstm_C[l], state_mlstm_n[l],
                                     state_mlstm_m[l], state_pool[l])
        xs, ss = _decoder_layer(xs, mods_mix[l, bp:, None, :], mods_ffn[l, bp:, None, :], sample_states, lw,
                                L=SAMPLE_ROWS, tv=dec_seq, pool_valid=POOL_BUF)
        p_new.append(_unflat_states(*sp))
        s_new.append(_unflat_states(*ss))
    ssd_p, conv_p, mc_p, mn_p, mm_p, pool_p = [jnp.stack(a) for a in zip(*p_new)]
    ssd_s, conv_s, mc_s, mn_s, mm_s, pool_s = [jnp.stack(a) for a in zip(*s_new)]
    return (xp, xs[:, :dec_seq], ssd_p, conv_p, mc_p, mn_p, mm_p, pool_p,
            ssd_s, conv_s, mc_s, mn_s, mm_s, pool_s)
```

```python
import functools

import jax
import jax.numpy as jnp
from jax import lax
from jax.experimental import pallas as pl
from jax.experimental.pallas import tpu as pltpu

F32 = jnp.float32
BF16 = jnp.bfloat16

D_MODEL = 1024
DEPTH = 4
SSD_HEADS = 16
SSD_HEADDIM = 64
SSD_GROUPS = 2
SSD_REP = SSD_HEADS // SSD_GROUPS
SSD_STATE = 128
SSD_CONV = 4
SSD_CONV_DIM = D_MODEL + 2 * SSD_GROUPS * SSD_STATE
CHUNK = 128
M_HEADS = 4
M_HEADDIM = D_MODEL // M_HEADS
POOL_WINDOWS = (2, 4, 8, 16)
POOL_GW = D_MODEL // len(POOL_WINDOWS)
POOL_BUF = max(POOL_WINDOWS) - 1
N_EGROUPS = 4
EXP_PER_GROUP = 4
N_EXPERTS = N_EGROUPS * EXP_PER_GROUP
D_FF_E = D_MODEL // 4
ALPHA = (2 * DEPTH) ** 0.25
LN_EPS = 1e-5
RMS_EPS = 1e-6

SUBLANES = 8
LANES = 128
VMEM_LIMIT_BYTES = 56 * 1024 * 1024

OFF_Z = 0
OFF_XBC = OFF_Z + D_MODEL
OFF_SMALL = OFF_XBC + SSD_CONV_DIM
OFF_Q = OFF_SMALL + LANES
OFF_K = OFF_Q + D_MODEL
OFF_V = OFF_K + D_MODEL
OFF_O = OFF_V + D_MODEL
OFF_UP = OFF_O + D_MODEL
N_PROJ = OFF_UP + D_MODEL
SM_A, SM_DT, SM_F, SM_I, SM_END = 0, SSD_HEADS, 2 * SSD_HEADS, 2 * SSD_HEADS + M_HEADS, 2 * SSD_HEADS + 2 * M_HEADS
RT_G, RT_E = 0, 16
NEG_BIG = -1e30

ROW_TILE = 256
SAMPLE_ROWS = 8
SAMPLE_SEQ_BLOCK = 4

STATE_SHAPES = ((SSD_HEADS * SSD_HEADDIM, SSD_STATE), (SSD_CONV - 1, SSD_CONV_DIM),
                (M_HEADS * M_HEADDIM, M_HEADDIM), (M_HEADS, M_HEADDIM), (1, M_HEADS), (POOL_BUF, D_MODEL))
N_STATES = len(STATE_SHAPES)


def _dot(a, b):
    return jnp.dot(a.astype(BF16), b.astype(BF16), preferred_element_type=F32)


def _dot_nt(a, b):
    return lax.dot_general(a.astype(BF16), b.astype(BF16), (((1,), (1,)), ((), ())),
                           preferred_element_type=F32)


def _dot_tn(a, b):
    return lax.dot_general(a.astype(BF16), b.astype(BF16), (((0,), (0,)), ((), ())),
                           preferred_element_type=F32)


def _dot_exact(a, b):
    return jnp.dot(a, b, precision=lax.Precision.HIGHEST, preferred_element_type=F32)


def _dot_nt_exact(a, b):
    return lax.dot_general(a, b, (((1,), (1,)), ((), ())), precision=lax.Precision.HIGHEST,
                           preferred_element_type=F32)


def _sigmoid(x):
    return 1.0 / (1.0 + jnp.exp(-x))


def _silu(x):
    return x * _sigmoid(x)


def _softplus(x):
    return jnp.maximum(x, 0.0) + jnp.log1p(jnp.exp(-jnp.abs(x)))


def _layer_norm(x, g, b):
    mu = jnp.mean(x, axis=-1, keepdims=True)
    xc = x - mu
    var = jnp.mean(xc * xc, axis=-1, keepdims=True)
    return xc * lax.rsqrt(var + LN_EPS) * g + b


def _compiler_params(n_grid):
    return pltpu.CompilerParams(dimension_semantics=("arbitrary",) * n_grid,
                                vmem_limit_bytes=VMEM_LIMIT_BYTES)


def _vmem_full():
    return pl.BlockSpec(memory_space=pltpu.VMEM)


def _const_spec(shape):
    return pl.BlockSpec(shape, lambda *_: (0,) * len(shape))


def _ada_kernel(c_ref, w_ref, b_ref, o_ref):
    o_ref[...] = _dot(c_ref[...], w_ref[...]) + b_ref[...]


def _ada_call(c_all, w, b):
    n = c_all.shape[0]
    return pl.pallas_call(
        _ada_kernel,
        grid=(DEPTH, 3),
        in_specs=[pl.BlockSpec((n, D_MODEL), lambda l, j: (0, 0)),
                  pl.BlockSpec((None, D_MODEL, D_MODEL), lambda l, j: (l, 0, j)),
                  pl.BlockSpec((None, 1, D_MODEL), lambda l, j: (l, 0, j))],
        out_specs=pl.BlockSpec((None, n, D_MODEL), lambda l, j: (l, 0, j)),
        out_shape=jax.ShapeDtypeStruct((DEPTH, n, 3 * D_MODEL), F32),
        compiler_params=_compiler_params(2),
        name="ada_mod",
    )(c_all, w, b)


def _modulate(x_ref, mod_ref):
    x = x_ref[...]
    shift = mod_ref[:, :, 0:D_MODEL]
    scale = mod_ref[:, :, D_MODEL:2 * D_MODEL]
    u = x * (1.0 + scale) + shift
    return u.reshape(x.shape[0] * x.shape[1], D_MODEL)


def _proj_kernel(x_ref, mod_ref, w_ref, o_ref):
    sb, rb, _ = x_ref.shape
    u = _modulate(x_ref, mod_ref).astype(BF16)
    col = 0
    while col < N_PROJ:
        width = min(D_MODEL, N_PROJ - col)
        o_ref[:, :, col:col + width] = _dot(u, w_ref[:, col:col + width]).reshape(sb, rb, width)
        col += width


def _row_blocks(n_seq, rows):
    if rows >= ROW_TILE:
        return 1, ROW_TILE
    return ROW_TILE // rows, rows


def _proj_call(x3, mods, wcat):
    n_seq, rows, _ = x3.shape
    sb, rb = _row_blocks(n_seq, rows)
    return pl.pallas_call(
        _proj_kernel,
        grid=(n_seq // sb, rows // rb),
        in_specs=[pl.BlockSpec((sb, rb, D_MODEL), lambda i, j: (i, j, 0)),
                  pl.BlockSpec((sb, 1, 3 * D_MODEL), lambda i, j: (i, 0, 0)),
                  _vmem_full()],
        out_specs=pl.BlockSpec((sb, rb, N_PROJ), lambda i, j: (i, j, 0)),
        out_shape=jax.ShapeDtypeStruct((n_seq, rows, N_PROJ), F32),
        compiler_params=_compiler_params(2),
        name="in_proj",
    )(x3, mods, wcat)


def _mixer_chunk(proj, st_in, st_out, par, scr, *, L, tv, pos0):
    ssd_i, conv_i, mc_i, mn_i, mm_i, pool_i = st_in
    ssd_o, conv_o, mc_o, mn_o, mm_o, pool_o = st_out
    convw_ref, convb_ref, hp_ref, snw_ref, mnw_ref, poolw_ref, pscale_ref = par
    xext_ref, pext_ref, yacc_ref = scr

    row_l = lax.broadcasted_iota(jnp.int32, (L, L), 0)
    col_l = lax.broadcasted_iota(jnp.int32, (L, L), 1)
    causal = row_l >= col_l

    lane = lax.broadcasted_iota(jnp.int32, (L, LANES), 1)
    pre = proj(OFF_SMALL, LANES) + hp_ref[0:1, :]
    sp = _softplus(pre)
    a_row = -jnp.exp(hp_ref[1:2, :])
    pmat = jnp.where(lane < SM_DT, sp * a_row,
                     jnp.where(lane < SM_F, sp,
                               jnp.where(lane < SM_I, -_softplus(-pre),
                                         jnp.where(lane < SM_END, pre, 0.0))))
    if tv < L:
        row = lax.broadcasted_iota(jnp.int32, (L, LANES), 0)
        pad = jnp.where(lane < SM_I, 0.0, jnp.where(lane < SM_END, NEG_BIG, 0.0))
        pmat = jnp.where(row < tv, pmat, pad)
    tri = causal.astype(F32)
    eye = (lax.broadcasted_iota(jnp.int32, (LANES, LANES), 0)
           == lax.broadcasted_iota(jnp.int32, (LANES, LANES), 1)).astype(F32)
    cum = _dot_exact(tri, pmat)
    pmat_t = _dot_nt_exact(eye, pmat)
    cum_t = _dot_nt_exact(eye, cum)

    xext_ref[SUBLANES - (SSD_CONV - 1):SUBLANES, :] = conv_i[...]
    xext_ref[SUBLANES:SUBLANES + L, :] = proj(OFF_XBC, SSD_CONV_DIM)
    acc = convb_ref[...]
    for k in range(SSD_CONV):
        start = SUBLANES - (SSD_CONV - 1) + k
        acc = acc + xext_ref[start:start + L, :] * convw_ref[k:k + 1, :]
    conv_o[...] = xext_ref[SUBLANES + tv - (SSD_CONV - 1):SUBLANES + tv, :]
    xbc = _silu(acc)
    xs = xbc[:, 0:D_MODEL]
    d_row = hp_ref[2:3, :]
    for g in range(SSD_GROUPS):
        bm = xbc[:, D_MODEL + g * SSD_STATE:D_MODEL + (g + 1) * SSD_STATE]
        cm = xbc[:, D_MODEL + (SSD_GROUPS + g) * SSD_STATE:D_MODEL + (SSD_GROUPS + g + 1) * SSD_STATE]
        cb = _dot_nt(cm, bm)
        for r in range(SSD_REP):
            h = g * SSD_REP + r
            hs = slice(h * SSD_HEADDIM, (h + 1) * SSD_HEADDIM)
            cum_c = cum[:, SM_A + h:SM_A + h + 1]
            cum_r = cum_t[SM_A + h:SM_A + h + 1, :]
            dt_c = pmat[:, SM_DT + h:SM_DT + h + 1]
            dt_r = pmat_t[SM_DT + h:SM_DT + h + 1, :]
            last = cum[L - 1:L, SM_A + h:SM_A + h + 1]
            seg = jnp.where(causal, cum_c - cum_r, -jnp.inf)
            wmat = cb * jnp.exp(seg) * dt_r
            x_h = xs[:, hs]
            s_h = ssd_i[hs, :]
            y = _dot(wmat, x_h) + _dot_nt(cm, s_h) * jnp.exp(cum_c)
            yacc_ref[:, hs] = y + d_row[:, h:h + 1] * x_h
            ws = jnp.exp(last - cum_c) * dt_c
            ssd_o[hs, :] = jnp.exp(last) * s_h + _dot_tn(x_h * ws, bm)
    yz = yacc_ref[...] * _silu(proj(OFF_Z, D_MODEL))
    y_ssd = yz * lax.rsqrt(jnp.mean(yz * yz, axis=-1, keepdims=True) + RMS_EPS) * snw_ref[...]

    ym_parts = []
    for h in range(M_HEADS):
        hs = slice(h * M_HEADDIM, (h + 1) * M_HEADDIM)
        q_h = proj(OFF_Q + h * M_HEADDIM, M_HEADDIM)
        k_h = proj(OFF_K + h * M_HEADDIM, M_HEADDIM) * (M_HEADDIM ** -0.5)
        v_h = proj(OFF_V + h * M_HEADDIM, M_HEADDIM)
        o_h = proj(OFF_O + h * M_HEADDIM, M_HEADDIM)
        b_c = cum[:, SM_F + h:SM_F + h + 1]
        b_r = cum_t[SM_F + h:SM_F + h + 1, :]
        i_c = pmat[:, SM_I + h:SM_I + h + 1]
        i_r = pmat_t[SM_I + h:SM_I + h + 1, :]
        m_prev = mm_i[:, h:h + 1]
        c_h = mc_i[hs, :]
        n_h = mn_i[h:h + 1, :]
        dmat = jnp.where(causal, b_c - b_r + i_r, -jnp.inf)
        m_st = b_c + m_prev
        m = jnp.maximum(m_st, jnp.max(dmat, axis=-1, keepdims=True))
        wts = jnp.exp(dmat - m) * _dot_nt(q_h, k_h)
        ws = jnp.exp(m_st - m)
        num = _dot(wts, v_h) + ws * _dot_nt(q_h, c_h)
        den = jnp.sum(wts, axis=-1, keepdims=True) + ws * jnp.sum(q_h * n_h, axis=-1, keepdims=True)
        hc = num / jnp.maximum(jnp.abs(den), jnp.exp(-m))
        last_b = cum[L - 1:L, SM_F + h:SM_F + h + 1]
        m_new = m[L - 1:L, :]
        wsrc = jnp.exp(last_b - b_c + i_c - m_new)
        wprev = jnp.exp(last_b + m_prev - m_new)
        mc_o[hs, :] = wprev * c_h + _dot_tn(v_h * wsrc, k_h)
        mn_o[h:h + 1, :] = wprev * n_h + jnp.sum(k_h * wsrc, axis=0, keepdims=True)
        mm_o[:, h:h + 1] = m_new
        mu = jnp.mean(hc, axis=-1, keepdims=True)
        hd = hc - mu
        var = jnp.mean(hd * hd, axis=-1, keepdims=True)
        ym_parts.append(hd * lax.rsqrt(var + LN_EPS) * mnw_ref[:, hs] * _sigmoid(o_h))
    y_m = jnp.concatenate(ym_parts, axis=-1)

    pext_ref[1:POOL_BUF + 1, :] = pool_i[...]
    up = proj(OFF_UP, D_MODEL)
    pext_ref[POOL_BUF + 1:POOL_BUF + 1 + L, :] = up
    pos = lax.broadcasted_iota(jnp.int32, (L, 1), 0) + pos0
    yp_parts = []
    for g, w in enumerate(POOL_WINDOWS):
        gs = slice(g * POOL_GW, (g + 1) * POOL_GW)
        wsum = up[:, gs]
        for j in range(1, w):
            wsum = wsum + pext_ref[POOL_BUF + 1 - j:POOL_BUF + 1 - j + L, gs]
        cnt = jnp.minimum(pos, w).astype(F32)
        dlt = wsum / cnt - up[:, gs]
        yp_parts.append(_dot(dlt, poolw_ref[g]) * pscale_ref[:, gs])
    pool_o[...] = pext_ref[1 + tv:1 + tv + POOL_BUF, :]
    return y_ssd, y_m, jnp.concatenate(yp_parts, axis=-1)


def _merge_rows(ub, ys, wgl_ref, gb_ref, wb_refs, wout_ref):
    merged = None
    for i, (y, wb_ref) in enumerate(zip(ys, wb_refs)):
        cs = slice(i * D_MODEL, (i + 1) * D_MODEL)
        gate = _sigmoid(_dot(ub, wgl_ref[:, cs]) + gb_ref[:, cs])
        term = gate * _dot(y, wb_ref[...])
        merged = term if merged is None else merged + term
    return _dot(merged, wout_ref[...])


N_MIXER_PARAMS = 7


def _mixer_param_specs():
    return [_const_spec((SSD_CONV, SSD_CONV_DIM)), _const_spec((1, SSD_CONV_DIM)),
            _const_spec((SUBLANES, LANES)), _const_spec((1, D_MODEL)), _const_spec((1, D_MODEL)),
            _const_spec((len(POOL_WINDOWS), POOL_GW, POOL_GW)), _const_spec((1, D_MODEL))]


def _mixer_param_args(lw):
    return (lw["conv_w"], lw["conv_b"], lw["head_params"], lw["ssd_norm_w"], lw["mlstm_norm_w"],
            lw["pool_w"], lw["pool_scale"])


def _stacked_state_shapes(n_seq):
    return [jax.ShapeDtypeStruct((DEPTH, n_seq) + shp, F32) for shp in STATE_SHAPES]


def _alias_args(prev_states, n_inputs_before, n_outputs_before):
    if prev_states is None:
        return [], [], {}
    specs = [pl.BlockSpec(memory_space=pl.ANY)] * N_STATES
    aliases = {n_inputs_before + k: n_outputs_before + k for k in range(N_STATES)}
    return list(prev_states), specs, aliases


def _prompt_mixer_kernel(x_ref, mod_ref, wcat_ref, wgl_ref, gb_ref, wbs_ref, wbm_ref, wbp_ref, wout_ref,
                         lng_ref, lnb_ref, *rest, L, n_alias):
    par = rest[:N_MIXER_PARAMS]
    rest = rest[N_MIXER_PARAMS + n_alias:]
    o_ref = rest[0]
    states = rest[1:1 + N_STATES]
    scr = rest[1 + N_STATES:]
    c = pl.program_id(1)

    @pl.when(c == 0)
    def _fresh_prompt_states():
        for ref in states:
            ref[...] = jnp.zeros(ref.shape, ref.dtype)

    x = x_ref[...]
    u = x * (1.0 + mod_ref[:, D_MODEL:2 * D_MODEL]) + mod_ref[:, 0:D_MODEL]
    ub = u.astype(BF16)
    proj = lambda off, width: _dot(ub, wcat_ref[:, off:off + width])
    ys = _mixer_chunk(proj, states, states, par, scr, L=L, tv=L, pos0=c * L + 1)
    mix = _merge_rows(ub, ys, wgl_ref, gb_ref, (wbs_ref, wbm_ref, wbp_ref), wout_ref)
    o_ref[...] = _layer_norm(ALPHA * x + (1.0 + mod_ref[:, 2 * D_MODEL:3 * D_MODEL]) * mix,
                             lng_ref[...], lnb_ref[...])


def _prompt_mixer_call(l, x3, mods, prev_states, lw):
    n_seq, rows, _ = x3.shape
    L = CHUNK
    row_spec = pl.BlockSpec((None, L, D_MODEL), lambda b, c: (b, c, 0))
    vec_spec = _const_spec((1, D_MODEL))
    state_specs = [pl.BlockSpec((None, None) + shp, lambda b, c: (l, b, 0, 0)) for shp in STATE_SHAPES]
    in_specs = [row_spec, pl.BlockSpec((None, 1, 3 * D_MODEL), lambda b, c: (b, 0, 0)),
                _vmem_full(), _vmem_full(), _const_spec((1, 3 * D_MODEL)),
                _vmem_full(), _vmem_full(), _vmem_full(), _vmem_full(), vec_spec, vec_spec]
    in_specs += _mixer_param_specs()
    alias_in, alias_specs, aliases = _alias_args(prev_states, len(in_specs), 1)
    outs = pl.pallas_call(
        functools.partial(_prompt_mixer_kernel, L=L, n_alias=len(alias_in)),
        grid=(n_seq, rows // L),
        in_specs=in_specs + alias_specs,
        out_specs=[row_spec] + state_specs,
        out_shape=[jax.ShapeDtypeStruct(x3.shape, F32)] + _stacked_state_shapes(n_seq),
        scratch_shapes=[pltpu.VMEM((SUBLANES + L, SSD_CONV_DIM), F32),
                        pltpu.VMEM((POOL_BUF + 1 + L, D_MODEL), F32),
                        pltpu.VMEM((L, D_MODEL), F32)],
        input_output_aliases=aliases,
        compiler_params=_compiler_params(2),
        name="prompt_mixers",
    )(x3, mods, lw["w_cat"], lw["w_gl"], lw["gate_b"], lw["w_br_ssd"], lw["w_br_mlstm"], lw["w_br_pool"],
      lw["w_out"], lw["ln1_g"], lw["ln1_b"], *_mixer_param_args(lw), *alias_in)
    return outs[0], outs[1:]


def _sample_mixer_kernel(proj_ref, *rest, n_blk, L, tv, n_alias):
    st_in = rest[:N_STATES]
    par = rest[N_STATES:N_STATES + N_MIXER_PARAMS]
    rest = rest[N_STATES + N_MIXER_PARAMS + n_alias:]
    y_refs = rest[:3]
    st_out = rest[3:3 + N_STATES]
    scr = rest[3 + N_STATES:]
    for s in range(n_blk):
        proj = lambda off, width, s=s: proj_ref[s, :, off:off + width]
        ys = _mixer_chunk(proj, [r.at[s] for r in st_in], [r.at[s] for r in st_out], par,
                          [r.at[s] for r in scr], L=L, tv=tv, pos0=1 + POOL_BUF)
        for y_ref, y in zip(y_refs, ys):
            y_ref[s] = y


def _sample_mixer_call(l, proj, states_in, prev_states, lw, *, tv):
    n_seq, L, _ = proj.shape
    nb = SAMPLE_SEQ_BLOCK
    state_specs = [pl.BlockSpec((None, nb) + shp, lambda i: (l, i, 0, 0)) for shp in STATE_SHAPES]
    y_spec = pl.BlockSpec((nb, L, D_MODEL), lambda i: (i, 0, 0))
    y_shape = jax.ShapeDtypeStruct((n_seq, L, D_MODEL), F32)
    in_specs = [pl.BlockSpec((nb, L, N_PROJ), lambda i: (i, 0, 0))] + state_specs + _mixer_param_specs()
    alias_in, alias_specs, aliases = _alias_args(prev_states, len(in_specs), 3)
    outs = pl.pallas_call(
        functools.partial(_sample_mixer_kernel, n_blk=nb, L=L, tv=tv, n_alias=len(alias_in)),
        grid=(n_seq // nb,),
        in_specs=in_specs + alias_specs,
        out_specs=[y_spec, y_spec, y_spec] + state_specs,
        out_shape=[y_shape, y_shape, y_shape] + _stacked_state_shapes(n_seq),
        scratch_shapes=[pltpu.VMEM((nb, SUBLANES + L, SSD_CONV_DIM), F32),
                        pltpu.VMEM((nb, POOL_BUF + 1 + L, D_MODEL), F32),
                        pltpu.VMEM((nb, L, D_MODEL), F32)],
        input_output_aliases=aliases,
        compiler_params=_compiler_params(1),
        name="sample_mixers",
    )(proj, *states_in, *_mixer_param_args(lw), *alias_in)
    return outs[:3], outs[3:]


def _merge_kernel(x_ref, mod_ref, yssd_ref, ym_ref, ypool_ref, wgl_ref, gb_ref,
                  wbs_ref, wbm_ref, wbp_ref, wout_ref, lng_ref, lnb_ref, o_ref):
    sb, rb, _ = x_ref.shape
    n = sb * rb
    ub = _modulate(x_ref, mod_ref).astype(BF16)
    ys = [r[...].reshape(n, D_MODEL) for r in (yssd_ref, ym_ref, ypool_ref)]
    mix = _merge_rows(ub, ys, wgl_ref, gb_ref, (wbs_ref, wbm_ref, wbp_ref), wout_ref).reshape(sb, rb, D_MODEL)
    gate_a = mod_ref[:, :, 2 * D_MODEL:3 * D_MODEL]
    o_ref[...] = _layer_norm(ALPHA * x_ref[...] + (1.0 + gate_a) * mix, lng_ref[...], lnb_ref[...])


def _merge_call(x3, mods, ys, lw):
    n_seq, rows, _ = x3.shape
    sb, rb = _row_blocks(n_seq, rows)
    row_spec = pl.BlockSpec((sb, rb, D_MODEL), lambda i, j: (i, j, 0))
    vec_spec = _const_spec((1, D_MODEL))
    return pl.pallas_call(
        _merge_kernel,
        grid=(n_seq // sb, rows // rb),
        in_specs=[row_spec, pl.BlockSpec((sb, 1, 3 * D_MODEL), lambda i, j: (i, 0, 0)),
                  row_spec, row_spec, row_spec,
                  _vmem_full(), _const_spec((1, 3 * D_MODEL)),
                  _vmem_full(), _vmem_full(), _vmem_full(), _vmem_full(), vec_spec, vec_spec],
        out_specs=row_spec,
        out_shape=jax.ShapeDtypeStruct(x3.shape, F32),
        compiler_params=_compiler_params(2),
        name="merge_norm",
    )(x3, mods, ys[0], ys[1], ys[2], lw["w_gl"], lw["gate_b"], lw["w_br_ssd"], lw["w_br_mlstm"],
      lw["w_br_pool"], lw["w_out"], lw["ln1_g"], lw["ln1_b"])


def _route(logits):
    n = logits.shape[0]
    lane = lax.broadcasted_iota(jnp.int32, (n, LANES), 1)
    is_g = (lane >= RT_G) & (lane < RT_G + N_EGROUPS)
    lg = jnp.where(is_g, logits, -jnp.inf)
    g_max = jnp.max(lg, axis=-1, keepdims=True)
    g_idx = jnp.min(jnp.where(lg == g_max, lane - RT_G, LANES), axis=-1, keepdims=True)
    g_prob = 1.0 / jnp.sum(jnp.exp(lg - g_max), axis=-1, keepdims=True)
    lo = RT_E + g_idx * EXP_PER_GROUP
    le = jnp.where((lane >= lo) & (lane < lo + EXP_PER_GROUP), logits, -jnp.inf)
    v1 = jnp.max(le, axis=-1, keepdims=True)
    i1 = jnp.min(jnp.where(le == v1, lane, LANES), axis=-1, keepdims=True)
    le2 = jnp.where(lane == i1, -jnp.inf, le)
    v2 = jnp.max(le2, axis=-1, keepdims=True)
    i2 = jnp.min(jnp.where(le2 == v2, lane, LANES), axis=-1, keepdims=True)
    e2 = jnp.exp(v2 - v1)
    p1 = g_prob / (1.0 + e2)
    p2 = g_prob * e2 / (1.0 + e2)
    return jnp.where(lane == i1, p1, 0.0) + jnp.where(lane == i2, p2, 0.0)


def _moe_kernel(x_ref, mod_ref, wrt_ref, brt_ref, wg_ref, wu_ref, wd_ref, lng_ref, lnb_ref, o_ref):
    sb, rb, _ = x_ref.shape
    u = _modulate(x_ref, mod_ref)
    wts = _route(_dot_exact(u, wrt_ref[...]) + brt_ref[...])
    ub = u.astype(BF16)
    ffn = None
    for g in range(N_EGROUPS):
        parts = []
        for r in range(EXP_PER_GROUP):
            e = g * EXP_PER_GROUP + r
            es = slice(e * D_FF_E, (e + 1) * D_FF_E)
            hid = _silu(_dot(ub, wg_ref[:, es])) * _dot(ub, wu_ref[:, es])
            parts.append(hid * wts[:, RT_E + e:RT_E + e + 1])
        hid_g = jnp.concatenate(parts, axis=-1)
        gs = slice(g * EXP_PER_GROUP * D_FF_E, (g + 1) * EXP_PER_GROUP * D_FF_E)
        term = _dot(hid_g, wd_ref[gs, :])
        ffn = term if ffn is None else ffn + term
    gate_f = mod_ref[:, :, 2 * D_MODEL:3 * D_MODEL]
    o_ref[...] = _layer_norm(ALPHA * x_ref[...] + (1.0 + gate_f) * ffn.reshape(sb, rb, D_MODEL),
                             lng_ref[...], lnb_ref[...])


def _moe_call(x3, mods, lw):
    n_seq, rows, _ = x3.shape
    sb, rb = _row_blocks(n_seq, rows)
    row_spec = pl.BlockSpec((sb, rb, D_MODEL), lambda i, j: (i, j, 0))
    vec_spec = _const_spec((1, D_MODEL))
    return pl.pallas_call(
        _moe_kernel,
        grid=(n_seq // sb, rows // rb),
        in_specs=[row_spec, pl.BlockSpec((sb, 1, 3 * D_MODEL), lambda i, j: (i, 0, 0)),
                  _vmem_full(), _const_spec((1, LANES)),
                  _vmem_full(), _vmem_full(), _vmem_full(), vec_spec, vec_spec],
        out_specs=row_spec,
        out_shape=jax.ShapeDtypeStruct(x3.shape, F32),
        compiler_params=_compiler_params(2),
        name="moe_norm",
    )(x3, mods, lw["w_rt"], lw["b_rt"], lw["w_e_gate"], lw["w_e_up"], lw["w_e_down"],
      lw["ln2_g"], lw["ln2_b"])


def _split_w_in(w):
    sizes = (D_MODEL, SSD_CONV_DIM, SSD_HEADS, D_MODEL, D_MODEL, D_MODEL, M_HEADS, M_HEADS, D_MODEL,
             D_MODEL, 3 * D_MODEL)
    out, off = [], 0
    for s in sizes:
        out.append(w[:, off:off + s])
        off += s
    return out


def _pad_lanes(v, width=LANES):
    return jnp.pad(v, ((0, 0), (0, width - v.shape[-1])))


def _layer_weights(l, p):
    wz, wxbc, wdt, wq, wk, wv, wi, wf, wo, wup, wgl = _split_w_in(p["w_in"][l])
    w_small = _pad_lanes(jnp.concatenate([wdt, wdt, wf, wi], axis=1))
    gate_b = p["mlstm_gate_b"][l]
    bias_row = jnp.concatenate([p["ssd_dt_bias"][l], p["ssd_dt_bias"][l], gate_b[M_HEADS:], gate_b[:M_HEADS]])
    head_params = jnp.concatenate([
        _pad_lanes(bias_row[None]), _pad_lanes(p["ssd_A_log"][l][None]), _pad_lanes(p["ssd_D"][l][None]),
        jnp.zeros((SUBLANES - 3, LANES), F32)], axis=0)
    w_rt = jnp.concatenate([_pad_lanes(p["w_rt_group"][l], RT_E), _pad_lanes(p["w_rt_expert"][l], LANES - RT_E)],
                           axis=1)
    b_rt = jnp.concatenate([_pad_lanes(p["b_rt_group"][l][None], RT_E),
                            _pad_lanes(p["b_rt_expert"][l][None], LANES - RT_E)], axis=1)
    stack_cols = lambda w: jnp.transpose(w, (1, 0, 2)).reshape(D_MODEL, N_EXPERTS * D_FF_E)
    row = lambda v: v[None]
    return dict(
        w_cat=jnp.concatenate([wz, wxbc, w_small, wq, wk, wv, wo, wup], axis=1).astype(BF16),
        w_gl=wgl.astype(BF16),
        conv_w=p["conv_w"][l], conv_b=row(p["conv_b"][l]), head_params=head_params,
        ssd_norm_w=row(p["ssd_norm_w"][l]), mlstm_norm_w=row(p["mlstm_norm_w"][l]),
        pool_w=p["pool_w"][l].astype(BF16), pool_scale=row(p["pool_scale"][l]),
        gate_b=row(p["gate_b"][l]),
        w_br_ssd=p["w_br_ssd"][l].astype(BF16), w_br_mlstm=p["w_br_mlstm"][l].astype(BF16),
        w_br_pool=p["w_br_pool"][l].astype(BF16), w_out=p["w_out"][l].astype(BF16),
        ln1_g=row(p["ln1_g"][l]), ln1_b=row(p["ln1_b"][l]),
        w_rt=w_rt, b_rt=b_rt,
        w_e_gate=stack_cols(p["w_e_gate"][l]).astype(BF16), w_e_up=stack_cols(p["w_e_up"][l]).astype(BF16),
        w_e_down=p["w_e_down"][l].reshape(N_EXPERTS * D_FF_E, D_MODEL).astype(BF16),
        ln2_g=row(p["ln2_g"][l]), ln2_b=row(p["ln2_b"][l]),
    )


def _flat_states(ssd, conv, mc, mn, mm, pool):
    return tuple(a.reshape(a.shape[:2] + shp) for a, shp in zip((ssd, conv, mc, mn, mm, pool), STATE_SHAPES))


def _unflat_states(states):
    ssd, conv, mc, mn, mm, pool = states
    d, n = ssd.shape[:2]
    return (ssd.reshape(d, n, SSD_HEADS, SSD_HEADDIM, SSD_STATE), conv,
            mc.reshape(d, n, M_HEADS, M_HEADDIM, M_HEADDIM), mn, mm.reshape(d, n, M_HEADS), pool)


def kernel(x_prompt, x_sample, state_ssd, state_conv, state_mlstm_C, state_mlstm_n, state_mlstm_m, state_pool, c_prompt, c_sample, w_ada_mix, b_ada_mix, w_in, conv_w, conv_b, ssd_A_log, ssd_dt_bias, ssd_D, ssd_norm_w, mlstm_gate_b, mlstm_norm_w, pool_w, pool_scale, gate_b, w_br_ssd, w_br_mlstm, w_br_pool, w_out, ln1_g, ln1_b, w_ada_ffn, b_ada_ffn, w_rt_group, b_rt_group, w_rt_expert, b_rt_expert, w_e_gate, w_e_up, w_e_down, ln2_g, ln2_b):
    params = dict(w_in=w_in, conv_w=conv_w, conv_b=conv_b, ssd_A_log=ssd_A_log, ssd_dt_bias=ssd_dt_bias,
                  ssd_D=ssd_D, ssd_norm_w=ssd_norm_w, mlstm_gate_b=mlstm_gate_b, mlstm_norm_w=mlstm_norm_w,
                  pool_w=pool_w, pool_scale=pool_scale, gate_b=gate_b, w_br_ssd=w_br_ssd,
                  w_br_mlstm=w_br_mlstm, w_br_pool=w_br_pool, w_out=w_out, ln1_g=ln1_g, ln1_b=ln1_b,
                  w_rt_group=w_rt_group, b_rt_group=b_rt_group, w_rt_expert=w_rt_expert,
                  b_rt_expert=b_rt_expert, w_e_gate=w_e_gate, w_e_up=w_e_up, w_e_down=w_e_down,
                  ln2_g=ln2_g, ln2_b=ln2_b)
    bp, seq, _ = x_prompt.shape
    bs, dec_seq, _ = x_sample.shape
    assert seq % CHUNK == 0 and 1 <= dec_seq <= SAMPLE_ROWS and bs % SAMPLE_SEQ_BLOCK == 0

    c_all = jnp.concatenate([c_prompt, c_sample], axis=0)
    mods_mix = _ada_call(c_all, w_ada_mix, b_ada_mix[:, None, :])
    mods_ffn = _ada_call(c_all, w_ada_ffn, b_ada_ffn[:, None, :])

    sample_in = _flat_states(state_ssd, state_conv, state_mlstm_C, state_mlstm_n, state_mlstm_m, state_pool)
    xp = x_prompt
    xs = jnp.pad(x_sample, ((0, 0), (0, SAMPLE_ROWS - dec_seq), (0, 0)))
    p_states, s_states = None, None
    for l in range(DEPTH):
        lw = _layer_weights(l, params)
        xp, p_states = _prompt_mixer_call(l, xp, mods_mix[l, :bp, None, :], p_states, lw)
        xp = _moe_call(xp, mods_ffn[l, :bp, None, :], lw)

        mods_s = mods_mix[l, bp:, None, :]
        proj = _proj_call(xs, mods_s, lw["w_cat"])
        ys, s_states = _sample_mixer_call(l, proj, sample_in, s_states, lw, tv=dec_seq)
        xs = _merge_call(xs, mods_s, ys, lw)
        xs = _moe_call(xs, mods_ffn[l, bp:, None, :], lw)
    return (xp, xs[:, :dec_seq]) + _unflat_states(p_states) + _unflat_states(s_states)
```

```python
import functools

import jax
import jax.numpy as jnp
from jax import lax
from jax.experimental import pallas as pl
from jax.experimental.pallas import tpu as pltpu

F32 = jnp.float32
BF16 = jnp.bfloat16

D_MODEL = 1024
DEPTH = 4
SSD_HEADS = 16
SSD_HEADDIM = 64
SSD_GROUPS = 2
SSD_REP = SSD_HEADS // SSD_GROUPS
SSD_STATE = 128
SSD_CONV = 4
SSD_CONV_DIM = D_MODEL + 2 * SSD_GROUPS * SSD_STATE
CHUNK = 128
M_HEADS = 4
M_HEADDIM = D_MODEL // M_HEADS
POOL_WINDOWS = (2, 4, 8, 16)
POOL_GW = D_MODEL // len(POOL_WINDOWS)
POOL_BUF = max(POOL_WINDOWS) - 1
N_EGROUPS = 4
EXP_PER_GROUP = 4
N_EXPERTS = N_EGROUPS * EXP_PER_GROUP
D_FF_E = D_MODEL // 4
ALPHA = (2 * DEPTH) ** 0.25
LN_EPS = 1e-5
RMS_EPS = 1e-6

SUBLANES = 8
LANES = 128
VMEM_LIMIT_BYTES = 56 * 1024 * 1024

OFF_Z = 0
OFF_XBC = OFF_Z + D_MODEL
OFF_SMALL = OFF_XBC + SSD_CONV_DIM
OFF_Q = OFF_SMALL + LANES
OFF_K = OFF_Q + D_MODEL
OFF_V = OFF_K + D_MODEL
OFF_O = OFF_V + D_MODEL
OFF_UP = OFF_O + D_MODEL
N_PROJ = OFF_UP + D_MODEL
SM_A, SM_DT, SM_F, SM_I, SM_END = 0, SSD_HEADS, 2 * SSD_HEADS, 2 * SSD_HEADS + M_HEADS, 2 * SSD_HEADS + 2 * M_HEADS
RT_G, RT_E = 0, 16
NEG_BIG = -1e30

ROW_TILE = 256
SAMPLE_ROWS = 8
SAMPLE_SEQ_BLOCK = 4

STATE_SHAPES = ((SSD_HEADS * SSD_HEADDIM, SSD_STATE), (SSD_CONV - 1, SSD_CONV_DIM),
                (M_HEADS * M_HEADDIM, M_HEADDIM), (M_HEADS, M_HEADDIM), (1, M_HEADS), (POOL_BUF, D_MODEL))
N_STATES = len(STATE_SHAPES)


def _dot(a, b):
    return jnp.dot(a.astype(BF16), b.astype(BF16), preferred_element_type=F32)


def _dot_nt(a, b):
    return lax.dot_general(a.astype(BF16), b.astype(BF16), (((1,), (1,)), ((), ())),
                           preferred_element_type=F32)


def _dot_tn(a, b):
    return lax.dot_general(a.astype(BF16), b.astype(BF16), (((0,), (0,)), ((), ())),
                           preferred_element_type=F32)


def _dot_exact(a, b):
    return jnp.dot(a, b, precision=lax.Precision.HIGHEST, preferred_element_type=F32)


def _dot_nt_exact(a, b):
    return lax.dot_general(a, b, (((1,), (1,)), ((), ())), precision=lax.Precision.HIGHEST,
                           preferred_element_type=F32)


def _sigmoid(x):
    return 1.0 / (1.0 + jnp.exp(-x))


def _silu(x):
    return x * _sigmoid(x)


def _softplus(x):
    return jnp.maximum(x, 0.0) + jnp.log1p(jnp.exp(-jnp.abs(x)))


def _layer_norm(x, g, b):
    mu = jnp.mean(x, axis=-1, keepdims=True)
    xc = x - mu
    var = jnp.mean(xc * xc, axis=-1, keepdims=True)
    return xc * lax.rsqrt(var + LN_EPS) * g + b


def _compiler_params(n_grid):
    return pltpu.CompilerParams(dimension_semantics=("arbitrary",) * n_grid,
                                vmem_limit_bytes=VMEM_LIMIT_BYTES)


def _vmem_full():
    return pl.BlockSpec(memory_space=pltpu.VMEM)


def _const_spec(shape):
    return pl.BlockSpec(shape, lambda *_: (0,) * len(shape))


def _ada_kernel(c_ref, w_ref, b_ref, o_ref):
    o_ref[...] = _dot(c_ref[...], w_ref[...]) + b_ref[...]


def _ada_call(c_all, w, b):
    n = c_all.shape[0]
    return pl.pallas_call(
        _ada_kernel,
        grid=(DEPTH, 3),
        in_specs=[pl.BlockSpec((n, D_MODEL), lambda l, j: (0, 0)),
                  pl.BlockSpec((None, D_MODEL, D_MODEL), lambda l, j: (l, 0, j)),
                  pl.BlockSpec((None, 1, D_MODEL), lambda l, j: (l, 0, j))],
        out_specs=pl.BlockSpec((None, n, D_MODEL), lambda l, j: (l, 0, j)),
        out_shape=jax.ShapeDtypeStruct((DEPTH, n, 3 * D_MODEL), F32),
        compiler_params=_compiler_params(2),
        name="ada_mod",
    )(c_all, w, b)


def _modulate(x_ref, mod_ref):
    x = x_ref[...]
    shift = mod_ref[:, :, 0:D_MODEL]
    scale = mod_ref[:, :, D_MODEL:2 * D_MODEL]
    u = x * (1.0 + scale) + shift
    return u.reshape(x.shape[0] * x.shape[1], D_MODEL)


def _proj_kernel(x_ref, mod_ref, w_ref, o_ref):
    sb, rb, _ = x_ref.shape
    u = _modulate(x_ref, mod_ref).astype(BF16)
    col = 0
    while col < N_PROJ:
        width = min(D_MODEL, N_PROJ - col)
        o_ref[:, :, col:col + width] = _dot(u, w_ref[:, col:col + width]).reshape(sb, rb, width)
        col += width


def _row_blocks(n_seq, rows):
    if rows >= ROW_TILE:
        return 1, ROW_TILE
    return ROW_TILE // rows, rows


def _proj_call(x3, mods, wcat):
    n_seq, rows, _ = x3.shape
    sb, rb = _row_blocks(n_seq, rows)
    return pl.pallas_call(
        _proj_kernel,
        grid=(n_seq // sb, rows // rb),
        in_specs=[pl.BlockSpec((sb, rb, D_MODEL), lambda i, j: (i, j, 0)),
                  pl.BlockSpec((sb, 1, 3 * D_MODEL), lambda i, j: (i, 0, 0)),
                  _vmem_full()],
        out_specs=pl.BlockSpec((sb, rb, N_PROJ), lambda i, j: (i, j, 0)),
        out_shape=jax.ShapeDtypeStruct((n_seq, rows, N_PROJ), F32),
        compiler_params=_compiler_params(2),
        name="in_proj",
    )(x3, mods, wcat)


def _mixer_chunk(proj, st_in, st_out, par, scr, *, n_seq, rows, tv, pos0):
    L = n_seq * rows
    rows_log2 = rows.bit_length() - 1
    assert rows == 1 << rows_log2
    ssd_i, conv_i, mc_i, mn_i, mm_i, pool_i = st_in
    ssd_o, conv_o, mc_o, mn_o, mm_o, pool_o = st_out
    convw_ref, convb_ref, hp_ref, snw_ref, mnw_ref, poolw_ref, pscale_ref = par
    xext_ref, pext_ref, yacc_ref = scr
    seq_rows = [slice(s * rows, (s + 1) * rows) for s in range(n_seq)]

    def per_seq(fn):
        parts = [fn(s, seq_rows[s]) for s in range(n_seq)]
        return parts[0] if n_seq == 1 else jnp.concatenate(parts, axis=0)

    row_l = lax.broadcasted_iota(jnp.int32, (L, L), 0)
    col_l = lax.broadcasted_iota(jnp.int32, (L, L), 1)
    causal = row_l >= col_l
    if n_seq > 1:
        same_seq = (row_l >> rows_log2) == (col_l >> rows_log2)
        causal = causal & same_seq

    lane = lax.broadcasted_iota(jnp.int32, (L, LANES), 1)
    pre = proj(OFF_SMALL, LANES) + hp_ref[0:1, :]
    sp = _softplus(pre)
    a_row = -jnp.exp(hp_ref[1:2, :])
    pmat = jnp.where(lane < SM_DT, sp * a_row,
                     jnp.where(lane < SM_F, sp,
                               jnp.where(lane < SM_I, -_softplus(-pre),
                                         jnp.where(lane < SM_END, pre, 0.0))))
    if tv < rows:
        row = lax.broadcasted_iota(jnp.int32, (L, LANES), 0)
        pad = jnp.where(lane < SM_I, 0.0, jnp.where(lane < SM_END, NEG_BIG, 0.0))
        pmat = jnp.where((row & (rows - 1)) < tv, pmat, pad)
    eye = (lax.broadcasted_iota(jnp.int32, (LANES, LANES), 0)
           == lax.broadcasted_iota(jnp.int32, (LANES, LANES), 1)).astype(F32)
    cum = _dot_exact(causal.astype(F32), pmat)
    pmat_t = _dot_nt_exact(eye, pmat)
    cum_t = _dot_nt_exact(eye, cum)
    if n_seq > 1:
        tot = _dot_exact(same_seq.astype(F32), pmat)
    else:
        tot = cum[L - 1:L, :]

    xext_ref[:, SUBLANES - (SSD_CONV - 1):SUBLANES, :] = conv_i[...]
    xext_ref[:, SUBLANES:SUBLANES + rows, :] = proj(OFF_XBC, SSD_CONV_DIM).reshape(n_seq, rows, SSD_CONV_DIM)
    acc = convb_ref[...]
    for k in range(SSD_CONV):
        start = SUBLANES - (SSD_CONV - 1) + k
        acc = acc + xext_ref[:, start:start + rows, :].reshape(L, SSD_CONV_DIM) * convw_ref[k:k + 1, :]
    conv_o[...] = xext_ref[:, SUBLANES + tv - (SSD_CONV - 1):SUBLANES + tv, :]
    xbc = _silu(acc)
    xs = xbc[:, 0:D_MODEL]
    d_row = hp_ref[2:3, :]
    gw = SSD_REP * SSD_HEADDIM
    for g in range(SSD_GROUPS):
        grp = slice(g * gw, (g + 1) * gw)
        bm = xbc[:, D_MODEL + g * SSD_STATE:D_MODEL + (g + 1) * SSD_STATE]
        cm = xbc[:, D_MODEL + (SSD_GROUPS + g) * SSD_STATE:D_MODEL + (SSD_GROUPS + g + 1) * SSD_STATE]
        cb = _dot_nt(cm, bm)
        y_state = per_seq(lambda s, rs: _dot_nt(cm[rs], ssd_i[s, grp, :]))
        xw_parts = []
        for r in range(SSD_REP):
            h = g * SSD_REP + r
            hs = slice(h * SSD_HEADDIM, (h + 1) * SSD_HEADDIM)
            cum_c = cum[:, SM_A + h:SM_A + h + 1]
            cum_r = cum_t[SM_A + h:SM_A + h + 1, :]
            dt_c = pmat[:, SM_DT + h:SM_DT + h + 1]
            dt_r = pmat_t[SM_DT + h:SM_DT + h + 1, :]
            seg = jnp.where(causal, cum_c - cum_r, -jnp.inf)
            wmat = cb * jnp.exp(seg) * dt_r
            x_h = xs[:, hs]
            y = _dot(wmat, x_h) + y_state[:, r * SSD_HEADDIM:(r + 1) * SSD_HEADDIM] * jnp.exp(cum_c)
            yacc_ref[:, hs] = y + d_row[:, h:h + 1] * x_h
            xw_parts.append(x_h * (jnp.exp(tot[:, SM_A + h:SM_A + h + 1] - cum_c) * dt_c))
        xw = jnp.concatenate(xw_parts, axis=-1)
        for s in range(n_seq):
            upd = _dot_tn(xw[seq_rows[s]], bm[seq_rows[s]])
            t0 = s * rows if n_seq > 1 else 0
            for r in range(SSD_REP):
                h = g * SSD_REP + r
                hs = slice(h * SSD_HEADDIM, (h + 1) * SSD_HEADDIM)
                decay = jnp.exp(tot[t0:t0 + 1, SM_A + h:SM_A + h + 1])
                ssd_o[s, hs, :] = decay * ssd_i[s, hs, :] + upd[r * SSD_HEADDIM:(r + 1) * SSD_HEADDIM, :]
    yz = yacc_ref[...] * _silu(proj(OFF_Z, D_MODEL))
    y_ssd = yz * lax.rsqrt(jnp.mean(yz * yz, axis=-1, keepdims=True) + RMS_EPS) * snw_ref[...]

    ym_parts = []
    for h in range(M_HEADS):
        hs = slice(h * M_HEADDIM, (h + 1) * M_HEADDIM)
        q_h = proj(OFF_Q + h * M_HEADDIM, M_HEADDIM)
        k_h = proj(OFF_K + h * M_HEADDIM, M_HEADDIM) * (M_HEADDIM ** -0.5)
        v_h = proj(OFF_V + h * M_HEADDIM, M_HEADDIM)
        o_h = proj(OFF_O + h * M_HEADDIM, M_HEADDIM)
        b_c = cum[:, SM_F + h:SM_F + h + 1]
        b_r = cum_t[SM_F + h:SM_F + h + 1, :]
        i_c = pmat[:, SM_I + h:SM_I + h + 1]
        i_r = pmat_t[SM_I + h:SM_I + h + 1, :]
        last_b = tot[:, SM_F + h:SM_F + h + 1]
        m_prev = per_seq(lambda s, rs: jnp.broadcast_to(mm_i[s, :, h:h + 1], (rows, 1)))
        dmat = jnp.where(causal, b_c - b_r + i_r, -jnp.inf)
        m_st = b_c + m_prev
        m = jnp.maximum(m_st, jnp.max(dmat, axis=-1, keepdims=True))
        wts = jnp.exp(dmat - m) * _dot_nt(q_h, k_h)
        ws = jnp.exp(m_st - m)
        cq = per_seq(lambda s, rs: _dot_nt(q_h[rs], mc_i[s, hs, :]))
        nq = per_seq(lambda s, rs: jnp.sum(q_h[rs] * mn_i[s, h:h + 1, :], axis=-1, keepdims=True))
        num = _dot(wts, v_h) + ws * cq
        den = jnp.sum(wts, axis=-1, keepdims=True) + ws * nq
        hc = num / jnp.maximum(jnp.abs(den), jnp.exp(-m))
        m_new = per_seq(lambda s, rs: jnp.broadcast_to(m[rs.stop - 1:rs.stop, :], (rows, 1)))
        wsrc = jnp.exp(last_b - b_c + i_c - m_new)
        wprev = jnp.exp(last_b + m_prev - m_new)
        vw = v_h * wsrc
        kw = k_h * wsrc
        for s in range(n_seq):
            rs = seq_rows[s]
            wp = wprev[rs.start:rs.start + 1, :]
            mc_o[s, hs, :] = wp * mc_i[s, hs, :] + _dot_tn(vw[rs], k_h[rs])
            mn_o[s, h:h + 1, :] = wp * mn_i[s, h:h + 1, :] + jnp.sum(kw[rs], axis=0, keepdims=True)
            mm_o[s, :, h:h + 1] = m_new[rs.start:rs.start + 1, :]
        mu = jnp.mean(hc, axis=-1, keepdims=True)
        hd = hc - mu
        var = jnp.mean(hd * hd, axis=-1, keepdims=True)
        ym_parts.append(hd * lax.rsqrt(var + LN_EPS) * mnw_ref[:, hs] * _sigmoid(o_h))
    y_m = jnp.concatenate(ym_parts, axis=-1)

    pext_ref[:, 1:POOL_BUF + 1, :] = pool_i[...]
    up = proj(OFF_UP, D_MODEL)
    pext_ref[:, POOL_BUF + 1:POOL_BUF + 1 + rows, :] = up.reshape(n_seq, rows, D_MODEL)
    pos = (lax.broadcasted_iota(jnp.int32, (L, 1), 0) & (rows - 1)) + pos0
    yp_parts = []
    for g, w in enumerate(POOL_WINDOWS):
        gs = slice(g * POOL_GW, (g + 1) * POOL_GW)
        wsum = up[:, gs]
        for j in range(1, w):
            wsum = wsum + pext_ref[:, POOL_BUF + 1 - j:POOL_BUF + 1 - j + rows, gs].reshape(L, POOL_GW)
        cnt = jnp.minimum(pos, w).astype(F32)
        dlt = wsum / cnt - up[:, gs]
        yp_parts.append(_dot(dlt, poolw_ref[g]) * pscale_ref[:, gs])
    pool_o[...] = pext_ref[:, 1 + tv:1 + tv + POOL_BUF, :]
    return y_ssd, y_m, jnp.concatenate(yp_parts, axis=-1)


def _merge_rows(ub, ys, wgl_ref, gb_ref, wb_refs, wout_ref):
    merged = None
    for i, (y, wb_ref) in enumerate(zip(ys, wb_refs)):
        cs = slice(i * D_MODEL, (i + 1) * D_MODEL)
        gate = _sigmoid(_dot(ub, wgl_ref[:, cs]) + gb_ref[:, cs])
        term = gate * _dot(y, wb_ref[...])
        merged = term if merged is None else merged + term
    return _dot(merged, wout_ref[...])


N_MIXER_PARAMS = 7


def _mixer_param_specs():
    return [_const_spec((SSD_CONV, SSD_CONV_DIM)), _const_spec((1, SSD_CONV_DIM)),
            _const_spec((SUBLANES, LANES)), _const_spec((1, D_MODEL)), _const_spec((1, D_MODEL)),
            _const_spec((len(POOL_WINDOWS), POOL_GW, POOL_GW)), _const_spec((1, D_MODEL))]


def _mixer_param_args(lw):
    return (lw["conv_w"], lw["conv_b"], lw["head_params"], lw["ssd_norm_w"], lw["mlstm_norm_w"],
            lw["pool_w"], lw["pool_scale"])


def _mixer_scratch(n_seq, rows):
    return [pltpu.VMEM((n_seq, SUBLANES + rows, SSD_CONV_DIM), F32),
            pltpu.VMEM((n_seq, POOL_BUF + 1 + rows, D_MODEL), F32),
            pltpu.VMEM((n_seq * rows, D_MODEL), F32)]


def _stacked_state_shapes(n_seq):
    return [jax.ShapeDtypeStruct((DEPTH, n_seq) + shp, F32) for shp in STATE_SHAPES]


def _alias_args(prev_states, n_inputs_before, n_outputs_before):
    if prev_states is None:
        return [], [], {}
    specs = [pl.BlockSpec(memory_space=pl.ANY)] * N_STATES
    aliases = {n_inputs_before + k: n_outputs_before + k for k in range(N_STATES)}
    return list(prev_states), specs, aliases


def _prompt_mixer_kernel(x_ref, mod_ref, wcat_ref, wgl_ref, gb_ref, wbs_ref, wbm_ref, wbp_ref, wout_ref,
                         lng_ref, lnb_ref, *rest, n_alias):
    par = rest[:N_MIXER_PARAMS]
    rest = rest[N_MIXER_PARAMS + n_alias:]
    o_ref = rest[0]
    states = rest[1:1 + N_STATES]
    scr = rest[1 + N_STATES:]
    L = x_ref.shape[0]
    c = pl.program_id(1)

    @pl.when(c == 0)
    def _fresh_prompt_states():
        for ref in states:
            ref[...] = jnp.zeros(ref.shape, ref.dtype)

    x = x_ref[...]
    u = x * (1.0 + mod_ref[:, D_MODEL:2 * D_MODEL]) + mod_ref[:, 0:D_MODEL]
    ub = u.astype(BF16)
    proj = lambda off, width: _dot(ub, wcat_ref[:, off:off + width])
    ys = _mixer_chunk(proj, states, states, par, scr, n_seq=1, rows=L, tv=L, pos0=c * L + 1)
    mix = _merge_rows(ub, ys, wgl_ref, gb_ref, (wbs_ref, wbm_ref, wbp_ref), wout_ref)
    o_ref[...] = _layer_norm(ALPHA * x + (1.0 + mod_ref[:, 2 * D_MODEL:3 * D_MODEL]) * mix,
                             lng_ref[...], lnb_ref[...])


def _prompt_mixer_call(l, x3, mods, prev_states, lw):
    n_seq, rows, _ = x3.shape
    L = CHUNK
    row_spec = pl.BlockSpec((None, L, D_MODEL), lambda b, c: (b, c, 0))
    vec_spec = _const_spec((1, D_MODEL))
    state_specs = [pl.BlockSpec((None, 1) + shp, lambda b, c: (l, b, 0, 0)) for shp in STATE_SHAPES]
    in_specs = [row_spec, pl.BlockSpec((None, 1, 3 * D_MODEL), lambda b, c: (b, 0, 0)),
                _vmem_full(), _vmem_full(), _const_spec((1, 3 * D_MODEL)),
                _vmem_full(), _vmem_full(), _vmem_full(), _vmem_full(), vec_spec, vec_spec]
    in_specs += _mixer_param_specs()
    alias_in, alias_specs, aliases = _alias_args(prev_states, len(in_specs), 1)
    outs = pl.pallas_call(
        functools.partial(_prompt_mixer_kernel, n_alias=len(alias_in)),
        grid=(n_seq, rows // L),
        in_specs=in_specs + alias_specs,
        out_specs=[row_spec] + state_specs,
        out_shape=[jax.ShapeDtypeStruct(x3.shape, F32)] + _stacked_state_shapes(n_seq),
        scratch_shapes=_mixer_scratch(1, L),
        input_output_aliases=aliases,
        compiler_params=_compiler_params(2),
        name="prompt_mixers",
    )(x3, mods, lw["w_cat"], lw["w_gl"], lw["gate_b"], lw["w_br_ssd"], lw["w_br_mlstm"], lw["w_br_pool"],
      lw["w_out"], lw["ln1_g"], lw["ln1_b"], *_mixer_param_args(lw), *alias_in)
    return outs[0], outs[1:]


def _sample_mixer_kernel(proj_ref, *rest, tv, n_alias):
    st_in = rest[:N_STATES]
    par = rest[N_STATES:N_STATES + N_MIXER_PARAMS]
    rest = rest[N_STATES + N_MIXER_PARAMS + n_alias:]
    y_refs = rest[:3]
    st_out = rest[3:3 + N_STATES]
    scr = rest[3 + N_STATES:]
    n_blk, rows, _ = proj_ref.shape
    proj = lambda off, width: proj_ref[:, :, off:off + width].reshape(n_blk * rows, width)
    ys = _mixer_chunk(proj, st_in, st_out, par, scr, n_seq=n_blk, rows=rows, tv=tv, pos0=1 + POOL_BUF)
    for y_ref, y in zip(y_refs, ys):
        y_ref[...] = y.reshape(n_blk, rows, D_MODEL)


def _sample_mixer_call(l, proj, states_in, prev_states, lw, *, tv):
    n_seq, rows, _ = proj.shape
    nb = SAMPLE_SEQ_BLOCK
    state_specs = [pl.BlockSpec((None, nb) + shp, lambda i: (l, i, 0, 0)) for shp in STATE_SHAPES]
    y_spec = pl.BlockSpec((nb, rows, D_MODEL), lambda i: (i, 0, 0))
    y_shape = jax.ShapeDtypeStruct((n_seq, rows, D_MODEL), F32)
    in_specs = [pl.BlockSpec((nb, rows, N_PROJ), lambda i: (i, 0, 0))] + state_specs + _mixer_param_specs()
    alias_in, alias_specs, aliases = _alias_args(prev_states, len(in_specs), 3)
    outs = pl.pallas_call(
        functools.partial(_sample_mixer_kernel, tv=tv, n_alias=len(alias_in)),
        grid=(n_seq // nb,),
        in_specs=in_specs + alias_specs,
        out_specs=[y_spec, y_spec, y_spec] + state_specs,
        out_shape=[y_shape, y_shape, y_shape] + _stacked_state_shapes(n_seq),
        scratch_shapes=_mixer_scratch(nb, rows),
        input_output_aliases=aliases,
        compiler_params=_compiler_params(1),
        name="sample_mixers",
    )(proj, *states_in, *_mixer_param_args(lw), *alias_in)
    return outs[:3], outs[3:]


def _merge_kernel(x_ref, mod_ref, yssd_ref, ym_ref, ypool_ref, wgl_ref, gb_ref,
                  wbs_ref, wbm_ref, wbp_ref, wout_ref, lng_ref, lnb_ref, o_ref):
    sb, rb, _ = x_ref.shape
    n = sb * rb
    ub = _modulate(x_ref, mod_ref).astype(BF16)
    ys = [r[...].reshape(n, D_MODEL) for r in (yssd_ref, ym_ref, ypool_ref)]
    mix = _merge_rows(ub, ys, wgl_ref, gb_ref, (wbs_ref, wbm_ref, wbp_ref), wout_ref).reshape(sb, rb, D_MODEL)
    gate_a = mod_ref[:, :, 2 * D_MODEL:3 * D_MODEL]
    o_ref[...] = _layer_norm(ALPHA * x_ref[...] + (1.0 + gate_a) * mix, lng_ref[...], lnb_ref[...])


def _merge_call(x3, mods, ys, lw):
    n_seq, rows, _ = x3.shape
    sb, rb = _row_blocks(n_seq, rows)
    row_spec = pl.BlockSpec((sb, rb, D_MODEL), lambda i, j: (i, j, 0))
    vec_spec = _const_spec((1, D_MODEL))
    return pl.pallas_call(
        _merge_kernel,
        grid=(n_seq // sb, rows // rb),
        in_specs=[row_spec, pl.BlockSpec((sb, 1, 3 * D_MODEL), lambda i, j: (i, 0, 0)),
                  row_spec, row_spec, row_spec,
                  _vmem_full(), _const_spec((1, 3 * D_MODEL)),
                  _vmem_full(), _vmem_full(), _vmem_full(), _vmem_full(), vec_spec, vec_spec],
        out_specs=row_spec,
        out_shape=jax.ShapeDtypeStruct(x3.shape, F32),
        compiler_params=_compiler_params(2),
        name="merge_norm",
    )(x3, mods, ys[0], ys[1], ys[2], lw["w_gl"], lw["gate_b"], lw["w_br_ssd"], lw["w_br_mlstm"],
      lw["w_br_pool"], lw["w_out"], lw["ln1_g"], lw["ln1_b"])


def _route(logits):
    n = logits.shape[0]
    lane = lax.broadcasted_iota(jnp.int32, (n, LANES), 1)
    is_g = (lane >= RT_G) & (lane < RT_G + N_EGROUPS)
    lg = jnp.where(is_g, logits, -jnp.inf)
    g_max = jnp.max(lg, axis=-1, keepdims=True)
    g_idx = jnp.min(jnp.where(lg == g_max, lane - RT_G, LANES), axis=-1, keepdims=True)
    g_prob = 1.0 / jnp.sum(jnp.exp(lg - g_max), axis=-1, keepdims=True)
    lo = RT_E + g_idx * EXP_PER_GROUP
    le = jnp.where((lane >= lo) & (lane < lo + EXP_PER_GROUP), logits, -jnp.inf)
    v1 = jnp.max(le, axis=-1, keepdims=True)
    i1 = jnp.min(jnp.where(le == v1, lane, LANES), axis=-1, keepdims=True)
    le2 = jnp.where(lane == i1, -jnp.inf, le)
    v2 = jnp.max(le2, axis=-1, keepdims=True)
    i2 = jnp.min(jnp.where(le2 == v2, lane, LANES), axis=-1, keepdims=True)
    e2 = jnp.exp(v2 - v1)
    p1 = g_prob / (1.0 + e2)
    p2 = g_prob * e2 / (1.0 + e2)
    return jnp.where(lane == i1, p1, 0.0) + jnp.where(lane == i2, p2, 0.0)


def _moe_kernel(x_ref, mod_ref, wrt_ref, brt_ref, wg_ref, wu_ref, wd_ref, lng_ref, lnb_ref, o_ref):
    sb, rb, _ = x_ref.shape
    u = _modulate(x_ref, mod_ref)
    wts = _route(_dot_exact(u, wrt_ref[...]) + brt_ref[...])
    ub = u.astype(BF16)
    ffn = None
    for g in range(N_EGROUPS):
        parts = []
        for r in range(EXP_PER_GROUP):
            e = g * EXP_PER_GROUP + r
            es = slice(e * D_FF_E, (e + 1) * D_FF_E)
            hid = _silu(_dot(ub, wg_ref[:, es])) * _dot(ub, wu_ref[:, es])
            parts.append(hid * wts[:, RT_E + e:RT_E + e + 1])
        hid_g = jnp.concatenate(parts, axis=-1)
        gs = slice(g * EXP_PER_GROUP * D_FF_E, (g + 1) * EXP_PER_GROUP * D_FF_E)
        term = _dot(hid_g, wd_ref[gs, :])
        ffn = term if ffn is None else ffn + term
    gate_f = mod_ref[:, :, 2 * D_MODEL:3 * D_MODEL]
    o_ref[...] = _layer_norm(ALPHA * x_ref[...] + (1.0 + gate_f) * ffn.reshape(sb, rb, D_MODEL),
                             lng_ref[...], lnb_ref[...])


def _moe_call(x3, mods, lw):
    n_seq, rows, _ = x3.shape
    sb, rb = _row_blocks(n_seq, rows)
    row_spec = pl.BlockSpec((sb, rb, D_MODEL), lambda i, j: (i, j, 0))
    vec_spec = _const_spec((1, D_MODEL))
    return pl.pallas_call(
        _moe_kernel,
        grid=(n_seq // sb, rows // rb),
        in_specs=[row_spec, pl.BlockSpec((sb, 1, 3 * D_MODEL), lambda i, j: (i, 0, 0)),
                  _vmem_full(), _const_spec((1, LANES)),
                  _vmem_full(), _vmem_full(), _vmem_full(), vec_spec, vec_spec],
        out_specs=row_spec,
        out_shape=jax.ShapeDtypeStruct(x3.shape, F32),
        compiler_params=_compiler_params(2),
        name="moe_norm",
    )(x3, mods, lw["w_rt"], lw["b_rt"], lw["w_e_gate"], lw["w_e_up"], lw["w_e_down"],
      lw["ln2_g"], lw["ln2_b"])


def _split_w_in(w):
    sizes = (D_MODEL, SSD_CONV_DIM, SSD_HEADS, D_MODEL, D_MODEL, D_MODEL, M_HEADS, M_HEADS, D_MODEL,
             D_MODEL, 3 * D_MODEL)
    out, off = [], 0
    for s in sizes:
        out.append(w[:, off:off + s])
        off += s
    return out


def _pad_lanes(v, width=LANES):
    return jnp.pad(v, ((0, 0), (0, width - v.shape[-1])))


def _layer_weights(l, p):
    wz, wxbc, wdt, wq, wk, wv, wi, wf, wo, wup, wgl = _split_w_in(p["w_in"][l])
    w_small = _pad_lanes(jnp.concatenate([wdt, wdt, wf, wi], axis=1))
    gate_b = p["mlstm_gate_b"][l]
    bias_row = jnp.concatenate([p["ssd_dt_bias"][l], p["ssd_dt_bias"][l], gate_b[M_HEADS:], gate_b[:M_HEADS]])
    head_params = jnp.concatenate([
        _pad_lanes(bias_row[None]), _pad_lanes(p["ssd_A_log"][l][None]), _pad_lanes(p["ssd_D"][l][None]),
        jnp.zeros((SUBLANES - 3, LANES), F32)], axis=0)
    w_rt = jnp.concatenate([_pad_lanes(p["w_rt_group"][l], RT_E), _pad_lanes(p["w_rt_expert"][l], LANES - RT_E)],
                           axis=1)
    b_rt = jnp.concatenate([_pad_lanes(p["b_rt_group"][l][None], RT_E),
                            _pad_lanes(p["b_rt_expert"][l][None], LANES - RT_E)], axis=1)
    stack_cols = lambda w: jnp.transpose(w, (1, 0, 2)).reshape(D_MODEL, N_EXPERTS * D_FF_E)
    row = lambda v: v[None]
    return dict(
        w_cat=jnp.concatenate([wz, wxbc, w_small, wq, wk, wv, wo, wup], axis=1).astype(BF16),
        w_gl=wgl.astype(BF16),
        conv_w=p["conv_w"][l], conv_b=row(p["conv_b"][l]), head_params=head_params,
        ssd_norm_w=row(p["ssd_norm_w"][l]), mlstm_norm_w=row(p["mlstm_norm_w"][l]),
        pool_w=p["pool_w"][l].astype(BF16), pool_scale=row(p["pool_scale"][l]),
        gate_b=row(p["gate_b"][l]),
        w_br_ssd=p["w_br_ssd"][l].astype(BF16), w_br_mlstm=p["w_br_mlstm"][l].astype(BF16),
        w_br_pool=p["w_br_pool"][l].astype(BF16), w_out=p["w_out"][l].astype(BF16),
        ln1_g=row(p["ln1_g"][l]), ln1_b=row(p["ln1_b"][l]),
        w_rt=w_rt, b_rt=b_rt,
        w_e_gate=stack_cols(p["w_e_gate"][l]).astype(BF16), w_e_up=stack_cols(p["w_e_up"][l]).astype(BF16),
        w_e_down=p["w_e_down"][l].reshape(N_EXPERTS * D_FF_E, D_MODEL).astype(BF16),
        ln2_g=row(p["ln2_g"][l]), ln2_b=row(p["ln2_b"][l]),
    )


def _flat_states(ssd, conv, mc, mn, mm, pool):
    return tuple(a.reshape(a.shape[:2] + shp) for a, shp in zip((ssd, conv, mc, mn, mm, pool), STATE_SHAPES))


def _unflat_states(states):
    ssd, conv, mc, mn, mm, pool = states
    d, n = ssd.shape[:2]
    return (ssd.reshape(d, n, SSD_HEADS, SSD_HEADDIM, SSD_STATE), conv,
            mc.reshape(d, n, M_HEADS, M_HEADDIM, M_HEADDIM), mn, mm.reshape(d, n, M_HEADS), pool)


def kernel(x_prompt, x_sample, state_ssd, state_conv, state_mlstm_C, state_mlstm_n, state_mlstm_m, state_pool, c_prompt, c_sample, w_ada_mix, b_ada_mix, w_in, conv_w, conv_b, ssd_A_log, ssd_dt_bias, ssd_D, ssd_norm_w, mlstm_gate_b, mlstm_norm_w, pool_w, pool_scale, gate_b, w_br_ssd, w_br_mlstm, w_br_pool, w_out, ln1_g, ln1_b, w_ada_ffn, b_ada_ffn, w_rt_group, b_rt_group, w_rt_expert, b_rt_expert, w_e_gate, w_e_up, w_e_down, ln2_g, ln2_b):
    params = dict(w_in=w_in, conv_w=conv_w, conv_b=conv_b, ssd_A_log=ssd_A_log, ssd_dt_bias=ssd_dt_bias,
                  ssd_D=ssd_D, ssd_norm_w=ssd_norm_w, mlstm_gate_b=mlstm_gate_b, mlstm_norm_w=mlstm_norm_w,
                  pool_w=pool_w, pool_scale=pool_scale, gate_b=gate_b, w_br_ssd=w_br_ssd,
                  w_br_mlstm=w_br_mlstm, w_br_pool=w_br_pool, w_out=w_out, ln1_g=ln1_g, ln1_b=ln1_b,
                  w_rt_group=w_rt_group, b_rt_group=b_rt_group, w_rt_expert=w_rt_expert,
                  b_rt_expert=b_rt_expert, w_e_gate=w_e_gate, w_e_up=w_e_up, w_e_down=w_e_down,
                  ln2_g=ln2_g, ln2_b=ln2_b)
    bp, seq, _ = x_prompt.shape
    bs, dec_seq, _ = x_sample.shape
    assert seq % CHUNK == 0 and 1 <= dec_seq <= SAMPLE_ROWS and bs % SAMPLE_SEQ_BLOCK == 0

    c_all = jnp.concatenate([c_prompt, c_sample], axis=0)
    mods_mix = _ada_call(c_all, w_ada_mix, b_ada_mix[:, None, :])
    mods_ffn = _ada_call(c_all, w_ada_ffn, b_ada_ffn[:, None, :])

    sample_in = _flat_states(state_ssd, state_conv, state_mlstm_C, state_mlstm_n, state_mlstm_m, state_pool)
    xp = x_prompt
    xs = jnp.pad(x_sample, ((0, 0), (0, SAMPLE_ROWS - dec_seq), (0, 0)))
    p_states, s_states = None, None
    for l in range(DEPTH):
        lw = _layer_weights(l, params)
        xp, p_states = _prompt_mixer_call(l, xp, mods_mix[l, :bp, None, :], p_states, lw)
        xp = _moe_call(xp, mods_ffn[l, :bp, None, :], lw)

        mods_s = mods_mix[l, bp:, None, :]
        proj = _proj_call(xs, mods_s, lw["w_cat"])
        ys, s_states = _sample_mixer_call(l, proj, sample_in, s_states, lw, tv=dec_seq)
        xs = _merge_call(xs, mods_s, ys, lw)
        xs = _moe_call(xs, mods_ffn[l, bp:, None, :], lw)
    return (xp, xs[:, :dec_seq]) + _unflat_states(p_states) + _unflat_states(s_states)
```

```python
import functools

import jax
import jax.numpy as jnp
from jax import lax
from jax.experimental import pallas as pl
from jax.experimental.pallas import tpu as pltpu

F32 = jnp.float32
BF16 = jnp.bfloat16

D_MODEL = 1024
DEPTH = 4
SSD_HEADS = 16
SSD_HEADDIM = 64
SSD_GROUPS = 2
SSD_REP = SSD_HEADS // SSD_GROUPS
SSD_STATE = 128
SSD_CONV = 4
SSD_CONV_DIM = D_MODEL + 2 * SSD_GROUPS * SSD_STATE
CHUNK = 128
M_HEADS = 4
M_HEADDIM = D_MODEL // M_HEADS
POOL_WINDOWS = (2, 4, 8, 16)
POOL_GW = D_MODEL // len(POOL_WINDOWS)
POOL_BUF = max(POOL_WINDOWS) - 1
N_EGROUPS = 4
EXP_PER_GROUP = 4
N_EXPERTS = N_EGROUPS * EXP_PER_GROUP
D_FF_E = D_MODEL // 4
ALPHA = (2 * DEPTH) ** 0.25
LN_EPS = 1e-5
RMS_EPS = 1e-6

SUBLANES = 8
LANES = 128
VMEM_LIMIT_BYTES = 56 * 1024 * 1024

OFF_Z = 0
OFF_XBC = OFF_Z + D_MODEL
OFF_SMALL = OFF_XBC + SSD_CONV_DIM
OFF_Q = OFF_SMALL + LANES
OFF_K = OFF_Q + D_MODEL
OFF_V = OFF_K + D_MODEL
OFF_O = OFF_V + D_MODEL
OFF_UP = OFF_O + D_MODEL
N_PROJ = OFF_UP + D_MODEL
SM_A, SM_DT, SM_F, SM_I, SM_END = 0, SSD_HEADS, 2 * SSD_HEADS, 2 * SSD_HEADS + M_HEADS, 2 * SSD_HEADS + 2 * M_HEADS
RT_G, RT_E = 0, 16
NEG_BIG = -1e30

ROW_TILE = 256
SAMPLE_ROWS = 8
SAMPLE_SEQ_BLOCK = 4

STATE_SHAPES = ((SSD_HEADS * SSD_HEADDIM, SSD_STATE), (SSD_CONV - 1, SSD_CONV_DIM),
                (M_HEADS * M_HEADDIM, M_HEADDIM), (M_HEADS, M_HEADDIM), (1, M_HEADS), (POOL_BUF, D_MODEL))
N_STATES = len(STATE_SHAPES)


def _dot(a, b):
    return jnp.dot(a.astype(BF16), b.astype(BF16), preferred_element_type=F32)


def _dot_nt(a, b):
    return lax.dot_general(a.astype(BF16), b.astype(BF16), (((1,), (1,)), ((), ())),
                           preferred_element_type=F32)


def _dot_tn(a, b):
    return lax.dot_general(a.astype(BF16), b.astype(BF16), (((0,), (0,)), ((), ())),
                           preferred_element_type=F32)


def _dot_exact(a, b):
    return jnp.dot(a, b, precision=lax.Precision.HIGHEST, preferred_element_type=F32)


def _dot_nt_exact(a, b):
    return lax.dot_general(a, b, (((1,), (1,)), ((), ())), precision=lax.Precision.HIGHEST,
                           preferred_element_type=F32)


def _sigmoid(x):
    return 1.0 / (1.0 + jnp.exp(-x))


def _silu(x):
    return x * _sigmoid(x)


def _softplus(x):
    return jnp.maximum(x, 0.0) + jnp.log1p(jnp.exp(-jnp.abs(x)))


def _layer_norm(x, g, b):
    mu = jnp.mean(x, axis=-1, keepdims=True)
    xc = x - mu
    var = jnp.mean(xc * xc, axis=-1, keepdims=True)
    return xc * lax.rsqrt(var + LN_EPS) * g + b


def _compiler_params(n_grid):
    return pltpu.CompilerParams(dimension_semantics=("arbitrary",) * n_grid,
                                vmem_limit_bytes=VMEM_LIMIT_BYTES)


def _vmem_full():
    return pl.BlockSpec(memory_space=pltpu.VMEM)


def _const_spec(shape):
    return pl.BlockSpec(shape, lambda *_: (0,) * len(shape))


def _ada_kernel(c_ref, w_ref, b_ref, o_ref):
    o_ref[...] = _dot(c_ref[...], w_ref[...]) + b_ref[...]


def _ada_call(c_all, w, b):
    n = c_all.shape[0]
    return pl.pallas_call(
        _ada_kernel,
        grid=(DEPTH, 3),
        in_specs=[pl.BlockSpec((n, D_MODEL), lambda l, j: (0, 0)),
                  pl.BlockSpec((None, D_MODEL, D_MODEL), lambda l, j: (l, 0, j)),
                  pl.BlockSpec((None, 1, D_MODEL), lambda l, j: (l, 0, j))],
        out_specs=pl.BlockSpec((None, n, D_MODEL), lambda l, j: (l, 0, j)),
        out_shape=jax.ShapeDtypeStruct((DEPTH, n, 3 * D_MODEL), F32),
        compiler_params=_compiler_params(2),
        name="ada_mod",
    )(c_all, w, b)


def _modulate(x_ref, mod_ref):
    x = x_ref[...]
    shift = mod_ref[:, :, 0:D_MODEL]
    scale = mod_ref[:, :, D_MODEL:2 * D_MODEL]
    u = x * (1.0 + scale) + shift
    return u.reshape(x.shape[0] * x.shape[1], D_MODEL)


def _proj_kernel(x_ref, mod_ref, w_ref, o_ref):
    sb, rb, _ = x_ref.shape
    u = _modulate(x_ref, mod_ref).astype(BF16)
    col = 0
    while col < N_PROJ:
        width = min(D_MODEL, N_PROJ - col)
        o_ref[:, :, col:col + width] = _dot(u, w_ref[:, col:col + width]).reshape(sb, rb, width)
        col += width


def _row_blocks(n_seq, rows, tile=ROW_TILE):
    if rows >= tile:
        return 1, tile
    return tile // rows, rows


def _proj_call(x3, mods, wcat):
    n_seq, rows, _ = x3.shape
    sb, rb = _row_blocks(n_seq, rows)
    return pl.pallas_call(
        _proj_kernel,
        grid=(n_seq // sb, rows // rb),
        in_specs=[pl.BlockSpec((sb, rb, D_MODEL), lambda i, j: (i, j, 0)),
                  pl.BlockSpec((sb, 1, 3 * D_MODEL), lambda i, j: (i, 0, 0)),
                  _vmem_full()],
        out_specs=pl.BlockSpec((sb, rb, N_PROJ), lambda i, j: (i, j, 0)),
        out_shape=jax.ShapeDtypeStruct((n_seq, rows, N_PROJ), F32),
        compiler_params=_compiler_params(2),
        name="in_proj",
    )(x3, mods, wcat)


def _mixer_chunk(proj, st_in, st_out, par, scr, *, n_seq, rows, tv, pos0, after_stage=lambda name: None):
    L = n_seq * rows
    rows_log2 = rows.bit_length() - 1
    assert rows == 1 << rows_log2
    ssd_i, conv_i, mc_i, mn_i, mm_i, pool_i = st_in
    ssd_o, conv_o, mc_o, mn_o, mm_o, pool_o = st_out
    convw_ref, convb_ref, hp_ref, snw_ref, mnw_ref, poolw_ref, pscale_ref = par
    xext_ref, pext_ref, yacc_ref = scr
    seq_rows = [slice(s * rows, (s + 1) * rows) for s in range(n_seq)]

    def per_seq(fn):
        parts = [fn(s, seq_rows[s]) for s in range(n_seq)]
        return parts[0] if n_seq == 1 else jnp.concatenate(parts, axis=0)

    row_l = lax.broadcasted_iota(jnp.int32, (L, L), 0)
    col_l = lax.broadcasted_iota(jnp.int32, (L, L), 1)
    causal = row_l >= col_l
    if n_seq > 1:
        same_seq = (row_l >> rows_log2) == (col_l >> rows_log2)
        causal = causal & same_seq

    lane = lax.broadcasted_iota(jnp.int32, (L, LANES), 1)
    pre = proj(OFF_SMALL, LANES) + hp_ref[0:1, :]
    sp = _softplus(pre)
    a_row = -jnp.exp(hp_ref[1:2, :])
    pmat = jnp.where(lane < SM_DT, sp * a_row,
                     jnp.where(lane < SM_F, sp,
                               jnp.where(lane < SM_I, -_softplus(-pre),
                                         jnp.where(lane < SM_END, pre, 0.0))))
    if tv < rows:
        row = lax.broadcasted_iota(jnp.int32, (L, LANES), 0)
        pad = jnp.where(lane < SM_I, 0.0, jnp.where(lane < SM_END, NEG_BIG, 0.0))
        pmat = jnp.where((row & (rows - 1)) < tv, pmat, pad)
    eye = (lax.broadcasted_iota(jnp.int32, (LANES, LANES), 0)
           == lax.broadcasted_iota(jnp.int32, (LANES, LANES), 1)).astype(F32)
    cum = _dot_exact(causal.astype(F32), pmat)
    pmat_t = _dot_nt_exact(eye, pmat)
    cum_t = _dot_nt_exact(eye, cum)
    if n_seq > 1:
        tot = _dot_exact(same_seq.astype(F32), pmat)
    else:
        tot = cum[L - 1:L, :]

    xext_ref[:, SUBLANES - (SSD_CONV - 1):SUBLANES, :] = conv_i[...]
    xext_ref[:, SUBLANES:SUBLANES + rows, :] = proj(OFF_XBC, SSD_CONV_DIM).reshape(n_seq, rows, SSD_CONV_DIM)
    acc = convb_ref[...]
    for k in range(SSD_CONV):
        start = SUBLANES - (SSD_CONV - 1) + k
        acc = acc + xext_ref[:, start:start + rows, :].reshape(L, SSD_CONV_DIM) * convw_ref[k:k + 1, :]
    conv_o[...] = xext_ref[:, SUBLANES + tv - (SSD_CONV - 1):SUBLANES + tv, :]
    after_stage("conv")
    xbc = _silu(acc)
    xs = xbc[:, 0:D_MODEL]
    d_row = hp_ref[2:3, :]
    gw = SSD_REP * SSD_HEADDIM
    for g in range(SSD_GROUPS):
        grp = slice(g * gw, (g + 1) * gw)
        bm = xbc[:, D_MODEL + g * SSD_STATE:D_MODEL + (g + 1) * SSD_STATE]
        cm = xbc[:, D_MODEL + (SSD_GROUPS + g) * SSD_STATE:D_MODEL + (SSD_GROUPS + g + 1) * SSD_STATE]
        cb = _dot_nt(cm, bm)
        y_state = per_seq(lambda s, rs: _dot_nt(cm[rs], ssd_i[s, grp, :]))
        xw_parts = []
        for r in range(SSD_REP):
            h = g * SSD_REP + r
            hs = slice(h * SSD_HEADDIM, (h + 1) * SSD_HEADDIM)
            cum_c = cum[:, SM_A + h:SM_A + h + 1]
            cum_r = cum_t[SM_A + h:SM_A + h + 1, :]
            dt_c = pmat[:, SM_DT + h:SM_DT + h + 1]
            dt_r = pmat_t[SM_DT + h:SM_DT + h + 1, :]
            seg = jnp.where(causal, cum_c - cum_r, -jnp.inf)
            wmat = cb * jnp.exp(seg) * dt_r
            x_h = xs[:, hs]
            y = _dot(wmat, x_h) + y_state[:, r * SSD_HEADDIM:(r + 1) * SSD_HEADDIM] * jnp.exp(cum_c)
            yacc_ref[:, hs] = y + d_row[:, h:h + 1] * x_h
            xw_parts.append(x_h * (jnp.exp(tot[:, SM_A + h:SM_A + h + 1] - cum_c) * dt_c))
            after_stage("ssd_head")
        xw = jnp.concatenate(xw_parts, axis=-1)
        for s in range(n_seq):
            upd = _dot_tn(xw[seq_rows[s]], bm[seq_rows[s]])
            t0 = s * rows if n_seq > 1 else 0
            for r in range(SSD_REP):
                h = g * SSD_REP + r
                hs = slice(h * SSD_HEADDIM, (h + 1) * SSD_HEADDIM)
                decay = jnp.exp(tot[t0:t0 + 1, SM_A + h:SM_A + h + 1])
                ssd_o[s, hs, :] = decay * ssd_i[s, hs, :] + upd[r * SSD_HEADDIM:(r + 1) * SSD_HEADDIM, :]
    yz = yacc_ref[...] * _silu(proj(OFF_Z, D_MODEL))
    y_ssd = yz * lax.rsqrt(jnp.mean(yz * yz, axis=-1, keepdims=True) + RMS_EPS) * snw_ref[...]
    after_stage("ssd")

    ym_parts = []
    for h in range(M_HEADS):
        hs = slice(h * M_HEADDIM, (h + 1) * M_HEADDIM)
        q_h = proj(OFF_Q + h * M_HEADDIM, M_HEADDIM)
        k_h = proj(OFF_K + h * M_HEADDIM, M_HEADDIM) * (M_HEADDIM ** -0.5)
        v_h = proj(OFF_V + h * M_HEADDIM, M_HEADDIM)
        o_h = proj(OFF_O + h * M_HEADDIM, M_HEADDIM)
        b_c = cum[:, SM_F + h:SM_F + h + 1]
        b_r = cum_t[SM_F + h:SM_F + h + 1, :]
        i_c = pmat[:, SM_I + h:SM_I + h + 1]
        i_r = pmat_t[SM_I + h:SM_I + h + 1, :]
        last_b = tot[:, SM_F + h:SM_F + h + 1]
        m_prev = per_seq(lambda s, rs: jnp.broadcast_to(mm_i[s, :, h:h + 1], (rows, 1)))
        dmat = jnp.where(causal, b_c - b_r + i_r, -jnp.inf)
        m_st = b_c + m_prev
        m = jnp.maximum(m_st, jnp.max(dmat, axis=-1, keepdims=True))
        wts = jnp.exp(dmat - m) * _dot_nt(q_h, k_h)
        ws = jnp.exp(m_st - m)
        cq = per_seq(lambda s, rs: _dot_nt(q_h[rs], mc_i[s, hs, :]))
        nq = per_seq(lambda s, rs: jnp.sum(q_h[rs] * mn_i[s, h:h + 1, :], axis=-1, keepdims=True))
        num = _dot(wts, v_h) + ws * cq
        den = jnp.sum(wts, axis=-1, keepdims=True) + ws * nq
        hc = num / jnp.maximum(jnp.abs(den), jnp.exp(-m))
        m_new = per_seq(lambda s, rs: jnp.broadcast_to(m[rs.stop - 1:rs.stop, :], (rows, 1)))
        wsrc = jnp.exp(last_b - b_c + i_c - m_new)
        wprev = jnp.exp(last_b + m_prev - m_new)
        vw = v_h * wsrc
        kw = k_h * wsrc
        for s in range(n_seq):
            rs = seq_rows[s]
            wp = wprev[rs.start:rs.start + 1, :]
            mc_o[s, hs, :] = wp * mc_i[s, hs, :] + _dot_tn(vw[rs], k_h[rs])
            mn_o[s, h:h + 1, :] = wp * mn_i[s, h:h + 1, :] + jnp.sum(kw[rs], axis=0, keepdims=True)
            mm_o[s, :, h:h + 1] = m_new[rs.start:rs.start + 1, :]
        mu = jnp.mean(hc, axis=-1, keepdims=True)
        hd = hc - mu
        var = jnp.mean(hd * hd, axis=-1, keepdims=True)
        ym_parts.append(hd * lax.rsqrt(var + LN_EPS) * mnw_ref[:, hs] * _sigmoid(o_h))
        after_stage("mlstm_head")
    y_m = jnp.concatenate(ym_parts, axis=-1)

    pext_ref[:, 1:POOL_BUF + 1, :] = pool_i[...]
    up = proj(OFF_UP, D_MODEL)
    pext_ref[:, POOL_BUF + 1:POOL_BUF + 1 + rows, :] = up.reshape(n_seq, rows, D_MODEL)
    pos = (lax.broadcasted_iota(jnp.int32, (L, 1), 0) & (rows - 1)) + pos0
    yp_parts = []
    for g, w in enumerate(POOL_WINDOWS):
        gs = slice(g * POOL_GW, (g + 1) * POOL_GW)
        wsum = up[:, gs]
        for j in range(1, w):
            wsum = wsum + pext_ref[:, POOL_BUF + 1 - j:POOL_BUF + 1 - j + rows, gs].reshape(L, POOL_GW)
        cnt = jnp.minimum(pos, w).astype(F32)
        dlt = wsum / cnt - up[:, gs]
        yp_parts.append(_dot(dlt, poolw_ref[g]) * pscale_ref[:, gs])
        after_stage("pool")
    pool_o[...] = pext_ref[:, 1 + tv:1 + tv + POOL_BUF, :]
    return y_ssd, y_m, jnp.concatenate(yp_parts, axis=-1)


def _merge_rows(gates, ys, wb_refs, wout_ref):
    merged = None
    for gate, y, wb_ref in zip(gates, ys, wb_refs):
        term = gate * _dot(y, wb_ref[...])
        merged = term if merged is None else merged + term
    return _dot(merged, wout_ref[...])


N_MIXER_PARAMS = 7


def _mixer_param_specs():
    return [_const_spec((SSD_CONV, SSD_CONV_DIM)), _const_spec((1, SSD_CONV_DIM)),
            _const_spec((SUBLANES, LANES)), _const_spec((1, D_MODEL)), _const_spec((1, D_MODEL)),
            _const_spec((len(POOL_WINDOWS), POOL_GW, POOL_GW)), _const_spec((1, D_MODEL))]


def _mixer_param_args(lw):
    return (lw["conv_w"], lw["conv_b"], lw["head_params"], lw["ssd_norm_w"], lw["mlstm_norm_w"],
            lw["pool_w"], lw["pool_scale"])


def _mixer_scratch(n_seq, rows):
    return [pltpu.VMEM((n_seq, SUBLANES + rows, SSD_CONV_DIM), F32),
            pltpu.VMEM((n_seq, POOL_BUF + 1 + rows, D_MODEL), F32),
            pltpu.VMEM((n_seq * rows, D_MODEL), F32)]


def _stacked_state_shapes(n_seq):
    return [jax.ShapeDtypeStruct((DEPTH, n_seq) + shp, F32) for shp in STATE_SHAPES]


def _alias_args(prev_states, n_inputs_before, n_outputs_before):
    if prev_states is None:
        return [], [], {}
    specs = [pl.BlockSpec(memory_space=pl.ANY)] * N_STATES
    aliases = {n_inputs_before + k: n_outputs_before + k for k in range(N_STATES)}
    return list(prev_states), specs, aliases


N_PROJ_ALL = N_PROJ + 3 * D_MODEL
OFF_GL = N_PROJ


def _column_pieces(lo, hi, width):
    return [(off, min(width, hi - off)) for off in range(lo, hi, width)]


def _prompt_mixer_kernel(x_ref, mod_ref, xn_ref, modn_ref, wcat_ref, wgl_ref, gb_ref, wbs_ref, wbm_ref, wbp_ref,
                         wout_ref, lng_ref, lnb_ref, *rest, n_alias):
    par = rest[:N_MIXER_PARAMS]
    rest = rest[N_MIXER_PARAMS + n_alias:]
    o_ref = rest[0]
    states = rest[1:1 + N_STATES]
    scr = rest[1 + N_STATES:4 + N_STATES]
    pscr_ref = rest[4 + N_STATES]
    L = x_ref.shape[0]
    b = pl.program_id(0)
    c = pl.program_id(1)

    def modulated(xr, mr):
        return (xr[...] * (1.0 + mr[:, D_MODEL:2 * D_MODEL]) + mr[:, 0:D_MODEL]).astype(BF16)

    def project_into_scratch(ub, piece):
        off, width = piece
        if off < OFF_GL:
            w = wcat_ref[:, off:off + width]
        else:
            w = wgl_ref[:, off - OFF_GL:off - OFF_GL + width]
        pscr_ref[:, off:off + width] = _dot(ub, w)

    @pl.when(c == 0)
    def _fresh_prompt_states():
        for ref in states:
            ref[...] = jnp.zeros(ref.shape, ref.dtype)

    @pl.when(jnp.logical_and(b == 0, c == 0))
    def _first_chunk_projections():
        ub0 = modulated(x_ref, mod_ref)
        for piece in _column_pieces(0, N_PROJ, D_MODEL) + _column_pieces(OFF_GL, N_PROJ_ALL, D_MODEL):
            project_into_scratch(ub0, piece)

    ub_next = modulated(xn_ref, modn_ref)
    ready = []
    released_by = {
        "conv": [_column_pieces(OFF_XBC, OFF_Q, 896)],
        "ssd": [_column_pieces(OFF_Z, OFF_XBC, D_MODEL)],
        "mlstm_head": [[(off + h * M_HEADDIM, M_HEADDIM) for off in (OFF_Q, OFF_K, OFF_V, OFF_O)]
                       for h in range(M_HEADS)],
        "pool": [_column_pieces(OFF_UP, N_PROJ, D_MODEL)],
    }
    pieces_per_call = {"conv": 2, "ssd_head": 1}

    def after_stage(name):
        if released_by.get(name):
            ready.extend(released_by[name].pop(0))
        for _ in range(min(len(ready), pieces_per_call.get(name, len(ready)))):
            project_into_scratch(ub_next, ready.pop(0))

    gates = [_sigmoid(pscr_ref[:, OFF_GL + i * D_MODEL:OFF_GL + (i + 1) * D_MODEL]
                      + gb_ref[:, i * D_MODEL:(i + 1) * D_MODEL]) for i in range(3)]
    ready.extend(_column_pieces(OFF_GL, N_PROJ_ALL, 768))
    proj = lambda off, width: pscr_ref[:, off:off + width]
    ys = _mixer_chunk(proj, states, states, par, scr, n_seq=1, rows=L, tv=L, pos0=c * L + 1,
                      after_stage=after_stage)
    after_stage("rest")
    assert not ready and not any(released_by.values())
    mix = _merge_rows(gates, ys, (wbs_ref, wbm_ref, wbp_ref), wout_ref)
    o_ref[...] = _layer_norm(ALPHA * x_ref[...] + (1.0 + mod_ref[:, 2 * D_MODEL:3 * D_MODEL]) * mix,
                             lng_ref[...], lnb_ref[...])


def _prompt_mixer_call(l, x3, mods, prev_states, lw):
    n_seq, rows, _ = x3.shape
    L = CHUNK
    nc = rows // L

    def next_chunk(b, c):
        flat = jnp.minimum(b * nc + c + 1, n_seq * nc - 1)
        return flat // nc, flat % nc

    row_spec = pl.BlockSpec((None, L, D_MODEL), lambda b, c: (b, c, 0))
    mod_spec = pl.BlockSpec((None, 1, 3 * D_MODEL), lambda b, c: (b, 0, 0))
    next_row_spec = pl.BlockSpec((None, L, D_MODEL), lambda b, c: next_chunk(b, c) + (0,))
    next_mod_spec = pl.BlockSpec((None, 1, 3 * D_MODEL), lambda b, c: (next_chunk(b, c)[0], 0, 0))
    vec_spec = _const_spec((1, D_MODEL))
    state_specs = [pl.BlockSpec((None, 1) + shp, lambda b, c: (l, b, 0, 0)) for shp in STATE_SHAPES]
    in_specs = [row_spec, mod_spec, next_row_spec, next_mod_spec,
                _vmem_full(), _vmem_full(), _const_spec((1, 3 * D_MODEL)),
                _vmem_full(), _vmem_full(), _vmem_full(), _vmem_full(), vec_spec, vec_spec]
    in_specs += _mixer_param_specs()
    alias_in, alias_specs, aliases = _alias_args(prev_states, len(in_specs), 1)
    outs = pl.pallas_call(
        functools.partial(_prompt_mixer_kernel, n_alias=len(alias_in)),
        grid=(n_seq, nc),
        in_specs=in_specs + alias_specs,
        out_specs=[row_spec] + state_specs,
        out_shape=[jax.ShapeDtypeStruct(x3.shape, F32)] + _stacked_state_shapes(n_seq),
        scratch_shapes=_mixer_scratch(1, L) + [pltpu.VMEM((L, N_PROJ_ALL), F32)],
        input_output_aliases=aliases,
        compiler_params=_compiler_params(2),
        name="prompt_mixers",
    )(x3, mods, x3, mods, lw["w_cat"], lw["w_gl"], lw["gate_b"], lw["w_br_ssd"], lw["w_br_mlstm"],
      lw["w_br_pool"], lw["w_out"], lw["ln1_g"], lw["ln1_b"], *_mixer_param_args(lw), *alias_in)
    return outs[0], outs[1:]


def _sample_mixer_kernel(proj_ref, *rest, tv, n_alias):
    st_in = rest[:N_STATES]
    par = rest[N_STATES:N_STATES + N_MIXER_PARAMS]
    rest = rest[N_STATES + N_MIXER_PARAMS + n_alias:]
    y_refs = rest[:3]
    st_out = rest[3:3 + N_STATES]
    scr = rest[3 + N_STATES:]
    n_blk, rows, _ = proj_ref.shape
    proj = lambda off, width: proj_ref[:, :, off:off + width].reshape(n_blk * rows, width)
    ys = _mixer_chunk(proj, st_in, st_out, par, scr, n_seq=n_blk, rows=rows, tv=tv, pos0=1 + POOL_BUF)
    for y_ref, y in zip(y_refs, ys):
        y_ref[...] = y.reshape(n_blk, rows, D_MODEL)


def _sample_mixer_call(l, proj, states_in, prev_states, lw, *, tv):
    n_seq, rows, _ = proj.shape
    nb = SAMPLE_SEQ_BLOCK
    state_specs = [pl.BlockSpec((None, nb) + shp, lambda i: (l, i, 0, 0)) for shp in STATE_SHAPES]
    y_spec = pl.BlockSpec((nb, rows, D_MODEL), lambda i: (i, 0, 0))
    y_shape = jax.ShapeDtypeStruct((n_seq, rows, D_MODEL), F32)
    in_specs = [pl.BlockSpec((nb, rows, N_PROJ), lambda i: (i, 0, 0))] + state_specs + _mixer_param_specs()
    alias_in, alias_specs, aliases = _alias_args(prev_states, len(in_specs), 3)
    outs = pl.pallas_call(
        functools.partial(_sample_mixer_kernel, tv=tv, n_alias=len(alias_in)),
        grid=(n_seq // nb,),
        in_specs=in_specs + alias_specs,
        out_specs=[y_spec, y_spec, y_spec] + state_specs,
        out_shape=[y_shape, y_shape, y_shape] + _stacked_state_shapes(n_seq),
        scratch_shapes=_mixer_scratch(nb, rows),
        input_output_aliases=aliases,
        compiler_params=_compiler_params(1),
        name="sample_mixers",
    )(proj, *states_in, *_mixer_param_args(lw), *alias_in)
    return outs[:3], outs[3:]


def _merge_kernel(x_ref, mod_ref, yssd_ref, ym_ref, ypool_ref, wgl_ref, gb_ref,
                  wbs_ref, wbm_ref, wbp_ref, wout_ref, lng_ref, lnb_ref, o_ref):
    sb, rb, _ = x_ref.shape
    n = sb * rb
    ub = _modulate(x_ref, mod_ref).astype(BF16)
    ys = [r[...].reshape(n, D_MODEL) for r in (yssd_ref, ym_ref, ypool_ref)]
    gates = [_sigmoid(_dot(ub, wgl_ref[:, i * D_MODEL:(i + 1) * D_MODEL]) + gb_ref[:, i * D_MODEL:(i + 1) * D_MODEL])
             for i in range(3)]
    mix = _merge_rows(gates, ys, (wbs_ref, wbm_ref, wbp_ref), wout_ref).reshape(sb, rb, D_MODEL)
    gate_a = mod_ref[:, :, 2 * D_MODEL:3 * D_MODEL]
    o_ref[...] = _layer_norm(ALPHA * x_ref[...] + (1.0 + gate_a) * mix, lng_ref[...], lnb_ref[...])


def _merge_call(x3, mods, ys, lw):
    n_seq, rows, _ = x3.shape
    sb, rb = _row_blocks(n_seq, rows)
    row_spec = pl.BlockSpec((sb, rb, D_MODEL), lambda i, j: (i, j, 0))
    vec_spec = _const_spec((1, D_MODEL))
    return pl.pallas_call(
        _merge_kernel,
        grid=(n_seq // sb, rows // rb),
        in_specs=[row_spec, pl.BlockSpec((sb, 1, 3 * D_MODEL), lambda i, j: (i, 0, 0)),
                  row_spec, row_spec, row_spec,
                  _vmem_full(), _const_spec((1, 3 * D_MODEL)),
                  _vmem_full(), _vmem_full(), _vmem_full(), _vmem_full(), vec_spec, vec_spec],
        out_specs=row_spec,
        out_shape=jax.ShapeDtypeStruct(x3.shape, F32),
        compiler_params=_compiler_params(2),
        name="merge_norm",
    )(x3, mods, ys[0], ys[1], ys[2], lw["w_gl"], lw["gate_b"], lw["w_br_ssd"], lw["w_br_mlstm"],
      lw["w_br_pool"], lw["w_out"], lw["ln1_g"], lw["ln1_b"])


MOE_TILE = 512
MOE_BLOCK = 128
MOE_SLOT_BLOCKS = MOE_TILE // MOE_BLOCK + N_EGROUPS - 1
MOE_SLOTS = MOE_SLOT_BLOCKS * MOE_BLOCK
GROUP_FF = EXP_PER_GROUP * D_FF_E
SIDE_POS, SIDE_HI, SIDE_MID, SIDE_LO = 0, 8, 16, 24


def _split3(v):
    hi = v.astype(BF16).astype(F32)
    mid = (v - hi).astype(BF16).astype(F32)
    lo = (v - hi - mid).astype(BF16).astype(F32)
    return hi, mid, lo


def _pad_rows(v, n):
    return jnp.concatenate([v, jnp.zeros((n - v.shape[0], v.shape[1]), v.dtype)], axis=0)


def _route_t(logits_t):
    t = logits_t.shape[1]
    row_g = lax.broadcasted_iota(jnp.int32, (SUBLANES, t), 0)
    lg = jnp.where(row_g < N_EGROUPS, logits_t[RT_G:RT_G + SUBLANES, :], -jnp.inf)
    g_max = jnp.max(lg, axis=0, keepdims=True)
    g_idx = jnp.min(jnp.where(lg == g_max, row_g, SUBLANES), axis=0, keepdims=True)
    g_prob = 1.0 / jnp.sum(jnp.exp(lg - g_max), axis=0, keepdims=True)
    row_e = lax.broadcasted_iota(jnp.int32, (N_EXPERTS, t), 0)
    le = jnp.where((row_e >> 2) == g_idx, logits_t[RT_E:RT_E + N_EXPERTS, :], -jnp.inf)
    v1 = jnp.max(le, axis=0, keepdims=True)
    i1 = jnp.min(jnp.where(le == v1, row_e, N_EXPERTS), axis=0, keepdims=True)
    le2 = jnp.where(row_e == i1, -jnp.inf, le)
    v2 = jnp.max(le2, axis=0, keepdims=True)
    i2 = jnp.min(jnp.where(le2 == v2, row_e, N_EXPERTS), axis=0, keepdims=True)
    e2 = jnp.exp(v2 - v1)
    p1 = g_prob / (1.0 + e2)
    p2 = g_prob * e2 / (1.0 + e2)
    wts = jnp.where(row_e == i1, p1, 0.0) + jnp.where(row_e == i2, p2, 0.0)
    w4 = None
    for g in range(N_EGROUPS):
        part = jnp.where(g_idx == g, wts[g * EXP_PER_GROUP:(g + 1) * EXP_PER_GROUP, :], 0.0)
        w4 = part if w4 is None else w4 + part
    return g_idx, w4


def _moe_kernel(x_ref, mod_ref, wrt_ref, brt_ref, wg_ref, wu_ref, wd_ref, lng_ref, lnb_ref, o_ref,
                sx_ref, sw_ref, so_ref):
    sb, rb, _ = x_ref.shape
    t = sb * rb
    u = _modulate(x_ref, mod_ref)
    u_hi = u.astype(BF16)
    u_lo = (u - u_hi.astype(F32)).astype(BF16)
    w = wrt_ref[...]
    w_hi = w.astype(BF16)
    w_lo = (w - w_hi.astype(F32)).astype(BF16)
    logits_t = _dot_nt(w_hi, u_hi) + _dot_nt(w_hi, u_lo) + _dot_nt(w_lo, u_hi) + brt_ref[...]
    g_idx, w4 = _route_t(logits_t)

    row_g = lax.broadcasted_iota(jnp.int32, (SUBLANES, t), 0)
    onehot_t = (row_g == g_idx).astype(F32)
    before = (lax.broadcasted_iota(jnp.int32, (t, t), 0) < lax.broadcasted_iota(jnp.int32, (t, t), 1))
    rank = _dot(onehot_t, before.astype(F32))
    cnt = jnp.sum(onehot_t, axis=1, keepdims=True)
    nblk = jnp.floor((cnt + (MOE_BLOCK - 1)) * (1.0 / MOE_BLOCK))
    sub = lax.broadcasted_iota(jnp.int32, (SUBLANES, 1), 0)
    first = jnp.zeros((SUBLANES, 1), F32)
    running = jnp.zeros((1, 1), F32)
    for g in range(1, N_EGROUPS):
        running = running + nblk[g - 1:g, :]
        first = first + jnp.where(sub == g, running, 0.0)
    pos_t = jnp.sum(onehot_t * (first * MOE_BLOCK + rank), axis=0, keepdims=True)
    blk_lane = lax.broadcasted_iota(jnp.int32, (SUBLANES, LANES), 1).astype(F32)
    in_blk = (blk_lane >= first) & (blk_lane < first + nblk)
    blk_group = jnp.sum(jnp.where(in_blk, sub.astype(F32), 0.0), axis=0, keepdims=True)

    w_hi4, w_mid4, w_lo4 = _split3(w4)
    side_t = jnp.concatenate([_pad_rows(pos_t, SUBLANES), _pad_rows(w_hi4, SUBLANES),
                              _pad_rows(w_mid4, SUBLANES), _pad_rows(w_lo4, SUBLANES),
                              jnp.zeros((LANES - 4 * SUBLANES, t), F32)], axis=0)
    side = side_t.T

    slot_r = lax.broadcasted_iota(jnp.int32, (MOE_SLOTS, t), 0)
    send = (slot_r == pos_t.astype(jnp.int32)).astype(BF16)
    payload = jnp.concatenate([u_hi, side.astype(BF16)], axis=-1)
    sorted_rows = _dot(send, payload)
    sx_ref[...] = sorted_rows[:, 0:D_MODEL].astype(BF16)
    sw_ref[...] = sorted_rows[:, D_MODEL:D_MODEL + LANES]

    for i in range(MOE_SLOT_BLOCKS):
        g = blk_group[0, i].astype(jnp.int32)
        rows = slice(i * MOE_BLOCK, (i + 1) * MOE_BLOCK)
        xb = sx_ref[rows, :]
        hid = _silu(_dot(xb, wg_ref[g])) * _dot(xb, wu_ref[g])
        ws = sw_ref[rows, :]
        w_blk = (ws[:, SIDE_HI:SIDE_HI + EXP_PER_GROUP] + ws[:, SIDE_MID:SIDE_MID + EXP_PER_GROUP]
                 + ws[:, SIDE_LO:SIDE_LO + EXP_PER_GROUP])
        hid = jnp.concatenate([hid[:, r * D_FF_E:(r + 1) * D_FF_E] * w_blk[:, r:r + 1]
                               for r in range(EXP_PER_GROUP)], axis=-1)
        out = _dot(hid, wd_ref[g])
        out_hi = out.astype(BF16)
        so_ref[rows, 0:D_MODEL] = out_hi
        so_ref[rows, D_MODEL:2 * D_MODEL] = (out - out_hi.astype(F32)).astype(BF16)

    slot_c = lax.broadcasted_iota(jnp.int32, (t, MOE_SLOTS), 1)
    fetch = (slot_c == side[:, SIDE_POS:SIDE_POS + 1].astype(jnp.int32)).astype(BF16)
    back = _dot(fetch, so_ref[...])
    ffn = back[:, 0:D_MODEL] + back[:, D_MODEL:2 * D_MODEL]
    gate_f = mod_ref[:, :, 2 * D_MODEL:3 * D_MODEL]
    o_ref[...] = _layer_norm(ALPHA * x_ref[...] + (1.0 + gate_f) * ffn.reshape(sb, rb, D_MODEL),
                             lng_ref[...], lnb_ref[...])


def _moe_call(x3, mods, lw):
    n_seq, rows, _ = x3.shape
    sb, rb = _row_blocks(n_seq, rows, MOE_TILE)
    row_spec = pl.BlockSpec((sb, rb, D_MODEL), lambda i, j: (i, j, 0))
    vec_spec = _const_spec((1, D_MODEL))
    return pl.pallas_call(
        _moe_kernel,
        grid=(n_seq // sb, rows // rb),
        in_specs=[row_spec, pl.BlockSpec((sb, 1, 3 * D_MODEL), lambda i, j: (i, 0, 0)),
                  _vmem_full(), _const_spec((LANES, 1)),
                  _vmem_full(), _vmem_full(), _vmem_full(), vec_spec, vec_spec],
        out_specs=row_spec,
        out_shape=jax.ShapeDtypeStruct(x3.shape, F32),
        scratch_shapes=[pltpu.VMEM((MOE_SLOTS, D_MODEL), BF16), pltpu.VMEM((MOE_SLOTS, LANES), F32),
                        pltpu.VMEM((MOE_SLOTS, 2 * D_MODEL), BF16)],
        compiler_params=_compiler_params(2),
        name="moe_norm",
    )(x3, mods, lw["w_rt_t"], lw["b_rt_col"], lw["w_e_gate"], lw["w_e_up"], lw["w_e_down"],
      lw["ln2_g"], lw["ln2_b"])


def _split_w_in(w):
    sizes = (D_MODEL, SSD_CONV_DIM, SSD_HEADS, D_MODEL, D_MODEL, D_MODEL, M_HEADS, M_HEADS, D_MODEL,
             D_MODEL, 3 * D_MODEL)
    out, off = [], 0
    for s in sizes:
        out.append(w[:, off:off + s])
        off += s
    return out


def _pad_lanes(v, width=LANES):
    return jnp.pad(v, ((0, 0), (0, width - v.shape[-1])))


def _layer_weights(l, p):
    wz, wxbc, wdt, wq, wk, wv, wi, wf, wo, wup, wgl = _split_w_in(p["w_in"][l])
    w_small = _pad_lanes(jnp.concatenate([wdt, wdt, wf, wi], axis=1))
    gate_b = p["mlstm_gate_b"][l]
    bias_row = jnp.concatenate([p["ssd_dt_bias"][l], p["ssd_dt_bias"][l], gate_b[M_HEADS:], gate_b[:M_HEADS]])
    head_params = jnp.concatenate([
        _pad_lanes(bias_row[None]), _pad_lanes(p["ssd_A_log"][l][None]), _pad_lanes(p["ssd_D"][l][None]),
        jnp.zeros((SUBLANES - 3, LANES), F32)], axis=0)
    w_rt = jnp.concatenate([_pad_lanes(p["w_rt_group"][l], RT_E), _pad_lanes(p["w_rt_expert"][l], LANES - RT_E)],
                           axis=1)
    b_rt = jnp.concatenate([_pad_lanes(p["b_rt_group"][l][None], RT_E),
                            _pad_lanes(p["b_rt_expert"][l][None], LANES - RT_E)], axis=1)
    group_cols = lambda w: jnp.transpose(w.reshape(N_EGROUPS, EXP_PER_GROUP, D_MODEL, D_FF_E),
                                         (0, 2, 1, 3)).reshape(N_EGROUPS, D_MODEL, GROUP_FF)
    row = lambda v: v[None]
    return dict(
        w_cat=jnp.concatenate([wz, wxbc, w_small, wq, wk, wv, wo, wup], axis=1).astype(BF16),
        w_gl=wgl.astype(BF16),
        conv_w=p["conv_w"][l], conv_b=row(p["conv_b"][l]), head_params=head_params,
        ssd_norm_w=row(p["ssd_norm_w"][l]), mlstm_norm_w=row(p["mlstm_norm_w"][l]),
        pool_w=p["pool_w"][l].astype(BF16), pool_scale=row(p["pool_scale"][l]),
        gate_b=row(p["gate_b"][l]),
        w_br_ssd=p["w_br_ssd"][l].astype(BF16), w_br_mlstm=p["w_br_mlstm"][l].astype(BF16),
        w_br_pool=p["w_br_pool"][l].astype(BF16), w_out=p["w_out"][l].astype(BF16),
        ln1_g=row(p["ln1_g"][l]), ln1_b=row(p["ln1_b"][l]),
        w_rt_t=w_rt.T, b_rt_col=b_rt.T,
        w_e_gate=group_cols(p["w_e_gate"][l]).astype(BF16), w_e_up=group_cols(p["w_e_up"][l]).astype(BF16),
        w_e_down=p["w_e_down"][l].reshape(N_EGROUPS, GROUP_FF, D_MODEL).astype(BF16),
        ln2_g=row(p["ln2_g"][l]), ln2_b=row(p["ln2_b"][l]),
    )


def _flat_states(ssd, conv, mc, mn, mm, pool):
    return tuple(a.reshape(a.shape[:2] + shp) for a, shp in zip((ssd, conv, mc, mn, mm, pool), STATE_SHAPES))


def _unflat_states(states):
    ssd, conv, mc, mn, mm, pool = states
    d, n = ssd.shape[:2]
    return (ssd.reshape(d, n, SSD_HEADS, SSD_HEADDIM, SSD_STATE), conv,
            mc.reshape(d, n, M_HEADS, M_HEADDIM, M_HEADDIM), mn, mm.reshape(d, n, M_HEADS), pool)


def kernel(x_prompt, x_sample, state_ssd, state_conv, state_mlstm_C, state_mlstm_n, state_mlstm_m, state_pool, c_prompt, c_sample, w_ada_mix, b_ada_mix, w_in, conv_w, conv_b, ssd_A_log, ssd_dt_bias, ssd_D, ssd_norm_w, mlstm_gate_b, mlstm_norm_w, pool_w, pool_scale, gate_b, w_br_ssd, w_br_mlstm, w_br_pool, w_out, ln1_g, ln1_b, w_ada_ffn, b_ada_ffn, w_rt_group, b_rt_group, w_rt_expert, b_rt_expert, w_e_gate, w_e_up, w_e_down, ln2_g, ln2_b):
    params = dict(w_in=w_in, conv_w=conv_w, conv_b=conv_b, ssd_A_log=ssd_A_log, ssd_dt_bias=ssd_dt_bias,
                  ssd_D=ssd_D, ssd_norm_w=ssd_norm_w, mlstm_gate_b=mlstm_gate_b, mlstm_norm_w=mlstm_norm_w,
                  pool_w=pool_w, pool_scale=pool_scale, gate_b=gate_b, w_br_ssd=w_br_ssd,
                  w_br_mlstm=w_br_mlstm, w_br_pool=w_br_pool, w_out=w_out, ln1_g=ln1_g, ln1_b=ln1_b,
                  w_rt_group=w_rt_group, b_rt_group=b_rt_group, w_rt_expert=w_rt_expert,
                  b_rt_expert=b_rt_expert, w_e_gate=w_e_gate, w_e_up=w_e_up, w_e_down=w_e_down,
                  ln2_g=ln2_g, ln2_b=ln2_b)
    bp, seq, _ = x_prompt.shape
    bs, dec_seq, _ = x_sample.shape
    assert seq % CHUNK == 0 and 1 <= dec_seq <= SAMPLE_ROWS and bs % SAMPLE_SEQ_BLOCK == 0

    c_all = jnp.concatenate([c_prompt, c_sample], axis=0)
    mods_mix = _ada_call(c_all, w_ada_mix, b_ada_mix[:, None, :])
    mods_ffn = _ada_call(c_all, w_ada_ffn, b_ada_ffn[:, None, :])

    sample_in = _flat_states(state_ssd, state_conv, state_mlstm_C, state_mlstm_n, state_mlstm_m, state_pool)
    xp = x_prompt
    xs = jnp.pad(x_sample, ((0, 0), (0, SAMPLE_ROWS - dec_seq), (0, 0)))
    p_states, s_states = None, None
    for l in range(DEPTH):
        lw = _layer_weights(l, params)
        xp, p_states = _prompt_mixer_call(l, xp, mods_mix[l, :bp, None, :], p_states, lw)
        xp = _moe_call(xp, mods_ffn[l, :bp, None, :], lw)

        mods_s = mods_mix[l, bp:, None, :]
        proj = _proj_call(xs, mods_s, lw["w_cat"])
        ys, s_states = _sample_mixer_call(l, proj, sample_in, s_states, lw, tv=dec_seq)
        xs = _merge_call(xs, mods_s, ys, lw)
        xs = _moe_call(xs, mods_ffn[l, bp:, None, :], lw)
    return (xp, xs[:, :dec_seq]) + _unflat_states(p_states) + _unflat_states(s_states)
```

```python
import functools

import jax
import jax.numpy as jnp
from jax import lax
from jax.experimental import pallas as pl
from jax.experimental.pallas import tpu as pltpu

F32 = jnp.float32
BF16 = jnp.bfloat16

D_MODEL = 1024
DEPTH = 4
SSD_HEADS = 16
SSD_HEADDIM = 64
SSD_GROUPS = 2
SSD_REP = SSD_HEADS // SSD_GROUPS
SSD_STATE = 128
SSD_CONV = 4
SSD_CONV_DIM = D_MODEL + 2 * SSD_GROUPS * SSD_STATE
CHUNK = 128
M_HEADS = 4
M_HEADDIM = D_MODEL // M_HEADS
POOL_WINDOWS = (2, 4, 8, 16)
POOL_GW = D_MODEL // len(POOL_WINDOWS)
POOL_BUF = max(POOL_WINDOWS) - 1
N_EGROUPS = 4
EXP_PER_GROUP = 4
N_EXPERTS = N_EGROUPS * EXP_PER_GROUP
D_FF_E = D_MODEL // 4
ALPHA = (2 * DEPTH) ** 0.25
LN_EPS = 1e-5
RMS_EPS = 1e-6

SUBLANES = 8
LANES = 128
VMEM_LIMIT_BYTES = 56 * 1024 * 1024

OFF_Z = 0
OFF_XBC = OFF_Z + D_MODEL
OFF_SMALL = OFF_XBC + SSD_CONV_DIM
OFF_Q = OFF_SMALL + LANES
OFF_K = OFF_Q + D_MODEL
OFF_V = OFF_K + D_MODEL
OFF_O = OFF_V + D_MODEL
OFF_UP = OFF_O + D_MODEL
N_PROJ = OFF_UP + D_MODEL
SM_A, SM_DT, SM_F, SM_I, SM_END = 0, SSD_HEADS, 2 * SSD_HEADS, 2 * SSD_HEADS + M_HEADS, 2 * SSD_HEADS + 2 * M_HEADS
RT_G, RT_E = 0, 16
NEG_BIG = -1e30

ROW_TILE = 256
SAMPLE_ROWS = 8
SAMPLE_SEQ_BLOCK = 4

STATE_SHAPES = ((SSD_HEADS * SSD_HEADDIM, SSD_STATE), (SSD_CONV - 1, SSD_CONV_DIM),
                (M_HEADS * M_HEADDIM, M_HEADDIM), (M_HEADS, M_HEADDIM), (1, M_HEADS), (POOL_BUF, D_MODEL))
N_STATES = len(STATE_SHAPES)


def _dot(a, b):
    return jnp.dot(a.astype(BF16), b.astype(BF16), preferred_element_type=F32)


def _dot_nt(a, b):
    return lax.dot_general(a.astype(BF16), b.astype(BF16), (((1,), (1,)), ((), ())),
                           preferred_element_type=F32)


def _dot_tn(a, b):
    return lax.dot_general(a.astype(BF16), b.astype(BF16), (((0,), (0,)), ((), ())),
                           preferred_element_type=F32)


def _split3(v):
    hi = v.astype(BF16)
    r1 = v - hi.astype(F32)
    mid = r1.astype(BF16)
    lo = (r1 - mid.astype(F32)).astype(BF16)
    return hi, mid, lo


def _select_dot(sel, v):
    n = v.shape[1]
    out = jnp.dot(sel.astype(BF16), jnp.concatenate(_split3(v), axis=1), preferred_element_type=F32)
    return out[:, 0:n] + out[:, n:2 * n] + out[:, 2 * n:3 * n]


def _select_dot_nt(sel, v):
    n = v.shape[0]
    out = lax.dot_general(sel.astype(BF16), jnp.concatenate(_split3(v), axis=0), (((1,), (1,)), ((), ())),
                          preferred_element_type=F32)
    return out[:, 0:n] + out[:, n:2 * n] + out[:, 2 * n:3 * n]


def _sigmoid(x):
    return 1.0 / (1.0 + jnp.exp(-x))


def _silu(x):
    return x * _sigmoid(x)


def _softplus(x):
    return jnp.maximum(x, 0.0) + jnp.log1p(jnp.exp(-jnp.abs(x)))


def _layer_norm(x, g, b):
    mu = jnp.mean(x, axis=-1, keepdims=True)
    xc = x - mu
    var = jnp.mean(xc * xc, axis=-1, keepdims=True)
    return xc * lax.rsqrt(var + LN_EPS) * g + b


def _compiler_params(n_grid):
    return pltpu.CompilerParams(dimension_semantics=("arbitrary",) * n_grid,
                                vmem_limit_bytes=VMEM_LIMIT_BYTES)


def _vmem_full():
    return pl.BlockSpec(memory_space=pltpu.VMEM)


def _const_spec(shape):
    return pl.BlockSpec(shape, lambda *_: (0,) * len(shape))


def _ada_kernel(c_ref, w_ref, b_ref, o_ref):
    o_ref[...] = _dot(c_ref[...], w_ref[...]) + b_ref[...]


def _ada_call(c_all, w, b):
    n = c_all.shape[0]
    return pl.pallas_call(
        _ada_kernel,
        grid=(DEPTH, 3),
        in_specs=[pl.BlockSpec((n, D_MODEL), lambda l, j: (0, 0)),
                  pl.BlockSpec((None, D_MODEL, D_MODEL), lambda l, j: (l, 0, j)),
                  pl.BlockSpec((None, 1, D_MODEL), lambda l, j: (l, 0, j))],
        out_specs=pl.BlockSpec((None, n, D_MODEL), lambda l, j: (l, 0, j)),
        out_shape=jax.ShapeDtypeStruct((DEPTH, n, 3 * D_MODEL), F32),
        compiler_params=_compiler_params(2),
        name="ada_mod",
    )(c_all, w, b)


def _modulate(x_ref, mod_ref):
    x = x_ref[...]
    shift = mod_ref[:, :, 0:D_MODEL]
    scale = mod_ref[:, :, D_MODEL:2 * D_MODEL]
    u = x * (1.0 + scale) + shift
    return u.reshape(x.shape[0] * x.shape[1], D_MODEL)


def _proj_kernel(x_ref, mod_ref, w_ref, o_ref):
    sb, rb, _ = x_ref.shape
    u = _modulate(x_ref, mod_ref).astype(BF16)
    col = 0
    while col < N_PROJ:
        width = min(D_MODEL, N_PROJ - col)
        o_ref[:, :, col:col + width] = _dot(u, w_ref[:, col:col + width]).reshape(sb, rb, width)
        col += width


def _row_blocks(n_seq, rows, tile=ROW_TILE):
    if rows >= tile:
        return 1, tile
    return tile // rows, rows


def _proj_call(x3, mods, wcat):
    n_seq, rows, _ = x3.shape
    sb, rb = _row_blocks(n_seq, rows)
    return pl.pallas_call(
        _proj_kernel,
        grid=(n_seq // sb, rows // rb),
        in_specs=[pl.BlockSpec((sb, rb, D_MODEL), lambda i, j: (i, j, 0)),
                  pl.BlockSpec((sb, 1, 3 * D_MODEL), lambda i, j: (i, 0, 0)),
                  _vmem_full()],
        out_specs=pl.BlockSpec((sb, rb, N_PROJ), lambda i, j: (i, j, 0)),
        out_shape=jax.ShapeDtypeStruct((n_seq, rows, N_PROJ), F32),
        compiler_params=_compiler_params(2),
        name="in_proj",
    )(x3, mods, wcat)


def _mixer_chunk(proj, st_in, st_out, par, scr, *, n_seq, rows, tv, pos0, after_stage=lambda name: None):
    L = n_seq * rows
    rows_log2 = rows.bit_length() - 1
    assert rows == 1 << rows_log2
    ssd_i, conv_i, mc_i, mn_i, mm_i, pool_i = st_in
    ssd_o, conv_o, mc_o, mn_o, mm_o, pool_o = st_out
    convw_ref, convb_ref, hp_ref, snw_ref, mnw_ref, poolw_ref, pscale_ref = par
    xext_ref, pext_ref, yacc_ref = scr
    seq_rows = [slice(s * rows, (s + 1) * rows) for s in range(n_seq)]

    def per_seq(fn):
        parts = [fn(s, seq_rows[s]) for s in range(n_seq)]
        return parts[0] if n_seq == 1 else jnp.concatenate(parts, axis=0)

    row_l = lax.broadcasted_iota(jnp.int32, (L, L), 0)
    col_l = lax.broadcasted_iota(jnp.int32, (L, L), 1)
    causal = row_l >= col_l
    if n_seq > 1:
        same_seq = (row_l >> rows_log2) == (col_l >> rows_log2)
        causal = causal & same_seq

    lane = lax.broadcasted_iota(jnp.int32, (L, LANES), 1)
    pre = proj(OFF_SMALL, LANES) + hp_ref[0:1, :]
    sp = _softplus(pre)
    a_row = -jnp.exp(hp_ref[1:2, :])
    pmat = jnp.where(lane < SM_DT, sp * a_row,
                     jnp.where(lane < SM_F, sp,
                               jnp.where(lane < SM_I, -_softplus(-pre),
                                         jnp.where(lane < SM_END, pre, 0.0))))
    if tv < rows:
        row = lax.broadcasted_iota(jnp.int32, (L, LANES), 0)
        pad = jnp.where(lane < SM_I, 0.0, jnp.where(lane < SM_END, NEG_BIG, 0.0))
        pmat = jnp.where((row & (rows - 1)) < tv, pmat, pad)
    eye = (lax.broadcasted_iota(jnp.int32, (LANES, LANES), 0)
           == lax.broadcasted_iota(jnp.int32, (LANES, LANES), 1))
    cum = _select_dot(causal, pmat)
    pmat_t = _select_dot_nt(eye, pmat)
    cum_t = _select_dot_nt(eye, cum)
    if n_seq > 1:
        tot = _select_dot(same_seq, pmat)
    else:
        tot = cum[L - 1:L, :]

    xext_ref[:, SUBLANES - (SSD_CONV - 1):SUBLANES, :] = conv_i[...]
    xext_ref[:, SUBLANES:SUBLANES + rows, :] = proj(OFF_XBC, SSD_CONV_DIM).reshape(n_seq, rows, SSD_CONV_DIM)
    acc = convb_ref[...]
    for k in range(SSD_CONV):
        start = SUBLANES - (SSD_CONV - 1) + k
        acc = acc + xext_ref[:, start:start + rows, :].reshape(L, SSD_CONV_DIM) * convw_ref[k:k + 1, :]
    conv_o[...] = xext_ref[:, SUBLANES + tv - (SSD_CONV - 1):SUBLANES + tv, :]
    after_stage("conv")
    xbc = _silu(acc)
    xs = xbc[:, 0:D_MODEL]
    d_row = hp_ref[2:3, :]
    gw = SSD_REP * SSD_HEADDIM
    for g in range(SSD_GROUPS):
        grp = slice(g * gw, (g + 1) * gw)
        bm = xbc[:, D_MODEL + g * SSD_STATE:D_MODEL + (g + 1) * SSD_STATE]
        cm = xbc[:, D_MODEL + (SSD_GROUPS + g) * SSD_STATE:D_MODEL + (SSD_GROUPS + g + 1) * SSD_STATE]
        cb = _dot_nt(cm, bm)
        y_state = per_seq(lambda s, rs: _dot_nt(cm[rs], ssd_i[s, grp, :]))
        xw_parts = []
        for r in range(SSD_REP):
            h = g * SSD_REP + r
            hs = slice(h * SSD_HEADDIM, (h + 1) * SSD_HEADDIM)
            cum_c = cum[:, SM_A + h:SM_A + h + 1]
            cum_r = cum_t[SM_A + h:SM_A + h + 1, :]
            dt_c = pmat[:, SM_DT + h:SM_DT + h + 1]
            dt_r = pmat_t[SM_DT + h:SM_DT + h + 1, :]
            seg = jnp.where(causal, cum_c - cum_r, -jnp.inf)
            wmat = cb * jnp.exp(seg) * dt_r
            x_h = xs[:, hs]
            y = _dot(wmat, x_h) + y_state[:, r * SSD_HEADDIM:(r + 1) * SSD_HEADDIM] * jnp.exp(cum_c)
            yacc_ref[:, hs] = y + d_row[:, h:h + 1] * x_h
            xw_parts.append(x_h * (jnp.exp(tot[:, SM_A + h:SM_A + h + 1] - cum_c) * dt_c))
            after_stage("ssd_head")
        xw = jnp.concatenate(xw_parts, axis=-1)
        for s in range(n_seq):
            upd = _dot_tn(xw[seq_rows[s]], bm[seq_rows[s]])
            t0 = s * rows if n_seq > 1 else 0
            for r in range(SSD_REP):
                h = g * SSD_REP + r
                hs = slice(h * SSD_HEADDIM, (h + 1) * SSD_HEADDIM)
                decay = jnp.exp(tot[t0:t0 + 1, SM_A + h:SM_A + h + 1])
                ssd_o[s, hs, :] = decay * ssd_i[s, hs, :] + upd[r * SSD_HEADDIM:(r + 1) * SSD_HEADDIM, :]
    yz = yacc_ref[...] * _silu(proj(OFF_Z, D_MODEL))
    y_ssd = yz * lax.rsqrt(jnp.mean(yz * yz, axis=-1, keepdims=True) + RMS_EPS) * snw_ref[...]
    after_stage("ssd")

    ym_parts = []
    for h in range(M_HEADS):
        hs = slice(h * M_HEADDIM, (h + 1) * M_HEADDIM)
        q_h = proj(OFF_Q + h * M_HEADDIM, M_HEADDIM)
        k_h = proj(OFF_K + h * M_HEADDIM, M_HEADDIM) * (M_HEADDIM ** -0.5)
        v_h = proj(OFF_V + h * M_HEADDIM, M_HEADDIM)
        o_h = proj(OFF_O + h * M_HEADDIM, M_HEADDIM)
        b_c = cum[:, SM_F + h:SM_F + h + 1]
        b_r = cum_t[SM_F + h:SM_F + h + 1, :]
        i_c = pmat[:, SM_I + h:SM_I + h + 1]
        i_r = pmat_t[SM_I + h:SM_I + h + 1, :]
        last_b = tot[:, SM_F + h:SM_F + h + 1]
        m_prev = per_seq(lambda s, rs: jnp.broadcast_to(mm_i[s, :, h:h + 1], (rows, 1)))
        dmat = jnp.where(causal, b_c - b_r + i_r, -jnp.inf)
        m_st = b_c + m_prev
        m = jnp.maximum(m_st, jnp.max(dmat, axis=-1, keepdims=True))
        wts = jnp.exp(dmat - m) * _dot_nt(q_h, k_h)
        ws = jnp.exp(m_st - m)
        cq = per_seq(lambda s, rs: _dot_nt(q_h[rs], mc_i[s, hs, :]))
        nq = per_seq(lambda s, rs: jnp.sum(q_h[rs] * mn_i[s, h:h + 1, :], axis=-1, keepdims=True))
        num = _dot(wts, v_h) + ws * cq
        den = jnp.sum(wts, axis=-1, keepdims=True) + ws * nq
        hc = num / jnp.maximum(jnp.abs(den), jnp.exp(-m))
        m_new = per_seq(lambda s, rs: jnp.broadcast_to(m[rs.stop - 1:rs.stop, :], (rows, 1)))
        wsrc = jnp.exp(last_b - b_c + i_c - m_new)
        wprev = jnp.exp(last_b + m_prev - m_new)
        vw = v_h * wsrc
        kw = k_h * wsrc
        for s in range(n_seq):
            rs = seq_rows[s]
            wp = wprev[rs.start:rs.start + 1, :]
            mc_o[s, hs, :] = wp * mc_i[s, hs, :] + _dot_tn(vw[rs], k_h[rs])
            mn_o[s, h:h + 1, :] = wp * mn_i[s, h:h + 1, :] + jnp.sum(kw[rs], axis=0, keepdims=True)
            mm_o[s, :, h:h + 1] = m_new[rs.start:rs.start + 1, :]
        mu = jnp.mean(hc, axis=-1, keepdims=True)
        hd = hc - mu
        var = jnp.mean(hd * hd, axis=-1, keepdims=True)
        ym_parts.append(hd * lax.rsqrt(var + LN_EPS) * mnw_ref[:, hs] * _sigmoid(o_h))
        after_stage("mlstm_head")
    y_m = jnp.concatenate(ym_parts, axis=-1)

    pext_ref[:, 1:POOL_BUF + 1, :] = pool_i[...]
    up = proj(OFF_UP, D_MODEL)
    pext_ref[:, POOL_BUF + 1:POOL_BUF + 1 + rows, :] = up.reshape(n_seq, rows, D_MODEL)
    pos = (lax.broadcasted_iota(jnp.int32, (L, 1), 0) & (rows - 1)) + pos0
    yp_parts = []
    for g, w in enumerate(POOL_WINDOWS):
        gs = slice(g * POOL_GW, (g + 1) * POOL_GW)
        wsum = up[:, gs]
        for j in range(1, w):
            wsum = wsum + pext_ref[:, POOL_BUF + 1 - j:POOL_BUF + 1 - j + rows, gs].reshape(L, POOL_GW)
        cnt = jnp.minimum(pos, w).astype(F32)
        dlt = wsum / cnt - up[:, gs]
        yp_parts.append(_dot(dlt, poolw_ref[g]) * pscale_ref[:, gs])
        after_stage("pool")
    pool_o[...] = pext_ref[:, 1 + tv:1 + tv + POOL_BUF, :]
    return y_ssd, y_m, jnp.concatenate(yp_parts, axis=-1)


def _merge_rows(gates, ys, wb_refs, wout_ref):
    merged = None
    for gate, y, wb_ref in zip(gates, ys, wb_refs):
        term = gate * _dot(y, wb_ref[...])
        merged = term if merged is None else merged + term
    return _dot(merged, wout_ref[...])


N_MIXER_PARAMS = 7


def _mixer_param_specs():
    return [_const_spec((SSD_CONV, SSD_CONV_DIM)), _const_spec((1, SSD_CONV_DIM)),
            _const_spec((SUBLANES, LANES)), _const_spec((1, D_MODEL)), _const_spec((1, D_MODEL)),
            _const_spec((len(POOL_WINDOWS), POOL_GW, POOL_GW)), _const_spec((1, D_MODEL))]


def _mixer_param_args(lw):
    return (lw["conv_w"], lw["conv_b"], lw["head_params"], lw["ssd_norm_w"], lw["mlstm_norm_w"],
            lw["pool_w"], lw["pool_scale"])


def _mixer_scratch(n_seq, rows):
    return [pltpu.VMEM((n_seq, SUBLANES + rows, SSD_CONV_DIM), F32),
            pltpu.VMEM((n_seq, POOL_BUF + 1 + rows, D_MODEL), F32),
            pltpu.VMEM((n_seq * rows, D_MODEL), F32)]


def _stacked_state_shapes(n_seq):
    return [jax.ShapeDtypeStruct((DEPTH, n_seq) + shp, F32) for shp in STATE_SHAPES]


def _alias_args(prev_states, n_inputs_before, n_outputs_before):
    if prev_states is None:
        return [], [], {}
    specs = [pl.BlockSpec(memory_space=pl.ANY)] * N_STATES
    aliases = {n_inputs_before + k: n_outputs_before + k for k in range(N_STATES)}
    return list(prev_states), specs, aliases


N_PROJ_ALL = N_PROJ + 3 * D_MODEL
OFF_GL = N_PROJ


def _column_pieces(lo, hi, width):
    return [(off, min(width, hi - off)) for off in range(lo, hi, width)]


def _prompt_mixer_kernel(x_ref, mod_ref, xn_ref, modn_ref, wcat_ref, wgl_ref, gb_ref, wbs_ref, wbm_ref, wbp_ref,
                         wout_ref, lng_ref, lnb_ref, *rest, n_alias):
    par = rest[:N_MIXER_PARAMS]
    rest = rest[N_MIXER_PARAMS + n_alias:]
    o_ref = rest[0]
    states = rest[1:1 + N_STATES]
    scr = rest[1 + N_STATES:4 + N_STATES]
    pscr_ref = rest[4 + N_STATES]
    L = x_ref.shape[0]
    b = pl.program_id(0)
    c = pl.program_id(1)

    def modulated(xr, mr):
        return (xr[...] * (1.0 + mr[:, D_MODEL:2 * D_MODEL]) + mr[:, 0:D_MODEL]).astype(BF16)

    def project_into_scratch(ub, piece):
        off, width = piece
        if off < OFF_GL:
            w = wcat_ref[:, off:off + width]
        else:
            w = wgl_ref[:, off - OFF_GL:off - OFF_GL + width]
        pscr_ref[:, off:off + width] = _dot(ub, w)

    @pl.when(c == 0)
    def _fresh_prompt_states():
        for ref in states:
            ref[...] = jnp.zeros(ref.shape, ref.dtype)

    @pl.when(jnp.logical_and(b == 0, c == 0))
    def _first_chunk_projections():
        ub0 = modulated(x_ref, mod_ref)
        for piece in _column_pieces(0, N_PROJ, D_MODEL) + _column_pieces(OFF_GL, N_PROJ_ALL, D_MODEL):
            project_into_scratch(ub0, piece)

    ub_next = modulated(xn_ref, modn_ref)
    ready = []
    released_by = {
        "conv": [_column_pieces(OFF_XBC, OFF_Q, 896)],
        "ssd": [_column_pieces(OFF_Z, OFF_XBC, D_MODEL)],
        "mlstm_head": [[(off + h * M_HEADDIM, M_HEADDIM) for off in (OFF_Q, OFF_K, OFF_V, OFF_O)]
                       for h in range(M_HEADS)],
        "pool": [_column_pieces(OFF_UP, N_PROJ, D_MODEL)],
    }
    pieces_per_call = {"conv": 2, "ssd_head": 1}

    def after_stage(name):
        if released_by.get(name):
            ready.extend(released_by[name].pop(0))
        for _ in range(min(len(ready), pieces_per_call.get(name, len(ready)))):
            project_into_scratch(ub_next, ready.pop(0))

    gates = [_sigmoid(pscr_ref[:, OFF_GL + i * D_MODEL:OFF_GL + (i + 1) * D_MODEL]
                      + gb_ref[:, i * D_MODEL:(i + 1) * D_MODEL]) for i in range(3)]
    ready.extend(_column_pieces(OFF_GL, N_PROJ_ALL, 768))
    proj = lambda off, width: pscr_ref[:, off:off + width]
    ys = _mixer_chunk(proj, states, states, par, scr, n_seq=1, rows=L, tv=L, pos0=c * L + 1,
                      after_stage=after_stage)
    after_stage("rest")
    assert not ready and not any(released_by.values())
    mix = _merge_rows(gates, ys, (wbs_ref, wbm_ref, wbp_ref), wout_ref)
    o_ref[...] = _layer_norm(ALPHA * x_ref[...] + (1.0 + mod_ref[:, 2 * D_MODEL:3 * D_MODEL]) * mix,
                             lng_ref[...], lnb_ref[...])


def _prompt_mixer_call(l, x3, mods, prev_states, lw):
    n_seq, rows, _ = x3.shape
    L = CHUNK
    nc = rows // L

    def next_chunk(b, c):
        flat = jnp.minimum(b * nc + c + 1, n_seq * nc - 1)
        return flat // nc, flat % nc

    row_spec = pl.BlockSpec((None, L, D_MODEL), lambda b, c: (b, c, 0))
    mod_spec = pl.BlockSpec((None, 1, 3 * D_MODEL), lambda b, c: (b, 0, 0))
    next_row_spec = pl.BlockSpec((None, L, D_MODEL), lambda b, c: next_chunk(b, c) + (0,))
    next_mod_spec = pl.BlockSpec((None, 1, 3 * D_MODEL), lambda b, c: (next_chunk(b, c)[0], 0, 0))
    vec_spec = _const_spec((1, D_MODEL))
    state_specs = [pl.BlockSpec((None, 1) + shp, lambda b, c: (l, b, 0, 0)) for shp in STATE_SHAPES]
    in_specs = [row_spec, mod_spec, next_row_spec, next_mod_spec,
                _vmem_full(), _vmem_full(), _const_spec((1, 3 * D_MODEL)),
                _vmem_full(), _vmem_full(), _vmem_full(), _vmem_full(), vec_spec, vec_spec]
    in_specs += _mixer_param_specs()
    alias_in, alias_specs, aliases = _alias_args(prev_states, len(in_specs), 1)
    outs = pl.pallas_call(
        functools.partial(_prompt_mixer_kernel, n_alias=len(alias_in)),
        grid=(n_seq, nc),
        in_specs=in_specs + alias_specs,
        out_specs=[row_spec] + state_specs,
        out_shape=[jax.ShapeDtypeStruct(x3.shape, F32)] + _stacked_state_shapes(n_seq),
        scratch_shapes=_mixer_scratch(1, L) + [pltpu.VMEM((L, N_PROJ_ALL), F32)],
        input_output_aliases=aliases,
        compiler_params=_compiler_params(2),
        name="prompt_mixers",
    )(x3, mods, x3, mods, lw["w_cat"], lw["w_gl"], lw["gate_b"], lw["w_br_ssd"], lw["w_br_mlstm"],
      lw["w_br_pool"], lw["w_out"], lw["ln1_g"], lw["ln1_b"], *_mixer_param_args(lw), *alias_in)
    return outs[0], outs[1:]


def _sample_mixer_kernel(proj_ref, *rest, tv, n_alias):
    st_in = rest[:N_STATES]
    par = rest[N_STATES:N_STATES + N_MIXER_PARAMS]
    rest = rest[N_STATES + N_MIXER_PARAMS + n_alias:]
    y_refs = rest[:3]
    st_out = rest[3:3 + N_STATES]
    scr = rest[3 + N_STATES:]
    n_blk, rows, _ = proj_ref.shape
    proj = lambda off, width: proj_ref[:, :, off:off + width].reshape(n_blk * rows, width)
    ys = _mixer_chunk(proj, st_in, st_out, par, scr, n_seq=n_blk, rows=rows, tv=tv, pos0=1 + POOL_BUF)
    for y_ref, y in zip(y_refs, ys):
        y_ref[...] = y.reshape(n_blk, rows, D_MODEL)


def _sample_mixer_call(l, proj, states_in, prev_states, lw, *, tv):
    n_seq, rows, _ = proj.shape
    nb = SAMPLE_SEQ_BLOCK
    state_specs = [pl.BlockSpec((None, nb) + shp, lambda i: (l, i, 0, 0)) for shp in STATE_SHAPES]
    y_spec = pl.BlockSpec((nb, rows, D_MODEL), lambda i: (i, 0, 0))
    y_shape = jax.ShapeDtypeStruct((n_seq, rows, D_MODEL), F32)
    in_specs = [pl.BlockSpec((nb, rows, N_PROJ), lambda i: (i, 0, 0))] + state_specs + _mixer_param_specs()
    alias_in, alias_specs, aliases = _alias_args(prev_states, len(in_specs), 3)
    outs = pl.pallas_call(
        functools.partial(_sample_mixer_kernel, tv=tv, n_alias=len(alias_in)),
        grid=(n_seq // nb,),
        in_specs=in_specs + alias_specs,
        out_specs=[y_spec, y_spec, y_spec] + state_specs,
        out_shape=[y_shape, y_shape, y_shape] + _stacked_state_shapes(n_seq),
        scratch_shapes=_mixer_scratch(nb, rows),
        input_output_aliases=aliases,
        compiler_params=_compiler_params(1),
        name="sample_mixers",
    )(proj, *states_in, *_mixer_param_args(lw), *alias_in)
    return outs[:3], outs[3:]


def _merge_kernel(x_ref, mod_ref, yssd_ref, ym_ref, ypool_ref, wgl_ref, gb_ref,
                  wbs_ref, wbm_ref, wbp_ref, wout_ref, lng_ref, lnb_ref, o_ref):
    sb, rb, _ = x_ref.shape
    n = sb * rb
    ub = _modulate(x_ref, mod_ref).astype(BF16)
    ys = [r[...].reshape(n, D_MODEL) for r in (yssd_ref, ym_ref, ypool_ref)]
    gates = [_sigmoid(_dot(ub, wgl_ref[:, i * D_MODEL:(i + 1) * D_MODEL]) + gb_ref[:, i * D_MODEL:(i + 1) * D_MODEL])
             for i in range(3)]
    mix = _merge_rows(gates, ys, (wbs_ref, wbm_ref, wbp_ref), wout_ref).reshape(sb, rb, D_MODEL)
    gate_a = mod_ref[:, :, 2 * D_MODEL:3 * D_MODEL]
    o_ref[...] = _layer_norm(ALPHA * x_ref[...] + (1.0 + gate_a) * mix, lng_ref[...], lnb_ref[...])


def _merge_call(x3, mods, ys, lw):
    n_seq, rows, _ = x3.shape
    sb, rb = _row_blocks(n_seq, rows)
    row_spec = pl.BlockSpec((sb, rb, D_MODEL), lambda i, j: (i, j, 0))
    vec_spec = _const_spec((1, D_MODEL))
    return pl.pallas_call(
        _merge_kernel,
        grid=(n_seq // sb, rows // rb),
        in_specs=[row_spec, pl.BlockSpec((sb, 1, 3 * D_MODEL), lambda i, j: (i, 0, 0)),
                  row_spec, row_spec, row_spec,
                  _vmem_full(), _const_spec((1, 3 * D_MODEL)),
                  _vmem_full(), _vmem_full(), _vmem_full(), _vmem_full(), vec_spec, vec_spec],
        out_specs=row_spec,
        out_shape=jax.ShapeDtypeStruct(x3.shape, F32),
        compiler_params=_compiler_params(2),
        name="merge_norm",
    )(x3, mods, ys[0], ys[1], ys[2], lw["w_gl"], lw["gate_b"], lw["w_br_ssd"], lw["w_br_mlstm"],
      lw["w_br_pool"], lw["w_out"], lw["ln1_g"], lw["ln1_b"])


MOE_TILE = 512
MOE_BLOCK = 128
MOE_SLOT_BLOCKS = MOE_TILE // MOE_BLOCK + N_EGROUPS - 1
MOE_SLOTS = MOE_SLOT_BLOCKS * MOE_BLOCK
SIDE_POS, SIDE_HI, SIDE_MID, SIDE_LO = 0, 8, 16, 24


def _pad_rows(v, n):
    return jnp.concatenate([v, jnp.zeros((n - v.shape[0], v.shape[1]), v.dtype)], axis=0)


def _route_t(logits_t):
    t = logits_t.shape[1]
    row_g = lax.broadcasted_iota(jnp.int32, (SUBLANES, t), 0)
    lg = jnp.where(row_g < N_EGROUPS, logits_t[RT_G:RT_G + SUBLANES, :], -jnp.inf)
    g_max = jnp.max(lg, axis=0, keepdims=True)
    g_idx = jnp.min(jnp.where(lg == g_max, row_g, SUBLANES), axis=0, keepdims=True)
    g_prob = 1.0 / jnp.sum(jnp.exp(lg - g_max), axis=0, keepdims=True)
    row_e = lax.broadcasted_iota(jnp.int32, (N_EXPERTS, t), 0)
    le = jnp.where((row_e >> 2) == g_idx, logits_t[RT_E:RT_E + N_EXPERTS, :], -jnp.inf)
    v1 = jnp.max(le, axis=0, keepdims=True)
    i1 = jnp.min(jnp.where(le == v1, row_e, N_EXPERTS), axis=0, keepdims=True)
    le2 = jnp.where(row_e == i1, -jnp.inf, le)
    v2 = jnp.max(le2, axis=0, keepdims=True)
    i2 = jnp.min(jnp.where(le2 == v2, row_e, N_EXPERTS), axis=0, keepdims=True)
    e2 = jnp.exp(v2 - v1)
    p1 = g_prob / (1.0 + e2)
    p2 = g_prob * e2 / (1.0 + e2)
    wts = jnp.where(row_e == i1, p1, 0.0) + jnp.where(row_e == i2, p2, 0.0)
    w4 = None
    for g in range(N_EGROUPS):
        part = jnp.where(g_idx == g, wts[g * EXP_PER_GROUP:(g + 1) * EXP_PER_GROUP, :], 0.0)
        w4 = part if w4 is None else w4 + part
    return g_idx, w4


def _moe_kernel(x_ref, mod_ref, wrt_ref, brt_ref, wg_ref, wu_ref, wd_ref, lng_ref, lnb_ref, o_ref,
                sx_ref, sw_ref, so_ref):
    sb, rb, _ = x_ref.shape
    t = sb * rb
    u = _modulate(x_ref, mod_ref)
    u_hi = u.astype(BF16)
    u_lo = (u - u_hi.astype(F32)).astype(BF16)
    w = wrt_ref[...]
    w_hi = w.astype(BF16)
    w_lo = (w - w_hi.astype(F32)).astype(BF16)
    logits_t = _dot_nt(w_hi, u_hi) + _dot_nt(w_hi, u_lo) + _dot_nt(w_lo, u_hi) + brt_ref[...]
    g_idx, w4 = _route_t(logits_t)

    row_g = lax.broadcasted_iota(jnp.int32, (SUBLANES, t), 0)
    onehot_t = (row_g == g_idx).astype(F32)
    before = (lax.broadcasted_iota(jnp.int32, (t, t), 0) < lax.broadcasted_iota(jnp.int32, (t, t), 1))
    rank = _dot(onehot_t, before.astype(F32))
    cnt = jnp.sum(onehot_t, axis=1, keepdims=True)
    nblk = jnp.floor((cnt + (MOE_BLOCK - 1)) * (1.0 / MOE_BLOCK))
    sub = lax.broadcasted_iota(jnp.int32, (SUBLANES, 1), 0)
    first = jnp.zeros((SUBLANES, 1), F32)
    running = jnp.zeros((1, 1), F32)
    for g in range(1, N_EGROUPS):
        running = running + nblk[g - 1:g, :]
        first = first + jnp.where(sub == g, running, 0.0)
    pos_t = jnp.sum(onehot_t * (first * MOE_BLOCK + rank), axis=0, keepdims=True)
    blk_lane = lax.broadcasted_iota(jnp.int32, (SUBLANES, LANES), 1).astype(F32)
    in_blk = (blk_lane >= first) & (blk_lane < first + nblk)
    blk_group = jnp.sum(jnp.where(in_blk, sub.astype(F32), 0.0), axis=0, keepdims=True)

    w_hi4, w_mid4, w_lo4 = [p.astype(F32) for p in _split3(w4)]
    side_t = jnp.concatenate([_pad_rows(pos_t, SUBLANES), _pad_rows(w_hi4, SUBLANES),
                              _pad_rows(w_mid4, SUBLANES), _pad_rows(w_lo4, SUBLANES),
                              jnp.zeros((LANES - 4 * SUBLANES, t), F32)], axis=0)
    side = side_t.T

    slot_r = lax.broadcasted_iota(jnp.int32, (MOE_SLOTS, t), 0)
    send = (slot_r == pos_t.astype(jnp.int32)).astype(BF16)
    payload = jnp.concatenate([u_hi, side.astype(BF16)], axis=-1)
    sorted_rows = _dot(send, payload)
    sx_ref[...] = sorted_rows[:, 0:D_MODEL].astype(BF16)
    sw_ref[...] = sorted_rows[:, D_MODEL:D_MODEL + LANES]

    for i in range(MOE_SLOT_BLOCKS):
        g = blk_group[0, i].astype(jnp.int32)
        rows = slice(i * MOE_BLOCK, (i + 1) * MOE_BLOCK)
        xb = sx_ref[rows, :]
        ws = sw_ref[rows, :]
        w_blk = (ws[:, SIDE_HI:SIDE_HI + EXP_PER_GROUP] + ws[:, SIDE_MID:SIDE_MID + EXP_PER_GROUP]
                 + ws[:, SIDE_LO:SIDE_LO + EXP_PER_GROUP])
        hid = []
        for r in range(EXP_PER_GROUP):
            e = g * EXP_PER_GROUP + r
            hid.append(_silu(_dot(xb, wg_ref[e])) * _dot(xb, wu_ref[e]) * w_blk[:, r:r + 1])
        so_ref[rows, :] = _dot(jnp.concatenate(hid, axis=-1), wd_ref[g]).astype(BF16)

    slot_c = lax.broadcasted_iota(jnp.int32, (t, MOE_SLOTS), 1)
    fetch = (slot_c == side[:, SIDE_POS:SIDE_POS + 1].astype(jnp.int32)).astype(BF16)
    ffn = _dot(fetch, so_ref[...])
    gate_f = mod_ref[:, :, 2 * D_MODEL:3 * D_MODEL]
    o_ref[...] = _layer_norm(ALPHA * x_ref[...] + (1.0 + gate_f) * ffn.reshape(sb, rb, D_MODEL),
                             lng_ref[...], lnb_ref[...])


def _moe_call(x3, mods, lw):
    n_seq, rows, _ = x3.shape
    sb, rb = _row_blocks(n_seq, rows, MOE_TILE)
    row_spec = pl.BlockSpec((sb, rb, D_MODEL), lambda i, j: (i, j, 0))
    vec_spec = _const_spec((1, D_MODEL))
    return pl.pallas_call(
        _moe_kernel,
        grid=(n_seq // sb, rows // rb),
        in_specs=[row_spec, pl.BlockSpec((sb, 1, 3 * D_MODEL), lambda i, j: (i, 0, 0)),
                  _vmem_full(), _const_spec((LANES, 1)),
                  _vmem_full(), _vmem_full(), _vmem_full(), vec_spec, vec_spec],
        out_specs=row_spec,
        out_shape=jax.ShapeDtypeStruct(x3.shape, F32),
        scratch_shapes=[pltpu.VMEM((MOE_SLOTS, D_MODEL), BF16), pltpu.VMEM((MOE_SLOTS, LANES), F32),
                        pltpu.VMEM((MOE_SLOTS, D_MODEL), BF16)],
        compiler_params=_compiler_params(2),
        name="moe_norm",
    )(x3, mods, lw["w_rt_t"], lw["b_rt_col"], lw["w_e_gate"], lw["w_e_up"], lw["w_e_down"],
      lw["ln2_g"], lw["ln2_b"])


def _split_w_in(w):
    sizes = (D_MODEL, SSD_CONV_DIM, SSD_HEADS, D_MODEL, D_MODEL, D_MODEL, M_HEADS, M_HEADS, D_MODEL,
             D_MODEL, 3 * D_MODEL)
    out, off = [], 0
    for s in sizes:
        out.append(w[:, off:off + s])
        off += s
    return out


def _pad_lanes(v, width=LANES):
    return jnp.pad(v, ((0, 0), (0, width - v.shape[-1])))


def _layer_weights(l, p):
    wz, wxbc, wdt, wq, wk, wv, wi, wf, wo, wup, wgl = _split_w_in(p["w_in"][l])
    w_small = _pad_lanes(jnp.concatenate([wdt, wdt, wf, wi], axis=1))
    gate_b = p["mlstm_gate_b"][l]
    bias_row = jnp.concatenate([p["ssd_dt_bias"][l], p["ssd_dt_bias"][l], gate_b[M_HEADS:], gate_b[:M_HEADS]])
    head_params = jnp.concatenate([
        _pad_lanes(bias_row[None]), _pad_lanes(p["ssd_A_log"][l][None]), _pad_lanes(p["ssd_D"][l][None]),
        jnp.zeros((SUBLANES - 3, LANES), F32)], axis=0)
    w_rt = jnp.concatenate([_pad_lanes(p["w_rt_group"][l], RT_E), _pad_lanes(p["w_rt_expert"][l], LANES - RT_E)],
                           axis=1)
    b_rt = jnp.concatenate([_pad_lanes(p["b_rt_group"][l][None], RT_E),
                            _pad_lanes(p["b_rt_expert"][l][None], LANES - RT_E)], axis=1)
    row = lambda v: v[None]
    return dict(
        w_cat=jnp.concatenate([wz, wxbc, w_small, wq, wk, wv, wo, wup], axis=1).astype(BF16),
        w_gl=wgl.astype(BF16),
        conv_w=p["conv_w"][l], conv_b=row(p["conv_b"][l]), head_params=head_params,
        ssd_norm_w=row(p["ssd_norm_w"][l]), mlstm_norm_w=row(p["mlstm_norm_w"][l]),
        pool_w=p["pool_w"][l].astype(BF16), pool_scale=row(p["pool_scale"][l]),
        gate_b=row(p["gate_b"][l]),
        w_br_ssd=p["w_br_ssd"][l].astype(BF16), w_br_mlstm=p["w_br_mlstm"][l].astype(BF16),
        w_br_pool=p["w_br_pool"][l].astype(BF16), w_out=p["w_out"][l].astype(BF16),
        ln1_g=row(p["ln1_g"][l]), ln1_b=row(p["ln1_b"][l]),
        w_rt_t=w_rt.T, b_rt_col=b_rt.T,
        w_e_gate=p["w_e_gate"][l].astype(BF16), w_e_up=p["w_e_up"][l].astype(BF16),
        w_e_down=p["w_e_down"][l].reshape(N_EGROUPS, EXP_PER_GROUP * D_FF_E, D_MODEL).astype(BF16),
        ln2_g=row(p["ln2_g"][l]), ln2_b=row(p["ln2_b"][l]),
    )


def _flat_states(ssd, conv, mc, mn, mm, pool):
    return tuple(a.reshape(a.shape[:2] + shp) for a, shp in zip((ssd, conv, mc, mn, mm, pool), STATE_SHAPES))


def _unflat_states(states):
    ssd, conv, mc, mn, mm, pool = states
    d, n = ssd.shape[:2]
    return (ssd.reshape(d, n, SSD_HEADS, SSD_HEADDIM, SSD_STATE), conv,
            mc.reshape(d, n, M_HEADS, M_HEADDIM, M_HEADDIM), mn, mm.reshape(d, n, M_HEADS), pool)


def kernel(x_prompt, x_sample, state_ssd, state_conv, state_mlstm_C, state_mlstm_n, state_mlstm_m, state_pool, c_prompt, c_sample, w_ada_mix, b_ada_mix, w_in, conv_w, conv_b, ssd_A_log, ssd_dt_bias, ssd_D, ssd_norm_w, mlstm_gate_b, mlstm_norm_w, pool_w, pool_scale, gate_b, w_br_ssd, w_br_mlstm, w_br_pool, w_out, ln1_g, ln1_b, w_ada_ffn, b_ada_ffn, w_rt_group, b_rt_group, w_rt_expert, b_rt_expert, w_e_gate, w_e_up, w_e_down, ln2_g, ln2_b):
    params = dict(w_in=w_in, conv_w=conv_w, conv_b=conv_b, ssd_A_log=ssd_A_log, ssd_dt_bias=ssd_dt_bias,
                  ssd_D=ssd_D, ssd_norm_w=ssd_norm_w, mlstm_gate_b=mlstm_gate_b, mlstm_norm_w=mlstm_norm_w,
                  pool_w=pool_w, pool_scale=pool_scale, gate_b=gate_b, w_br_ssd=w_br_ssd,
                  w_br_mlstm=w_br_mlstm, w_br_pool=w_br_pool, w_out=w_out, ln1_g=ln1_g, ln1_b=ln1_b,
                  w_rt_group=w_rt_group, b_rt_group=b_rt_group, w_rt_expert=w_rt_expert,
                  b_rt_expert=b_rt_expert, w_e_gate=w_e_gate, w_e_up=w_e_up, w_e_down=w_e_down,
                  ln2_g=ln2_g, ln2_b=ln2_b)
    bp, seq, _ = x_prompt.shape
    bs, dec_seq, _ = x_sample.shape
    assert seq % CHUNK == 0 and 1 <= dec_seq <= SAMPLE_ROWS and bs % SAMPLE_SEQ_BLOCK == 0

    c_all = jnp.concatenate([c_prompt, c_sample], axis=0)
    mods_mix = _ada_call(c_all, w_ada_mix, b_ada_mix[:, None, :])
    mods_ffn = _ada_call(c_all, w_ada_ffn, b_ada_ffn[:, None, :])

    sample_in = _flat_states(state_ssd, state_conv, state_mlstm_C, state_mlstm_n, state_mlstm_m, state_pool)
    xp = x_prompt
    xs = jnp.pad(x_sample, ((0, 0), (0, SAMPLE_ROWS - dec_seq), (0, 0)))
    p_states, s_states = None, None
    for l in range(DEPTH):
        lw = _layer_weights(l, params)
        xp, p_states = _prompt_mixer_call(l, xp, mods_mix[l, :bp, None, :], p_states, lw)
        xp = _moe_call(xp, mods_ffn[l, :bp, None, :], lw)

        mods_s = mods_mix[l, bp:, None, :]
        proj = _proj_call(xs, mods_s, lw["w_cat"])
        ys, s_states = _sample_mixer_call(l, proj, sample_in, s_states, lw, tv=dec_seq)
        xs = _merge_call(xs, mods_s, ys, lw)
        xs = _moe_call(xs, mods_ffn[l, bp:, None, :], lw)
    return (xp, xs[:, :dec_seq]) + _unflat_states(p_states) + _unflat_states(s_states)
```

```python
import functools

import jax
import jax.numpy as jnp
from jax import lax
from jax.experimental import pallas as pl
from jax.experimental.pallas import tpu as pltpu

F32 = jnp.float32
BF16 = jnp.bfloat16

D_MODEL = 1024
DEPTH = 4
SSD_HEADS = 16
SSD_HEADDIM = 64
SSD_GROUPS = 2
SSD_REP = SSD_HEADS // SSD_GROUPS
SSD_STATE = 128
SSD_CONV = 4
SSD_CONV_DIM = D_MODEL + 2 * SSD_GROUPS * SSD_STATE
CHUNK = 128
M_HEADS = 4
M_HEADDIM = D_MODEL // M_HEADS
POOL_WINDOWS = (2, 4, 8, 16)
POOL_GW = D_MODEL // len(POOL_WINDOWS)
POOL_BUF = max(POOL_WINDOWS) - 1
N_EGROUPS = 4
EXP_PER_GROUP = 4
N_EXPERTS = N_EGROUPS * EXP_PER_GROUP
D_FF_E = D_MODEL // 4
ALPHA = (2 * DEPTH) ** 0.25
LN_EPS = 1e-5
RMS_EPS = 1e-6

SUBLANES = 8
LANES = 128
VMEM_LIMIT_BYTES = 56 * 1024 * 1024

OFF_Z = 0
OFF_XBC = OFF_Z + D_MODEL
OFF_SMALL = OFF_XBC + SSD_CONV_DIM
OFF_Q = OFF_SMALL + LANES
OFF_K = OFF_Q + D_MODEL
OFF_V = OFF_K + D_MODEL
OFF_O = OFF_V + D_MODEL
OFF_UP = OFF_O + D_MODEL
N_PROJ = OFF_UP + D_MODEL
SM_A, SM_DT, SM_F, SM_I, SM_END = 0, SSD_HEADS, 2 * SSD_HEADS, 2 * SSD_HEADS + M_HEADS, 2 * SSD_HEADS + 2 * M_HEADS
RT_G, RT_E = 0, 16
NEG_BIG = -1e30

ROW_TILE = 256
SAMPLE_ROWS = 8
SAMPLE_SEQ_BLOCK = 4

STATE_SHAPES = ((SSD_HEADS * SSD_HEADDIM, SSD_STATE), (SSD_CONV - 1, SSD_CONV_DIM),
                (M_HEADS * M_HEADDIM, M_HEADDIM), (M_HEADS, M_HEADDIM), (1, M_HEADS), (POOL_BUF, D_MODEL))
N_STATES = len(STATE_SHAPES)
N_MIXER_SCRATCH = 5
POOL_TOP = SUBLANES + POOL_BUF + 1


def _dot(a, b):
    return jnp.dot(a.astype(BF16), b.astype(BF16), preferred_element_type=F32)


def _dot_nt(a, b):
    return lax.dot_general(a.astype(BF16), b.astype(BF16), (((1,), (1,)), ((), ())),
                           preferred_element_type=F32)


def _dot_tn(a, b):
    return lax.dot_general(a.astype(BF16), b.astype(BF16), (((0,), (0,)), ((), ())),
                           preferred_element_type=F32)


def _split3(v):
    hi = v.astype(BF16)
    r1 = v - hi.astype(F32)
    mid = r1.astype(BF16)
    lo = (r1 - mid.astype(F32)).astype(BF16)
    return hi, mid, lo


def _select_dot(sel, v):
    n = v.shape[1]
    out = jnp.dot(sel.astype(BF16), jnp.concatenate(_split3(v), axis=1), preferred_element_type=F32)
    return out[:, 0:n] + out[:, n:2 * n] + out[:, 2 * n:3 * n]


def _select_dot_nt(sel, v):
    n = v.shape[0]
    out = lax.dot_general(sel.astype(BF16), jnp.concatenate(_split3(v), axis=0), (((1,), (1,)), ((), ())),
                          preferred_element_type=F32)
    return out[:, 0:n] + out[:, n:2 * n] + out[:, 2 * n:3 * n]


def _sigmoid(x):
    return 1.0 / (1.0 + jnp.exp(-x))


def _silu(x):
    return x * _sigmoid(x)


def _softplus(x):
    return jnp.maximum(x, 0.0) + jnp.log1p(jnp.exp(-jnp.abs(x)))


def _layer_norm(x, g, b):
    mu = jnp.mean(x, axis=-1, keepdims=True)
    xc = x - mu
    var = jnp.mean(xc * xc, axis=-1, keepdims=True)
    return xc * lax.rsqrt(var + LN_EPS) * g + b


def _compiler_params(n_grid):
    return pltpu.CompilerParams(dimension_semantics=("arbitrary",) * n_grid,
                                vmem_limit_bytes=VMEM_LIMIT_BYTES)


def _vmem_full():
    return pl.BlockSpec(memory_space=pltpu.VMEM)


def _const_spec(shape):
    return pl.BlockSpec(shape, lambda *_: (0,) * len(shape))


def _ada_kernel(c_ref, w_ref, b_ref, o_ref):
    o_ref[...] = _dot(c_ref[...], w_ref[...]) + b_ref[...]


def _ada_call(c_all, w, b):
    n = c_all.shape[0]
    return pl.pallas_call(
        _ada_kernel,
        grid=(DEPTH, 3),
        in_specs=[pl.BlockSpec((n, D_MODEL), lambda l, j: (0, 0)),
                  pl.BlockSpec((None, D_MODEL, D_MODEL), lambda l, j: (l, 0, j)),
                  pl.BlockSpec((None, 1, D_MODEL), lambda l, j: (l, 0, j))],
        out_specs=pl.BlockSpec((None, n, D_MODEL), lambda l, j: (l, 0, j)),
        out_shape=jax.ShapeDtypeStruct((DEPTH, n, 3 * D_MODEL), F32),
        compiler_params=_compiler_params(2),
        name="ada_mod",
    )(c_all, w, b)


def _modulate(x_ref, mod_ref):
    x = x_ref[...]
    shift = mod_ref[:, :, 0:D_MODEL]
    scale = mod_ref[:, :, D_MODEL:2 * D_MODEL]
    u = x * (1.0 + scale) + shift
    return u.reshape(x.shape[0] * x.shape[1], D_MODEL)


def _proj_kernel(x_ref, mod_ref, w_ref, o_ref):
    sb, rb, _ = x_ref.shape
    u = _modulate(x_ref, mod_ref).astype(BF16)
    col = 0
    while col < N_PROJ:
        width = min(D_MODEL, N_PROJ - col)
        o_ref[:, :, col:col + width] = _dot(u, w_ref[:, col:col + width]).reshape(sb, rb, width)
        col += width


def _row_blocks(n_seq, rows, tile=ROW_TILE):
    if rows >= tile:
        return 1, tile
    return tile // rows, rows


def _proj_call(x3, mods, wcat):
    n_seq, rows, _ = x3.shape
    sb, rb = _row_blocks(n_seq, rows)
    return pl.pallas_call(
        _proj_kernel,
        grid=(n_seq // sb, rows // rb),
        in_specs=[pl.BlockSpec((sb, rb, D_MODEL), lambda i, j: (i, j, 0)),
                  pl.BlockSpec((sb, 1, 3 * D_MODEL), lambda i, j: (i, 0, 0)),
                  _vmem_full()],
        out_specs=pl.BlockSpec((sb, rb, N_PROJ), lambda i, j: (i, j, 0)),
        out_shape=jax.ShapeDtypeStruct((n_seq, rows, N_PROJ), F32),
        compiler_params=_compiler_params(2),
        name="in_proj",
    )(x3, mods, wcat)


def _mixer_chunk(proj, st_in, st_out, par, scr, *, n_seq, rows, tv, pos0, after_stage=lambda name: None):
    L = n_seq * rows
    rows_log2 = rows.bit_length() - 1
    assert rows == 1 << rows_log2
    ssd_i, conv_i, mc_i, mn_i, mm_i, pool_i = st_in
    ssd_o, conv_o, mc_o, mn_o, mm_o, pool_o = st_out
    convw_ref, convb_ref, hp_ref, snw_ref, mnw_ref, poolw_ref, pscale_ref = par
    xext_ref, pext_ref, psa_ref, psb_ref, yacc_ref = scr
    seq_rows = [slice(s * rows, (s + 1) * rows) for s in range(n_seq)]

    def per_seq(fn):
        parts = [fn(s, seq_rows[s]) for s in range(n_seq)]
        return parts[0] if n_seq == 1 else jnp.concatenate(parts, axis=0)

    row_l = lax.broadcasted_iota(jnp.int32, (L, L), 0)
    col_l = lax.broadcasted_iota(jnp.int32, (L, L), 1)
    causal = row_l >= col_l
    if n_seq > 1:
        same_seq = (row_l >> rows_log2) == (col_l >> rows_log2)
        causal = causal & same_seq

    lane = lax.broadcasted_iota(jnp.int32, (L, LANES), 1)
    pre = proj(OFF_SMALL, LANES) + hp_ref[0:1, :]
    sp = _softplus(pre)
    a_row = -jnp.exp(hp_ref[1:2, :])
    pmat = jnp.where(lane < SM_DT, sp * a_row,
                     jnp.where(lane < SM_F, sp,
                               jnp.where(lane < SM_I, -_softplus(-pre),
                                         jnp.where(lane < SM_END, pre, 0.0))))
    if tv < rows:
        row = lax.broadcasted_iota(jnp.int32, (L, LANES), 0)
        pad = jnp.where(lane < SM_I, 0.0, jnp.where(lane < SM_END, NEG_BIG, 0.0))
        pmat = jnp.where((row & (rows - 1)) < tv, pmat, pad)
    eye = (lax.broadcasted_iota(jnp.int32, (LANES, LANES), 0)
           == lax.broadcasted_iota(jnp.int32, (LANES, LANES), 1))
    cum = _select_dot(causal, pmat)
    pmat_t = _select_dot_nt(eye, pmat)
    cum_t = _select_dot_nt(eye, cum)
    if n_seq > 1:
        tot = _select_dot(same_seq, pmat)
    else:
        tot = cum[L - 1:L, :]

    xext_ref[:, SUBLANES - (SSD_CONV - 1):SUBLANES, :] = conv_i[...]
    xext_ref[:, SUBLANES:SUBLANES + rows, :] = proj(OFF_XBC, SSD_CONV_DIM).reshape(n_seq, rows, SSD_CONV_DIM)
    acc = convb_ref[...]
    for k in range(SSD_CONV):
        start = SUBLANES - (SSD_CONV - 1) + k
        acc = acc + xext_ref[:, start:start + rows, :].reshape(L, SSD_CONV_DIM) * convw_ref[k:k + 1, :]
    conv_o[...] = xext_ref[:, SUBLANES + tv - (SSD_CONV - 1):SUBLANES + tv, :]
    after_stage("conv")
    xbc = _silu(acc)
    xs = xbc[:, 0:D_MODEL]
    d_row = hp_ref[2:3, :]
    gw = SSD_REP * SSD_HEADDIM
    for g in range(SSD_GROUPS):
        grp = slice(g * gw, (g + 1) * gw)
        bm = xbc[:, D_MODEL + g * SSD_STATE:D_MODEL + (g + 1) * SSD_STATE]
        cm = xbc[:, D_MODEL + (SSD_GROUPS + g) * SSD_STATE:D_MODEL + (SSD_GROUPS + g + 1) * SSD_STATE]
        cb = _dot_nt(cm, bm)
        y_state = per_seq(lambda s, rs: _dot_nt(cm[rs], ssd_i[s, grp, :]))
        xw_parts = []
        for r in range(SSD_REP):
            h = g * SSD_REP + r
            hs = slice(h * SSD_HEADDIM, (h + 1) * SSD_HEADDIM)
            cum_c = cum[:, SM_A + h:SM_A + h + 1]
            cum_r = cum_t[SM_A + h:SM_A + h + 1, :]
            dt_c = pmat[:, SM_DT + h:SM_DT + h + 1]
            dt_r = pmat_t[SM_DT + h:SM_DT + h + 1, :]
            seg = jnp.where(causal, cum_c - cum_r, -jnp.inf)
            wmat = cb * jnp.exp(seg) * dt_r
            x_h = xs[:, hs]
            y = _dot(wmat, x_h) + y_state[:, r * SSD_HEADDIM:(r + 1) * SSD_HEADDIM] * jnp.exp(cum_c)
            yacc_ref[:, hs] = y + d_row[:, h:h + 1] * x_h
            xw_parts.append(x_h * (jnp.exp(tot[:, SM_A + h:SM_A + h + 1] - cum_c) * dt_c))
            after_stage("ssd_head")
        xw = jnp.concatenate(xw_parts, axis=-1)
        for s in range(n_seq):
            upd = _dot_tn(xw[seq_rows[s]], bm[seq_rows[s]])
            t0 = s * rows if n_seq > 1 else 0
            for r in range(SSD_REP):
                h = g * SSD_REP + r
                hs = slice(h * SSD_HEADDIM, (h + 1) * SSD_HEADDIM)
                decay = jnp.exp(tot[t0:t0 + 1, SM_A + h:SM_A + h + 1])
                ssd_o[s, hs, :] = decay * ssd_i[s, hs, :] + upd[r * SSD_HEADDIM:(r + 1) * SSD_HEADDIM, :]
    yz = yacc_ref[...] * _silu(proj(OFF_Z, D_MODEL))
    y_ssd = yz * lax.rsqrt(jnp.mean(yz * yz, axis=-1, keepdims=True) + RMS_EPS) * snw_ref[...]
    after_stage("ssd")

    ym_parts = []
    for h in range(M_HEADS):
        hs = slice(h * M_HEADDIM, (h + 1) * M_HEADDIM)
        q_h = proj(OFF_Q + h * M_HEADDIM, M_HEADDIM)
        k_h = proj(OFF_K + h * M_HEADDIM, M_HEADDIM) * (M_HEADDIM ** -0.5)
        v_h = proj(OFF_V + h * M_HEADDIM, M_HEADDIM)
        o_h = proj(OFF_O + h * M_HEADDIM, M_HEADDIM)
        b_c = cum[:, SM_F + h:SM_F + h + 1]
        b_r = cum_t[SM_F + h:SM_F + h + 1, :]
        i_c = pmat[:, SM_I + h:SM_I + h + 1]
        i_r = pmat_t[SM_I + h:SM_I + h + 1, :]
        last_b = tot[:, SM_F + h:SM_F + h + 1]
        m_prev = per_seq(lambda s, rs: jnp.broadcast_to(mm_i[s, :, h:h + 1], (rows, 1)))
        dmat = jnp.where(causal, b_c - b_r + i_r, -jnp.inf)
        m_st = b_c + m_prev
        m = jnp.maximum(m_st, jnp.max(dmat, axis=-1, keepdims=True))
        wts = jnp.exp(dmat - m) * _dot_nt(q_h, k_h)
        ws = jnp.exp(m_st - m)
        cq = per_seq(lambda s, rs: _dot_nt(q_h[rs], mc_i[s, hs, :]))
        nq = per_seq(lambda s, rs: jnp.sum(q_h[rs] * mn_i[s, h:h + 1, :], axis=-1, keepdims=True))
        num = _dot(wts, v_h) + ws * cq
        den = jnp.sum(wts, axis=-1, keepdims=True) + ws * nq
        hc = num / jnp.maximum(jnp.abs(den), jnp.exp(-m))
        m_new = per_seq(lambda s, rs: jnp.broadcast_to(m[rs.stop - 1:rs.stop, :], (rows, 1)))
        wsrc = jnp.exp(last_b - b_c + i_c - m_new)
        wprev = jnp.exp(last_b + m_prev - m_new)
        vw = v_h * wsrc
        kw = k_h * wsrc
        for s in range(n_seq):
            rs = seq_rows[s]
            wp = wprev[rs.start:rs.start + 1, :]
            mc_o[s, hs, :] = wp * mc_i[s, hs, :] + _dot_tn(vw[rs], k_h[rs])
            mn_o[s, h:h + 1, :] = wp * mn_i[s, h:h + 1, :] + jnp.sum(kw[rs], axis=0, keepdims=True)
            mm_o[s, :, h:h + 1] = m_new[rs.start:rs.start + 1, :]
        mu = jnp.mean(hc, axis=-1, keepdims=True)
        hd = hc - mu
        var = jnp.mean(hd * hd, axis=-1, keepdims=True)
        ym_parts.append(hd * lax.rsqrt(var + LN_EPS) * mnw_ref[:, hs] * _sigmoid(o_h))
        after_stage("mlstm_head")
    y_m = jnp.concatenate(ym_parts, axis=-1)

    assert POOL_WINDOWS == tuple(2 << g for g in range(len(POOL_WINDOWS)))
    first_row = POOL_TOP - POOL_BUF
    pext_ref[:, 0:first_row, :] = jnp.zeros((n_seq, first_row, D_MODEL), F32)
    for ref in (psa_ref, psb_ref):
        ref[:, 0:SUBLANES, :] = jnp.zeros((n_seq, SUBLANES, D_MODEL), F32)
    pext_ref[:, first_row:POOL_TOP, :] = pool_i[...]
    up = proj(OFF_UP, D_MODEL)
    pext_ref[:, POOL_TOP:POOL_TOP + rows, :] = up.reshape(n_seq, rows, D_MODEL)
    span = POOL_TOP - SUBLANES + rows
    src, dst, shift = pext_ref, psa_ref, 1
    for g in range(len(POOL_WINDOWS) - 1):
        lanes = slice(g * POOL_GW, D_MODEL)
        dst[:, SUBLANES:SUBLANES + span, lanes] = (src[:, SUBLANES:SUBLANES + span, lanes]
                                                   + src[:, SUBLANES - shift:SUBLANES - shift + span, lanes])
        src, dst, shift = dst, (psb_ref if dst is psa_ref else psa_ref), 2 * shift
    last = slice((len(POOL_WINDOWS) - 1) * POOL_GW, D_MODEL)
    widest = src[:, POOL_TOP:POOL_TOP + rows, last] + src[:, POOL_TOP - shift:POOL_TOP - shift + rows, last]
    pos = (lax.broadcasted_iota(jnp.int32, (L, 1), 0) & (rows - 1)) + pos0
    yp_parts = []
    for g, w in enumerate(POOL_WINDOWS):
        gs = slice(g * POOL_GW, (g + 1) * POOL_GW)
        if g == len(POOL_WINDOWS) - 1:
            wsum = widest.reshape(L, POOL_GW)
        else:
            wsum = (psa_ref if g % 2 == 0 else psb_ref)[:, POOL_TOP:POOL_TOP + rows, gs].reshape(L, POOL_GW)
        cnt = jnp.minimum(pos, w).astype(F32)
        dlt = wsum / cnt - up[:, gs]
        yp_parts.append(_dot(dlt, poolw_ref[g]) * pscale_ref[:, gs])
        after_stage("pool")
    pool_o[...] = pext_ref[:, first_row + tv:POOL_TOP + tv, :]
    return y_ssd, y_m, jnp.concatenate(yp_parts, axis=-1)


def _merge_rows(gates, ys, w_ref):
    merged = None
    for i, (gate, y) in enumerate(zip(gates, ys)):
        term = gate * _dot(y, w_ref[:, OFF_BR + i * D_MODEL:OFF_BR + (i + 1) * D_MODEL])
        merged = term if merged is None else merged + term
    return _dot(merged, w_ref[:, OFF_OUT:OFF_OUT + D_MODEL])


N_MIXER_PARAMS = 7


def _mixer_param_specs():
    return [_const_spec((SSD_CONV, SSD_CONV_DIM)), _const_spec((1, SSD_CONV_DIM)),
            _const_spec((SUBLANES, LANES)), _const_spec((1, D_MODEL)), _const_spec((1, D_MODEL)),
            _const_spec((len(POOL_WINDOWS), POOL_GW, POOL_GW)), _const_spec((1, D_MODEL))]


def _mixer_param_args(lw):
    return (lw["conv_w"], lw["conv_b"], lw["head_params"], lw["ssd_norm_w"], lw["mlstm_norm_w"],
            lw["pool_w"], lw["pool_scale"])


def _mixer_scratch(n_seq, rows):
    pool_rows = pltpu.VMEM((n_seq, POOL_TOP + rows, D_MODEL), F32)
    return [pltpu.VMEM((n_seq, SUBLANES + rows, SSD_CONV_DIM), F32), pool_rows, pool_rows, pool_rows,
            pltpu.VMEM((n_seq * rows, D_MODEL), F32)]


def _stacked_state_shapes(n_seq):
    return [jax.ShapeDtypeStruct((DEPTH, n_seq) + shp, F32) for shp in STATE_SHAPES]


def _alias_args(prev_states, n_inputs_before, n_outputs_before):
    if prev_states is None:
        return [], [], {}
    specs = [pl.BlockSpec(memory_space=pl.ANY)] * N_STATES
    aliases = {n_inputs_before + k: n_outputs_before + k for k in range(N_STATES)}
    return list(prev_states), specs, aliases


N_PROJ_ALL = N_PROJ + 3 * D_MODEL
OFF_GL = N_PROJ
OFF_BR = N_PROJ_ALL
OFF_OUT = OFF_BR + 3 * D_MODEL


def _column_pieces(lo, hi, width):
    return [(off, min(width, hi - off)) for off in range(lo, hi, width)]


def _prompt_mixer_kernel(x_ref, mod_ref, xn_ref, modn_ref, wcat_ref, gb_ref, lng_ref, lnb_ref, *rest, n_alias):
    par = rest[:N_MIXER_PARAMS]
    rest = rest[N_MIXER_PARAMS + n_alias:]
    o_ref = rest[0]
    states = rest[1:1 + N_STATES]
    scr = rest[1 + N_STATES:1 + N_STATES + N_MIXER_SCRATCH]
    pscr_ref = rest[1 + N_STATES + N_MIXER_SCRATCH]
    L = x_ref.shape[0]
    b = pl.program_id(0)
    c = pl.program_id(1)

    def modulated(xr, mr):
        return (xr[...] * (1.0 + mr[:, D_MODEL:2 * D_MODEL]) + mr[:, 0:D_MODEL]).astype(BF16)

    def project_into_scratch(ub, piece):
        off, width = piece
        pscr_ref[:, off:off + width] = _dot(ub, wcat_ref[:, off:off + width])

    @pl.when(c == 0)
    def _fresh_prompt_states():
        for ref in states:
            ref[...] = jnp.zeros(ref.shape, ref.dtype)

    @pl.when(jnp.logical_and(b == 0, c == 0))
    def _first_chunk_projections():
        ub0 = modulated(x_ref, mod_ref)
        for piece in _column_pieces(0, N_PROJ, D_MODEL) + _column_pieces(OFF_GL, N_PROJ_ALL, D_MODEL):
            project_into_scratch(ub0, piece)

    ub_next = modulated(xn_ref, modn_ref)
    ready = []
    released_by = {
        "conv": [_column_pieces(OFF_XBC, OFF_Q, 896)],
        "ssd": [_column_pieces(OFF_Z, OFF_XBC, D_MODEL)],
        "mlstm_head": [[(off + h * M_HEADDIM, M_HEADDIM) for off in (OFF_Q, OFF_K, OFF_V, OFF_O)]
                       for h in range(M_HEADS)],
        "pool": [_column_pieces(OFF_UP, N_PROJ, D_MODEL)],
    }
    pieces_per_call = {"conv": 2, "ssd_head": 1}

    def after_stage(name):
        if released_by.get(name):
            ready.extend(released_by[name].pop(0))
        for _ in range(min(len(ready), pieces_per_call.get(name, len(ready)))):
            project_into_scratch(ub_next, ready.pop(0))

    gates = [_sigmoid(pscr_ref[:, OFF_GL + i * D_MODEL:OFF_GL + (i + 1) * D_MODEL]
                      + gb_ref[:, i * D_MODEL:(i + 1) * D_MODEL]) for i in range(3)]
    ready.extend(_column_pieces(OFF_GL, N_PROJ_ALL, 768))
    proj = lambda off, width: pscr_ref[:, off:off + width]
    ys = _mixer_chunk(proj, states, states, par, scr, n_seq=1, rows=L, tv=L, pos0=c * L + 1,
                      after_stage=after_stage)
    after_stage("rest")
    assert not ready and not any(released_by.values())
    mix = _merge_rows(gates, ys, wcat_ref)
    o_ref[...] = _layer_norm(ALPHA * x_ref[...] + (1.0 + mod_ref[:, 2 * D_MODEL:3 * D_MODEL]) * mix,
                             lng_ref[...], lnb_ref[...])


def _prompt_mixer_call(l, x3, mods, prev_states, lw):
    n_seq, rows, _ = x3.shape
    L = CHUNK
    nc = rows // L

    def next_chunk(b, c):
        flat = jnp.minimum(b * nc + c + 1, n_seq * nc - 1)
        return flat // nc, flat % nc

    row_spec = pl.BlockSpec((None, L, D_MODEL), lambda b, c: (b, c, 0))
    mod_spec = pl.BlockSpec((None, 1, 3 * D_MODEL), lambda b, c: (b, 0, 0))
    next_row_spec = pl.BlockSpec((None, L, D_MODEL), lambda b, c: next_chunk(b, c) + (0,))
    next_mod_spec = pl.BlockSpec((None, 1, 3 * D_MODEL), lambda b, c: (next_chunk(b, c)[0], 0, 0))
    vec_spec = _const_spec((1, D_MODEL))
    state_specs = [pl.BlockSpec((None, 1) + shp, lambda b, c: (l, b, 0, 0)) for shp in STATE_SHAPES]
    in_specs = [row_spec, mod_spec, next_row_spec, next_mod_spec,
                _vmem_full(), _const_spec((1, 3 * D_MODEL)), vec_spec, vec_spec]
    in_specs += _mixer_param_specs()
    alias_in, alias_specs, aliases = _alias_args(prev_states, len(in_specs), 1)
    outs = pl.pallas_call(
        functools.partial(_prompt_mixer_kernel, n_alias=len(alias_in)),
        grid=(n_seq, nc),
        in_specs=in_specs + alias_specs,
        out_specs=[row_spec] + state_specs,
        out_shape=[jax.ShapeDtypeStruct(x3.shape, F32)] + _stacked_state_shapes(n_seq),
        scratch_shapes=_mixer_scratch(1, L) + [pltpu.VMEM((L, N_PROJ_ALL), F32)],
        input_output_aliases=aliases,
        compiler_params=_compiler_params(2),
        name="prompt_mixers",
    )(x3, mods, x3, mods, lw["w_cat"], lw["gate_b"], lw["ln1_g"], lw["ln1_b"], *_mixer_param_args(lw), *alias_in)
    return outs[0], outs[1:]


def _sample_mixer_kernel(proj_ref, *rest, tv, n_alias):
    st_in = rest[:N_STATES]
    par = rest[N_STATES:N_STATES + N_MIXER_PARAMS]
    rest = rest[N_STATES + N_MIXER_PARAMS + n_alias:]
    y_refs = rest[:3]
    st_out = rest[3:3 + N_STATES]
    scr = rest[3 + N_STATES:]
    n_blk, rows, _ = proj_ref.shape
    proj = lambda off, width: proj_ref[:, :, off:off + width].reshape(n_blk * rows, width)
    ys = _mixer_chunk(proj, st_in, st_out, par, scr, n_seq=n_blk, rows=rows, tv=tv, pos0=1 + POOL_BUF)
    for y_ref, y in zip(y_refs, ys):
        y_ref[...] = y.reshape(n_blk, rows, D_MODEL)


def _sample_mixer_call(l, proj, states_in, prev_states, lw, *, tv):
    n_seq, rows, _ = proj.shape
    nb = SAMPLE_SEQ_BLOCK
    state_specs = [pl.BlockSpec((None, nb) + shp, lambda i: (l, i, 0, 0)) for shp in STATE_SHAPES]
    y_spec = pl.BlockSpec((nb, rows, D_MODEL), lambda i: (i, 0, 0))
    y_shape = jax.ShapeDtypeStruct((n_seq, rows, D_MODEL), F32)
    in_specs = [pl.BlockSpec((nb, rows, N_PROJ), lambda i: (i, 0, 0))] + state_specs + _mixer_param_specs()
    alias_in, alias_specs, aliases = _alias_args(prev_states, len(in_specs), 3)
    outs = pl.pallas_call(
        functools.partial(_sample_mixer_kernel, tv=tv, n_alias=len(alias_in)),
        grid=(n_seq // nb,),
        in_specs=in_specs + alias_specs,
        out_specs=[y_spec, y_spec, y_spec] + state_specs,
        out_shape=[y_shape, y_shape, y_shape] + _stacked_state_shapes(n_seq),
        scratch_shapes=_mixer_scratch(nb, rows),
        input_output_aliases=aliases,
        compiler_params=_compiler_params(1),
        name="sample_mixers",
    )(proj, *states_in, *_mixer_param_args(lw), *alias_in)
    return outs[:3], outs[3:]


def _merge_kernel(x_ref, mod_ref, yssd_ref, ym_ref, ypool_ref, wgl_ref, gb_ref,
                  lng_ref, lnb_ref, o_ref):
    sb, rb, _ = x_ref.shape
    n = sb * rb
    ub = _modulate(x_ref, mod_ref).astype(BF16)
    ys = [r[...].reshape(n, D_MODEL) for r in (yssd_ref, ym_ref, ypool_ref)]
    gates = [_sigmoid(_dot(ub, wgl_ref[:, OFF_GL + i * D_MODEL:OFF_GL + (i + 1) * D_MODEL])
                      + gb_ref[:, i * D_MODEL:(i + 1) * D_MODEL]) for i in range(3)]
    mix = _merge_rows(gates, ys, wgl_ref).reshape(sb, rb, D_MODEL)
    gate_a = mod_ref[:, :, 2 * D_MODEL:3 * D_MODEL]
    o_ref[...] = _layer_norm(ALPHA * x_ref[...] + (1.0 + gate_a) * mix, lng_ref[...], lnb_ref[...])


def _merge_call(x3, mods, ys, lw):
    n_seq, rows, _ = x3.shape
    sb, rb = _row_blocks(n_seq, rows)
    row_spec = pl.BlockSpec((sb, rb, D_MODEL), lambda i, j: (i, j, 0))
    vec_spec = _const_spec((1, D_MODEL))
    return pl.pallas_call(
        _merge_kernel,
        grid=(n_seq // sb, rows // rb),
        in_specs=[row_spec, pl.BlockSpec((sb, 1, 3 * D_MODEL), lambda i, j: (i, 0, 0)),
                  row_spec, row_spec, row_spec,
                  _vmem_full(), _const_spec((1, 3 * D_MODEL)), vec_spec, vec_spec],
        out_specs=row_spec,
        out_shape=jax.ShapeDtypeStruct(x3.shape, F32),
        compiler_params=_compiler_params(2),
        name="merge_norm",
    )(x3, mods, ys[0], ys[1], ys[2], lw["w_cat"], lw["gate_b"], lw["ln1_g"], lw["ln1_b"])


MOE_TILE = 512
MOE_BLOCK = 128
MOE_SLOT_BLOCKS = MOE_TILE // MOE_BLOCK + N_EGROUPS - 1
MOE_SLOTS = MOE_SLOT_BLOCKS * MOE_BLOCK
SIDE_POS, SIDE_HI, SIDE_MID, SIDE_LO = 0, 8, 16, 24


def _pad_rows(v, n):
    return jnp.concatenate([v, jnp.zeros((n - v.shape[0], v.shape[1]), v.dtype)], axis=0)


def _route_t(logits_t):
    t = logits_t.shape[1]
    row_g = lax.broadcasted_iota(jnp.int32, (SUBLANES, t), 0)
    lg = jnp.where(row_g < N_EGROUPS, logits_t[RT_G:RT_G + SUBLANES, :], -jnp.inf)
    g_max = jnp.max(lg, axis=0, keepdims=True)
    g_idx = jnp.min(jnp.where(lg == g_max, row_g, SUBLANES), axis=0, keepdims=True)
    g_prob = 1.0 / jnp.sum(jnp.exp(lg - g_max), axis=0, keepdims=True)
    row_e = lax.broadcasted_iota(jnp.int32, (N_EXPERTS, t), 0)
    le = jnp.where((row_e >> 2) == g_idx, logits_t[RT_E:RT_E + N_EXPERTS, :], -jnp.inf)
    v1 = jnp.max(le, axis=0, keepdims=True)
    i1 = jnp.min(jnp.where(le == v1, row_e, N_EXPERTS), axis=0, keepdims=True)
    le2 = jnp.where(row_e == i1, -jnp.inf, le)
    v2 = jnp.max(le2, axis=0, keepdims=True)
    i2 = jnp.min(jnp.where(le2 == v2, row_e, N_EXPERTS), axis=0, keepdims=True)
    e2 = jnp.exp(v2 - v1)
    p1 = g_prob / (1.0 + e2)
    p2 = g_prob * e2 / (1.0 + e2)
    wts = jnp.where(row_e == i1, p1, 0.0) + jnp.where(row_e == i2, p2, 0.0)
    w4 = None
    for g in range(N_EGROUPS):
        part = jnp.where(g_idx == g, wts[g * EXP_PER_GROUP:(g + 1) * EXP_PER_GROUP, :], 0.0)
        w4 = part if w4 is None else w4 + part
    return g_idx, w4


def _moe_kernel(x_ref, mod_ref, wrt_ref, brt_ref, wg_ref, wu_ref, wd_ref, lng_ref, lnb_ref, o_ref,
                sx_ref, sw_ref, so_ref):
    sb, rb, _ = x_ref.shape
    t = sb * rb
    u = _modulate(x_ref, mod_ref)
    u_hi = u.astype(BF16)
    u_lo = (u - u_hi.astype(F32)).astype(BF16)
    w = wrt_ref[...]
    w_hi = w.astype(BF16)
    w_lo = (w - w_hi.astype(F32)).astype(BF16)
    logits_t = _dot_nt(w_hi, u_hi) + _dot_nt(w_hi, u_lo) + _dot_nt(w_lo, u_hi) + brt_ref[...]
    g_idx, w4 = _route_t(logits_t)

    row_g = lax.broadcasted_iota(jnp.int32, (SUBLANES, t), 0)
    onehot_t = (row_g == g_idx).astype(F32)
    before = (lax.broadcasted_iota(jnp.int32, (t, t), 0) < lax.broadcasted_iota(jnp.int32, (t, t), 1))
    rank = _dot(onehot_t, before.astype(F32))
    cnt = jnp.sum(onehot_t, axis=1, keepdims=True)
    nblk = jnp.floor((cnt + (MOE_BLOCK - 1)) * (1.0 / MOE_BLOCK))
    sub = lax.broadcasted_iota(jnp.int32, (SUBLANES, 1), 0)
    first = jnp.zeros((SUBLANES, 1), F32)
    running = jnp.zeros((1, 1), F32)
    for g in range(1, N_EGROUPS):
        running = running + nblk[g - 1:g, :]
        first = first + jnp.where(sub == g, running, 0.0)
    pos_t = jnp.sum(onehot_t * (first * MOE_BLOCK + rank), axis=0, keepdims=True)
    blk_lane = lax.broadcasted_iota(jnp.int32, (SUBLANES, LANES), 1).astype(F32)
    in_blk = (blk_lane >= first) & (blk_lane < first + nblk)
    blk_group = jnp.sum(jnp.where(in_blk, sub.astype(F32), 0.0), axis=0, keepdims=True)

    w_hi4, w_mid4, w_lo4 = [p.astype(F32) for p in _split3(w4)]
    side_t = jnp.concatenate([_pad_rows(pos_t, SUBLANES), _pad_rows(w_hi4, SUBLANES),
                              _pad_rows(w_mid4, SUBLANES), _pad_rows(w_lo4, SUBLANES),
                              jnp.zeros((LANES - 4 * SUBLANES, t), F32)], axis=0)
    side = side_t.T

    slot_r = lax.broadcasted_iota(jnp.int32, (MOE_SLOTS, t), 0)
    send = (slot_r == pos_t.astype(jnp.int32)).astype(BF16)
    payload = jnp.concatenate([u_hi, side.astype(BF16)], axis=-1)
    sorted_rows = _dot(send, payload)
    sx_ref[...] = sorted_rows[:, 0:D_MODEL].astype(BF16)
    sw_ref[...] = sorted_rows[:, D_MODEL:D_MODEL + LANES]

    n_used = jnp.sum(nblk[:, 0:1]).astype(jnp.int32)
    for i in range(MOE_SLOT_BLOCKS):
        rows = slice(i * MOE_BLOCK, (i + 1) * MOE_BLOCK)

        def expert_block(i=i, rows=rows):
            g = blk_group[0, i].astype(jnp.int32)
            xb = sx_ref[rows, :]
            ws = sw_ref[rows, :]
            w_blk = (ws[:, SIDE_HI:SIDE_HI + EXP_PER_GROUP] + ws[:, SIDE_MID:SIDE_MID + EXP_PER_GROUP]
                     + ws[:, SIDE_LO:SIDE_LO + EXP_PER_GROUP])
            hid = []
            for r in range(EXP_PER_GROUP):
                e = g * EXP_PER_GROUP + r
                hid.append(_silu(_dot(xb, wg_ref[e])) * _dot(xb, wu_ref[e]) * w_blk[:, r:r + 1])
            so_ref[rows, :] = _dot(jnp.concatenate(hid, axis=-1), wd_ref[g]).astype(BF16)

        def empty_block(rows=rows):
            so_ref[rows, :] = jnp.zeros((MOE_BLOCK, D_MODEL), BF16)

        if i < MOE_TILE // MOE_BLOCK:
            expert_block()
        else:
            pl.when(i < n_used)(expert_block)
            pl.when(i >= n_used)(empty_block)

    slot_c = lax.broadcasted_iota(jnp.int32, (t, MOE_SLOTS), 1)
    fetch = (slot_c == side[:, SIDE_POS:SIDE_POS + 1].astype(jnp.int32)).astype(BF16)
    ffn = _dot(fetch, so_ref[...])
    gate_f = mod_ref[:, :, 2 * D_MODEL:3 * D_MODEL]
    o_ref[...] = _layer_norm(ALPHA * x_ref[...] + (1.0 + gate_f) * ffn.reshape(sb, rb, D_MODEL),
                             lng_ref[...], lnb_ref[...])


def _moe_call(x3, mods, lw):
    n_seq, rows, _ = x3.shape
    sb, rb = _row_blocks(n_seq, rows, MOE_TILE)
    row_spec = pl.BlockSpec((sb, rb, D_MODEL), lambda i, j: (i, j, 0))
    vec_spec = _const_spec((1, D_MODEL))
    return pl.pallas_call(
        _moe_kernel,
        grid=(n_seq // sb, rows // rb),
        in_specs=[row_spec, pl.BlockSpec((sb, 1, 3 * D_MODEL), lambda i, j: (i, 0, 0)),
                  _vmem_full(), _const_spec((LANES, 1)),
                  _vmem_full(), _vmem_full(), _vmem_full(), vec_spec, vec_spec],
        out_specs=row_spec,
        out_shape=jax.ShapeDtypeStruct(x3.shape, F32),
        scratch_shapes=[pltpu.VMEM((MOE_SLOTS, D_MODEL), BF16), pltpu.VMEM((MOE_SLOTS, LANES), F32),
                        pltpu.VMEM((MOE_SLOTS, D_MODEL), BF16)],
        compiler_params=_compiler_params(2),
        name="moe_norm",
    )(x3, mods, lw["w_rt_t"], lw["b_rt_col"], lw["w_e_gate"], lw["w_e_up"], lw["w_e_down"],
      lw["ln2_g"], lw["ln2_b"])


def _split_w_in(w):
    sizes = (D_MODEL, SSD_CONV_DIM, SSD_HEADS, D_MODEL, D_MODEL, D_MODEL, M_HEADS, M_HEADS, D_MODEL,
             D_MODEL, 3 * D_MODEL)
    out, off = [], 0
    for s in sizes:
        out.append(w[:, off:off + s])
        off += s
    return out


def _pad_lanes(v, width=LANES):
    return jnp.pad(v, ((0, 0), (0, width - v.shape[-1])))


def _layer_weights(l, p):
    wz, wxbc, wdt, wq, wk, wv, wi, wf, wo, wup, wgl = _split_w_in(p["w_in"][l])
    w_small = _pad_lanes(jnp.concatenate([wdt, wdt, wf, wi], axis=1))
    gate_b = p["mlstm_gate_b"][l]
    bias_row = jnp.concatenate([p["ssd_dt_bias"][l], p["ssd_dt_bias"][l], gate_b[M_HEADS:], gate_b[:M_HEADS]])
    head_params = jnp.concatenate([
        _pad_lanes(bias_row[None]), _pad_lanes(p["ssd_A_log"][l][None]), _pad_lanes(p["ssd_D"][l][None]),
        jnp.zeros((SUBLANES - 3, LANES), F32)], axis=0)
    w_rt = jnp.concatenate([_pad_lanes(p["w_rt_group"][l], RT_E), _pad_lanes(p["w_rt_expert"][l], LANES - RT_E)],
                           axis=1)
    b_rt = jnp.concatenate([_pad_lanes(p["b_rt_group"][l][None], RT_E),
                            _pad_lanes(p["b_rt_expert"][l][None], LANES - RT_E)], axis=1)
    row = lambda v: v[None]
    return dict(
        w_cat=jnp.concatenate([wz, wxbc, w_small, wq, wk, wv, wo, wup, wgl, p["w_br_ssd"][l], p["w_br_mlstm"][l],
                               p["w_br_pool"][l], p["w_out"][l]], axis=1).astype(BF16),
        conv_w=p["conv_w"][l], conv_b=row(p["conv_b"][l]), head_params=head_params,
        ssd_norm_w=row(p["ssd_norm_w"][l]), mlstm_norm_w=row(p["mlstm_norm_w"][l]),
        pool_w=p["pool_w"][l].astype(BF16), pool_scale=row(p["pool_scale"][l]),
        gate_b=row(p["gate_b"][l]),
        ln1_g=row(p["ln1_g"][l]), ln1_b=row(p["ln1_b"][l]),
        w_rt_t=w_rt.T, b_rt_col=b_rt.T,
        w_e_gate=p["w_e_gate"][l].astype(BF16), w_e_up=p["w_e_up"][l].astype(BF16),
        w_e_down=p["w_e_down"][l].reshape(N_EGROUPS, EXP_PER_GROUP * D_FF_E, D_MODEL).astype(BF16),
        ln2_g=row(p["ln2_g"][l]), ln2_b=row(p["ln2_b"][l]),
    )


def _flat_states(ssd, conv, mc, mn, mm, pool):
    return tuple(a.reshape(a.shape[:2] + shp) for a, shp in zip((ssd, conv, mc, mn, mm, pool), STATE_SHAPES))


def _unflat_states(states):
    ssd, conv, mc, mn, mm, pool = states
    d, n = ssd.shape[:2]
    return (ssd.reshape(d, n, SSD_HEADS, SSD_HEADDIM, SSD_STATE), conv,
            mc.reshape(d, n, M_HEADS, M_HEADDIM, M_HEADDIM), mn, mm.reshape(d, n, M_HEADS), pool)


def kernel(x_prompt, x_sample, state_ssd, state_conv, state_mlstm_C, state_mlstm_n, state_mlstm_m, state_pool, c_prompt, c_sample, w_ada_mix, b_ada_mix, w_in, conv_w, conv_b, ssd_A_log, ssd_dt_bias, ssd_D, ssd_norm_w, mlstm_gate_b, mlstm_norm_w, pool_w, pool_scale, gate_b, w_br_ssd, w_br_mlstm, w_br_pool, w_out, ln1_g, ln1_b, w_ada_ffn, b_ada_ffn, w_rt_group, b_rt_group, w_rt_expert, b_rt_expert, w_e_gate, w_e_up, w_e_down, ln2_g, ln2_b):
    params = dict(w_in=w_in, conv_w=conv_w, conv_b=conv_b, ssd_A_log=ssd_A_log, ssd_dt_bias=ssd_dt_bias,
                  ssd_D=ssd_D, ssd_norm_w=ssd_norm_w, mlstm_gate_b=mlstm_gate_b, mlstm_norm_w=mlstm_norm_w,
                  pool_w=pool_w, pool_scale=pool_scale, gate_b=gate_b, w_br_ssd=w_br_ssd,
                  w_br_mlstm=w_br_mlstm, w_br_pool=w_br_pool, w_out=w_out, ln1_g=ln1_g, ln1_b=ln1_b,
                  w_rt_group=w_rt_group, b_rt_group=b_rt_group, w_rt_expert=w_rt_expert,
                  b_rt_expert=b_rt_expert, w_e_gate=w_e_gate, w_e_up=w_e_up, w_e_down=w_e_down,
                  ln2_g=ln2_g, ln2_b=ln2_b)
    bp, seq, _ = x_prompt.shape
    bs, dec_seq, _ = x_sample.shape
    assert seq % CHUNK == 0 and 1 <= dec_seq <= SAMPLE_ROWS and bs % SAMPLE_SEQ_BLOCK == 0

    c_all = jnp.concatenate([c_prompt, c_sample], axis=0)
    mods_mix = _ada_call(c_all, w_ada_mix, b_ada_mix[:, None, :])
    mods_ffn = _ada_call(c_all, w_ada_ffn, b_ada_ffn[:, None, :])

    sample_in = _flat_states(state_ssd, state_conv, state_mlstm_C, state_mlstm_n, state_mlstm_m, state_pool)
    xp = x_prompt
    xs = jnp.pad(x_sample, ((0, 0), (0, SAMPLE_ROWS - dec_seq), (0, 0)))
    p_states, s_states = None, None
    for l in range(DEPTH):
        lw = _layer_weights(l, params)
        xp, p_states = _prompt_mixer_call(l, xp, mods_mix[l, :bp, None, :], p_states, lw)
        xp = _moe_call(xp, mods_ffn[l, :bp, None, :], lw)

        mods_s = mods_mix[l, bp:, None, :]
        proj = _proj_call(xs, mods_s, lw["w_cat"])
        ys, s_states = _sample_mixer_call(l, proj, sample_in, s_states, lw, tv=dec_seq)
        xs = _merge_call(xs, mods_s, ys, lw)
        xs = _moe_call(xs, mods_ffn[l, bp:, None, :], lw)
    return (xp, xs[:, :dec_seq]) + _unflat_states(p_states) + _unflat_states(s_states)
```

```python
import functools

import jax
import jax.numpy as jnp
from jax import lax
from jax.experimental import pallas as pl
from jax.experimental.pallas import tpu as pltpu

F32 = jnp.float32
BF16 = jnp.bfloat16

D_MODEL = 1024
DEPTH = 4
SSD_HEADS = 16
SSD_HEADDIM = 64
SSD_GROUPS = 2
SSD_REP = SSD_HEADS // SSD_GROUPS
SSD_STATE = 128
SSD_CONV = 4
SSD_CONV_DIM = D_MODEL + 2 * SSD_GROUPS * SSD_STATE
CHUNK = 128
M_HEADS = 4
M_HEADDIM = D_MODEL // M_HEADS
POOL_WINDOWS = (2, 4, 8, 16)
POOL_GW = D_MODEL // len(POOL_WINDOWS)
POOL_BUF = max(POOL_WINDOWS) - 1
N_EGROUPS = 4
EXP_PER_GROUP = 4
N_EXPERTS = N_EGROUPS * EXP_PER_GROUP
D_FF_E = D_MODEL // 4
ALPHA = (2 * DEPTH) ** 0.25
LN_EPS = 1e-5
RMS_EPS = 1e-6

SUBLANES = 8
LANES = 128
VMEM_LIMIT_BYTES = 56 * 1024 * 1024

OFF_Z = 0
OFF_XBC = OFF_Z + D_MODEL
OFF_SMALL = OFF_XBC + SSD_CONV_DIM
OFF_Q = OFF_SMALL + LANES
OFF_K = OFF_Q + D_MODEL
OFF_V = OFF_K + D_MODEL
OFF_O = OFF_V + D_MODEL
OFF_UP = OFF_O + D_MODEL
N_PROJ = OFF_UP + D_MODEL
SM_A, SM_DT, SM_F, SM_I, SM_END = 0, SSD_HEADS, 2 * SSD_HEADS, 2 * SSD_HEADS + M_HEADS, 2 * SSD_HEADS + 2 * M_HEADS
RT_G, RT_E = 0, 16
NEG_BIG = -1e30

ROW_TILE = 256
SAMPLE_ROWS = 8
SAMPLE_SEQ_BLOCK = 4

STATE_SHAPES = ((SSD_HEADS * SSD_HEADDIM, SSD_STATE), (SSD_CONV - 1, SSD_CONV_DIM),
                (M_HEADS * M_HEADDIM, M_HEADDIM), (M_HEADS, M_HEADDIM), (1, M_HEADS), (POOL_BUF, D_MODEL))
N_STATES = len(STATE_SHAPES)
N_MIXER_SCRATCH = 5
POOL_TOP = SUBLANES + POOL_BUF + 1


def _dot(a, b):
    return jnp.dot(a.astype(BF16), b.astype(BF16), preferred_element_type=F32)


def _dot_nt(a, b):
    return lax.dot_general(a.astype(BF16), b.astype(BF16), (((1,), (1,)), ((), ())),
                           preferred_element_type=F32)


def _dot_tn(a, b):
    return lax.dot_general(a.astype(BF16), b.astype(BF16), (((0,), (0,)), ((), ())),
                           preferred_element_type=F32)


def _split3(v):
    hi = v.astype(BF16)
    r1 = v - hi.astype(F32)
    mid = r1.astype(BF16)
    lo = (r1 - mid.astype(F32)).astype(BF16)
    return hi, mid, lo


def _select_dot(sel, v):
    n = v.shape[1]
    out = jnp.dot(sel.astype(BF16), jnp.concatenate(_split3(v), axis=1), preferred_element_type=F32)
    return out[:, 0:n] + out[:, n:2 * n] + out[:, 2 * n:3 * n]


def _select_dot_nt(sel, v):
    n = v.shape[0]
    out = lax.dot_general(sel.astype(BF16), jnp.concatenate(_split3(v), axis=0), (((1,), (1,)), ((), ())),
                          preferred_element_type=F32)
    return out[:, 0:n] + out[:, n:2 * n] + out[:, 2 * n:3 * n]


def _sigmoid(x):
    return 1.0 / (1.0 + jnp.exp(-x))


def _silu(x):
    return x * _sigmoid(x)


def _softplus(x):
    return jnp.maximum(x, 0.0) + jnp.log1p(jnp.exp(-jnp.abs(x)))


def _layer_norm(x, g, b):
    mu = jnp.mean(x, axis=-1, keepdims=True)
    xc = x - mu
    var = jnp.mean(xc * xc, axis=-1, keepdims=True)
    return xc * lax.rsqrt(var + LN_EPS) * g + b


def _compiler_params(n_grid):
    return pltpu.CompilerParams(dimension_semantics=("arbitrary",) * n_grid,
                                vmem_limit_bytes=VMEM_LIMIT_BYTES)


def _vmem_full():
    return pl.BlockSpec(memory_space=pltpu.VMEM)


def _const_spec(shape):
    return pl.BlockSpec(shape, lambda *_: (0,) * len(shape))


def _packed_weight_spec(half):
    return pl.BlockSpec((D_MODEL, N_PROJ), lambda *_: (0, half))


def _ada_kernel(c_ref, w_ref, b_ref, o_ref):
    o_ref[...] = _dot(c_ref[...], w_ref[...]) + b_ref[...]


def _ada_call(c_all, w, b):
    n = c_all.shape[0]
    return pl.pallas_call(
        _ada_kernel,
        grid=(DEPTH, 3),
        in_specs=[pl.BlockSpec((n, D_MODEL), lambda l, j: (0, 0)),
                  pl.BlockSpec((None, D_MODEL, D_MODEL), lambda l, j: (l, 0, j)),
                  pl.BlockSpec((None, 1, D_MODEL), lambda l, j: (l, 0, j))],
        out_specs=pl.BlockSpec((None, n, D_MODEL), lambda l, j: (l, 0, j)),
        out_shape=jax.ShapeDtypeStruct((DEPTH, n, 3 * D_MODEL), F32),
        compiler_params=_compiler_params(2),
        name="ada_mod",
    )(c_all, w, b)


def _modulate(x_ref, mod_ref):
    x = x_ref[...]
    shift = mod_ref[:, :, 0:D_MODEL]
    scale = mod_ref[:, :, D_MODEL:2 * D_MODEL]
    u = x * (1.0 + scale) + shift
    return u.reshape(x.shape[0] * x.shape[1], D_MODEL)


def _proj_kernel(x_ref, mod_ref, w_ref, o_ref):
    sb, rb, _ = x_ref.shape
    u = _modulate(x_ref, mod_ref).astype(BF16)
    col = 0
    while col < N_PROJ:
        width = min(D_MODEL, N_PROJ - col)
        o_ref[:, :, col:col + width] = _dot(u, w_ref[:, col:col + width]).reshape(sb, rb, width)
        col += width


def _row_blocks(n_seq, rows, tile=ROW_TILE):
    if rows >= tile:
        return 1, tile
    return tile // rows, rows


def _proj_call(x3, mods, wcat):
    n_seq, rows, _ = x3.shape
    sb, rb = _row_blocks(n_seq, rows)
    return pl.pallas_call(
        _proj_kernel,
        grid=(n_seq // sb, rows // rb),
        in_specs=[pl.BlockSpec((sb, rb, D_MODEL), lambda i, j: (i, j, 0)),
                  pl.BlockSpec((sb, 1, 3 * D_MODEL), lambda i, j: (i, 0, 0)),
                  _packed_weight_spec(0)],
        out_specs=pl.BlockSpec((sb, rb, N_PROJ), lambda i, j: (i, j, 0)),
        out_shape=jax.ShapeDtypeStruct((n_seq, rows, N_PROJ), F32),
        compiler_params=_compiler_params(2),
        name="in_proj",
    )(x3, mods, wcat)


def _mixer_chunk(proj, st_in, st_out, par, scr, *, n_seq, rows, tv, pos0, after_stage=lambda name: None):
    L = n_seq * rows
    rows_log2 = rows.bit_length() - 1
    assert rows == 1 << rows_log2
    ssd_i, conv_i, mc_i, mn_i, mm_i, pool_i = st_in
    ssd_o, conv_o, mc_o, mn_o, mm_o, pool_o = st_out
    convw_ref, convb_ref, hp_ref, snw_ref, mnw_ref, poolw_ref, pscale_ref = par
    xext_ref, pext_ref, psa_ref, psb_ref, yacc_ref = scr
    seq_rows = [slice(s * rows, (s + 1) * rows) for s in range(n_seq)]

    def per_seq(fn):
        parts = [fn(s, seq_rows[s]) for s in range(n_seq)]
        return parts[0] if n_seq == 1 else jnp.concatenate(parts, axis=0)

    row_l = lax.broadcasted_iota(jnp.int32, (L, L), 0)
    col_l = lax.broadcasted_iota(jnp.int32, (L, L), 1)
    causal = row_l >= col_l
    if n_seq > 1:
        same_seq = (row_l >> rows_log2) == (col_l >> rows_log2)
        causal = causal & same_seq

    lane = lax.broadcasted_iota(jnp.int32, (L, LANES), 1)
    pre = proj(OFF_SMALL, LANES) + hp_ref[0:1, :]
    sp = _softplus(pre)
    a_row = -jnp.exp(hp_ref[1:2, :])
    pmat = jnp.where(lane < SM_DT, sp * a_row,
                     jnp.where(lane < SM_F, sp,
                               jnp.where(lane < SM_I, -_softplus(-pre),
                                         jnp.where(lane < SM_END, pre, 0.0))))
    if tv < rows:
        row = lax.broadcasted_iota(jnp.int32, (L, LANES), 0)
        pad = jnp.where(lane < SM_I, 0.0, jnp.where(lane < SM_END, NEG_BIG, 0.0))
        pmat = jnp.where((row & (rows - 1)) < tv, pmat, pad)
    eye = (lax.broadcasted_iota(jnp.int32, (LANES, LANES), 0)
           == lax.broadcasted_iota(jnp.int32, (LANES, LANES), 1))
    cum = _select_dot(causal, pmat)
    pmat_t = _select_dot_nt(eye, pmat)
    cum_t = _select_dot_nt(eye, cum)
    if n_seq > 1:
        tot = _select_dot(same_seq, pmat)
    else:
        tot = cum[L - 1:L, :]

    xext_ref[:, SUBLANES - (SSD_CONV - 1):SUBLANES, :] = conv_i[...]
    xext_ref[:, SUBLANES:SUBLANES + rows, :] = proj(OFF_XBC, SSD_CONV_DIM).reshape(n_seq, rows, SSD_CONV_DIM)
    acc = convb_ref[...]
    for k in range(SSD_CONV):
        start = SUBLANES - (SSD_CONV - 1) + k
        acc = acc + xext_ref[:, start:start + rows, :].reshape(L, SSD_CONV_DIM) * convw_ref[k:k + 1, :]
    conv_o[...] = xext_ref[:, SUBLANES + tv - (SSD_CONV - 1):SUBLANES + tv, :]
    after_stage("conv")
    xbc = _silu(acc)
    xs = xbc[:, 0:D_MODEL]
    d_row = hp_ref[2:3, :]
    gw = SSD_REP * SSD_HEADDIM
    for g in range(SSD_GROUPS):
        grp = slice(g * gw, (g + 1) * gw)
        bm = xbc[:, D_MODEL + g * SSD_STATE:D_MODEL + (g + 1) * SSD_STATE]
        cm = xbc[:, D_MODEL + (SSD_GROUPS + g) * SSD_STATE:D_MODEL + (SSD_GROUPS + g + 1) * SSD_STATE]
        cb = _dot_nt(cm, bm)
        y_state = per_seq(lambda s, rs: _dot_nt(cm[rs], ssd_i[s, grp, :]))
        xw_parts = []
        first_head = lax.broadcasted_iota(jnp.int32, (L, LANES), 1) < SSD_HEADDIM
        for r in range(0, SSD_REP, 2):
            heads = (g * SSD_REP + r, g * SSD_REP + r + 1)
            ps = slice(heads[0] * SSD_HEADDIM, heads[0] * SSD_HEADDIM + LANES)
            per_head = lambda fn: jnp.where(first_head, fn(heads[0]), fn(heads[1]))
            wmats = []
            for h in heads:
                cum_c = cum[:, SM_A + h:SM_A + h + 1]
                cum_r = cum_t[SM_A + h:SM_A + h + 1, :]
                dt_r = pmat_t[SM_DT + h:SM_DT + h + 1, :]
                seg = jnp.where(causal, cum_c - cum_r, -jnp.inf)
                wmats.append((cb * jnp.exp(seg) * dt_r).astype(BF16))
            x_p = xs[:, ps]
            x_diag = jnp.concatenate([jnp.where(first_head, x_p, 0.0), jnp.where(first_head, 0.0, x_p)], axis=0)
            decay_in = per_head(lambda h: jnp.exp(cum[:, SM_A + h:SM_A + h + 1]))
            y = (_dot(jnp.concatenate(wmats, axis=1), x_diag)
                 + y_state[:, r * SSD_HEADDIM:r * SSD_HEADDIM + LANES] * decay_in)
            yacc_ref[:, ps] = y + per_head(lambda h: d_row[:, h:h + 1]) * x_p
            xw_parts.append(x_p * per_head(
                lambda h: jnp.exp(tot[:, SM_A + h:SM_A + h + 1] - cum[:, SM_A + h:SM_A + h + 1])
                * pmat[:, SM_DT + h:SM_DT + h + 1]))
            after_stage("ssd_head")
            after_stage("ssd_head")
        xw = jnp.concatenate(xw_parts, axis=-1)
        for s in range(n_seq):
            upd = _dot_tn(xw[seq_rows[s]], bm[seq_rows[s]])
            t0 = s * rows if n_seq > 1 else 0
            for r in range(SSD_REP):
                h = g * SSD_REP + r
                hs = slice(h * SSD_HEADDIM, (h + 1) * SSD_HEADDIM)
                decay = jnp.exp(tot[t0:t0 + 1, SM_A + h:SM_A + h + 1])
                ssd_o[s, hs, :] = decay * ssd_i[s, hs, :] + upd[r * SSD_HEADDIM:(r + 1) * SSD_HEADDIM, :]
    yz = yacc_ref[...] * _silu(proj(OFF_Z, D_MODEL))
    y_ssd = yz * lax.rsqrt(jnp.mean(yz * yz, axis=-1, keepdims=True) + RMS_EPS) * snw_ref[...]
    after_stage("ssd")

    ym_parts = []
    for h in range(M_HEADS):
        hs = slice(h * M_HEADDIM, (h + 1) * M_HEADDIM)
        q_h = proj(OFF_Q + h * M_HEADDIM, M_HEADDIM)
        k_h = proj(OFF_K + h * M_HEADDIM, M_HEADDIM) * (M_HEADDIM ** -0.5)
        v_h = proj(OFF_V + h * M_HEADDIM, M_HEADDIM)
        o_h = proj(OFF_O + h * M_HEADDIM, M_HEADDIM)
        b_c = cum[:, SM_F + h:SM_F + h + 1]
        b_r = cum_t[SM_F + h:SM_F + h + 1, :]
        i_c = pmat[:, SM_I + h:SM_I + h + 1]
        i_r = pmat_t[SM_I + h:SM_I + h + 1, :]
        last_b = tot[:, SM_F + h:SM_F + h + 1]
        m_prev = per_seq(lambda s, rs: jnp.broadcast_to(mm_i[s, :, h:h + 1], (rows, 1)))
        dmat = jnp.where(causal, b_c - b_r + i_r, -jnp.inf)
        m_st = b_c + m_prev
        m = jnp.maximum(m_st, jnp.max(dmat, axis=-1, keepdims=True))
        wts = jnp.exp(dmat - m) * _dot_nt(q_h, k_h)
        ws = jnp.exp(m_st - m)
        cq = per_seq(lambda s, rs: _dot_nt(q_h[rs], mc_i[s, hs, :]))
        nq = per_seq(lambda s, rs: jnp.sum(q_h[rs] * mn_i[s, h:h + 1, :], axis=-1, keepdims=True))
        num = _dot(wts, v_h) + ws * cq
        den = jnp.sum(wts, axis=-1, keepdims=True) + ws * nq
        hc = num / jnp.maximum(jnp.abs(den), jnp.exp(-m))
        m_new = per_seq(lambda s, rs: jnp.broadcast_to(m[rs.stop - 1:rs.stop, :], (rows, 1)))
        wsrc = jnp.exp(last_b - b_c + i_c - m_new)
        wprev = jnp.exp(last_b + m_prev - m_new)
        vw = v_h * wsrc
        kw = k_h * wsrc
        for s in range(n_seq):
            rs = seq_rows[s]
            wp = wprev[rs.start:rs.start + 1, :]
            mc_o[s, hs, :] = wp * mc_i[s, hs, :] + _dot_tn(vw[rs], k_h[rs])
            mn_o[s, h:h + 1, :] = wp * mn_i[s, h:h + 1, :] + jnp.sum(kw[rs], axis=0, keepdims=True)
            mm_o[s, :, h:h + 1] = m_new[rs.start:rs.start + 1, :]
        mu = jnp.mean(hc, axis=-1, keepdims=True)
        hd = hc - mu
        var = jnp.mean(hd * hd, axis=-1, keepdims=True)
        ym_parts.append(hd * lax.rsqrt(var + LN_EPS) * mnw_ref[:, hs] * _sigmoid(o_h))
        after_stage("mlstm_head")
    y_m = jnp.concatenate(ym_parts, axis=-1)

    assert POOL_WINDOWS == tuple(2 << g for g in range(len(POOL_WINDOWS)))
    first_row = POOL_TOP - POOL_BUF
    pext_ref[:, 0:first_row, :] = jnp.zeros((n_seq, first_row, D_MODEL), F32)
    for ref in (psa_ref, psb_ref):
        ref[:, 0:SUBLANES, :] = jnp.zeros((n_seq, SUBLANES, D_MODEL), F32)
    pext_ref[:, first_row:POOL_TOP, :] = pool_i[...]
    up = proj(OFF_UP, D_MODEL)
    pext_ref[:, POOL_TOP:POOL_TOP + rows, :] = up.reshape(n_seq, rows, D_MODEL)
    span = POOL_TOP - SUBLANES + rows
    src, dst, shift = pext_ref, psa_ref, 1
    for g in range(len(POOL_WINDOWS) - 1):
        lanes = slice(g * POOL_GW, D_MODEL)
        dst[:, SUBLANES:SUBLANES + span, lanes] = (src[:, SUBLANES:SUBLANES + span, lanes]
                                                   + src[:, SUBLANES - shift:SUBLANES - shift + span, lanes])
        src, dst, shift = dst, (psb_ref if dst is psa_ref else psa_ref), 2 * shift
    last = slice((len(POOL_WINDOWS) - 1) * POOL_GW, D_MODEL)
    widest = src[:, POOL_TOP:POOL_TOP + rows, last] + src[:, POOL_TOP - shift:POOL_TOP - shift + rows, last]
    pos = (lax.broadcasted_iota(jnp.int32, (L, 1), 0) & (rows - 1)) + pos0
    yp_parts = []
    for g, w in enumerate(POOL_WINDOWS):
        gs = slice(g * POOL_GW, (g + 1) * POOL_GW)
        if g == len(POOL_WINDOWS) - 1:
            wsum = widest.reshape(L, POOL_GW)
        else:
            wsum = (psa_ref if g % 2 == 0 else psb_ref)[:, POOL_TOP:POOL_TOP + rows, gs].reshape(L, POOL_GW)
        cnt = jnp.minimum(pos, w).astype(F32)
        dlt = wsum / cnt - up[:, gs]
        yp_parts.append(_dot(dlt, poolw_ref[g]) * pscale_ref[:, gs])
        after_stage("pool")
    pool_o[...] = pext_ref[:, first_row + tv:POOL_TOP + tv, :]
    return y_ssd, y_m, jnp.concatenate(yp_parts, axis=-1)


def _merge_rows(gates, ys, w_ref, col0=0):
    merged = None
    for i, (gate, y) in enumerate(zip(gates, ys)):
        lo = OFF_BR - col0 + i * D_MODEL
        term = gate * _dot(y, w_ref[:, lo:lo + D_MODEL])
        merged = term if merged is None else merged + term
    return _dot(merged, w_ref[:, OFF_OUT - col0:OFF_OUT - col0 + D_MODEL])


N_MIXER_PARAMS = 7


def _mixer_param_specs():
    return [_const_spec((SSD_CONV, SSD_CONV_DIM)), _const_spec((1, SSD_CONV_DIM)),
            _const_spec((SUBLANES, LANES)), _const_spec((1, D_MODEL)), _const_spec((1, D_MODEL)),
            _const_spec((len(POOL_WINDOWS), POOL_GW, POOL_GW)), _const_spec((1, D_MODEL))]


def _mixer_param_args(lw):
    return (lw["conv_w"], lw["conv_b"], lw["head_params"], lw["ssd_norm_w"], lw["mlstm_norm_w"],
            lw["pool_w"], lw["pool_scale"])


def _mixer_scratch(n_seq, rows):
    pool_rows = pltpu.VMEM((n_seq, POOL_TOP + rows, D_MODEL), F32)
    return [pltpu.VMEM((n_seq, SUBLANES + rows, SSD_CONV_DIM), F32), pool_rows, pool_rows, pool_rows,
            pltpu.VMEM((n_seq * rows, D_MODEL), F32)]


def _stacked_state_shapes(n_seq):
    return [jax.ShapeDtypeStruct((DEPTH, n_seq) + shp, F32) for shp in STATE_SHAPES]


def _alias_args(prev_states, n_inputs_before, n_outputs_before):
    if prev_states is None:
        return [], [], {}
    specs = [pl.BlockSpec(memory_space=pl.ANY)] * N_STATES
    aliases = {n_inputs_before + k: n_outputs_before + k for k in range(N_STATES)}
    return list(prev_states), specs, aliases


N_PROJ_ALL = N_PROJ + 3 * D_MODEL
OFF_GL = N_PROJ
OFF_BR = N_PROJ_ALL
OFF_OUT = OFF_BR + 3 * D_MODEL
assert OFF_GL == N_PROJ and OFF_OUT + D_MODEL - OFF_GL <= N_PROJ


def _column_pieces(lo, hi, width):
    return [(off, min(width, hi - off)) for off in range(lo, hi, width)]


def _prompt_mixer_kernel(x_ref, mod_ref, xn_ref, modn_ref, wcat_ref, gb_ref, lng_ref, lnb_ref, *rest, n_alias):
    par = rest[:N_MIXER_PARAMS]
    rest = rest[N_MIXER_PARAMS + n_alias:]
    o_ref = rest[0]
    states = rest[1:1 + N_STATES]
    scr = rest[1 + N_STATES:1 + N_STATES + N_MIXER_SCRATCH]
    pscr_ref = rest[1 + N_STATES + N_MIXER_SCRATCH]
    L = x_ref.shape[0]
    b = pl.program_id(0)
    c = pl.program_id(1)

    def modulated(xr, mr):
        return (xr[...] * (1.0 + mr[:, D_MODEL:2 * D_MODEL]) + mr[:, 0:D_MODEL]).astype(BF16)

    def project_into_scratch(ub, piece):
        off, width = piece
        pscr_ref[:, off:off + width] = _dot(ub, wcat_ref[:, off:off + width])

    @pl.when(c == 0)
    def _fresh_prompt_states():
        for ref in states:
            ref[...] = jnp.zeros(ref.shape, ref.dtype)

    @pl.when(jnp.logical_and(b == 0, c == 0))
    def _first_chunk_projections():
        ub0 = modulated(x_ref, mod_ref)
        for piece in _column_pieces(0, N_PROJ, D_MODEL) + _column_pieces(OFF_GL, N_PROJ_ALL, D_MODEL):
            project_into_scratch(ub0, piece)

    ub_next = modulated(xn_ref, modn_ref)
    ready = []
    released_by = {
        "conv": [_column_pieces(OFF_XBC, OFF_Q, 896)],
        "ssd": [_column_pieces(OFF_Z, OFF_XBC, D_MODEL)],
        "mlstm_head": [[(off + h * M_HEADDIM, M_HEADDIM) for off in (OFF_Q, OFF_K, OFF_V, OFF_O)]
                       for h in range(M_HEADS)],
        "pool": [_column_pieces(OFF_UP, N_PROJ, D_MODEL)],
    }
    pieces_per_call = {"conv": 2, "ssd_head": 1}

    def after_stage(name):
        if released_by.get(name):
            ready.extend(released_by[name].pop(0))
        for _ in range(min(len(ready), pieces_per_call.get(name, len(ready)))):
            project_into_scratch(ub_next, ready.pop(0))

    gates = [_sigmoid(pscr_ref[:, OFF_GL + i * D_MODEL:OFF_GL + (i + 1) * D_MODEL]
                      + gb_ref[:, i * D_MODEL:(i + 1) * D_MODEL]) for i in range(3)]
    ready.extend(_column_pieces(OFF_GL, N_PROJ_ALL, 768))
    proj = lambda off, width: pscr_ref[:, off:off + width]
    ys = _mixer_chunk(proj, states, states, par, scr, n_seq=1, rows=L, tv=L, pos0=c * L + 1,
                      after_stage=after_stage)
    after_stage("rest")
    assert not ready and not any(released_by.values())
    mix = _merge_rows(gates, ys, wcat_ref)
    o_ref[...] = _layer_norm(ALPHA * x_ref[...] + (1.0 + mod_ref[:, 2 * D_MODEL:3 * D_MODEL]) * mix,
                             lng_ref[...], lnb_ref[...])


def _prompt_mixer_call(l, x3, mods, prev_states, lw):
    n_seq, rows, _ = x3.shape
    L = CHUNK
    nc = rows // L

    def next_chunk(b, c):
        flat = jnp.minimum(b * nc + c + 1, n_seq * nc - 1)
        return flat // nc, flat % nc

    row_spec = pl.BlockSpec((None, L, D_MODEL), lambda b, c: (b, c, 0))
    mod_spec = pl.BlockSpec((None, 1, 3 * D_MODEL), lambda b, c: (b, 0, 0))
    next_row_spec = pl.BlockSpec((None, L, D_MODEL), lambda b, c: next_chunk(b, c) + (0,))
    next_mod_spec = pl.BlockSpec((None, 1, 3 * D_MODEL), lambda b, c: (next_chunk(b, c)[0], 0, 0))
    vec_spec = _const_spec((1, D_MODEL))
    state_specs = [pl.BlockSpec((None, 1) + shp, lambda b, c: (l, b, 0, 0)) for shp in STATE_SHAPES]
    in_specs = [row_spec, mod_spec, next_row_spec, next_mod_spec,
                _vmem_full(), _const_spec((1, 3 * D_MODEL)), vec_spec, vec_spec]
    in_specs += _mixer_param_specs()
    alias_in, alias_specs, aliases = _alias_args(prev_states, len(in_specs), 1)
    outs = pl.pallas_call(
        functools.partial(_prompt_mixer_kernel, n_alias=len(alias_in)),
        grid=(n_seq, nc),
        in_specs=in_specs + alias_specs,
        out_specs=[row_spec] + state_specs,
        out_shape=[jax.ShapeDtypeStruct(x3.shape, F32)] + _stacked_state_shapes(n_seq),
        scratch_shapes=_mixer_scratch(1, L) + [pltpu.VMEM((L, N_PROJ_ALL), F32)],
        input_output_aliases=aliases,
        compiler_params=_compiler_params(2),
        name="prompt_mixers",
    )(x3, mods, x3, mods, lw["w_cat"], lw["gate_b"], lw["ln1_g"], lw["ln1_b"], *_mixer_param_args(lw), *alias_in)
    return outs[0], outs[1:]


def _sample_mixer_kernel(proj_ref, *rest, tv, n_alias):
    st_in = rest[:N_STATES]
    par = rest[N_STATES:N_STATES + N_MIXER_PARAMS]
    rest = rest[N_STATES + N_MIXER_PARAMS + n_alias:]
    y_refs = rest[:3]
    st_out = rest[3:3 + N_STATES]
    scr = rest[3 + N_STATES:]
    n_blk, rows, _ = proj_ref.shape
    proj = lambda off, width: proj_ref[:, :, off:off + width].reshape(n_blk * rows, width)
    ys = _mixer_chunk(proj, st_in, st_out, par, scr, n_seq=n_blk, rows=rows, tv=tv, pos0=1 + POOL_BUF)
    for y_ref, y in zip(y_refs, ys):
        y_ref[...] = y.reshape(n_blk, rows, D_MODEL)


def _sample_mixer_call(l, proj, states_in, prev_states, lw, *, tv):
    n_seq, rows, _ = proj.shape
    nb = SAMPLE_SEQ_BLOCK
    state_specs = [pl.BlockSpec((None, nb) + shp, lambda i: (l, i, 0, 0)) for shp in STATE_SHAPES]
    y_spec = pl.BlockSpec((nb, rows, D_MODEL), lambda i: (i, 0, 0))
    y_shape = jax.ShapeDtypeStruct((n_seq, rows, D_MODEL), F32)
    in_specs = [pl.BlockSpec((nb, rows, N_PROJ), lambda i: (i, 0, 0))] + state_specs + _mixer_param_specs()
    alias_in, alias_specs, aliases = _alias_args(prev_states, len(in_specs), 3)
    outs = pl.pallas_call(
        functools.partial(_sample_mixer_kernel, tv=tv, n_alias=len(alias_in)),
        grid=(n_seq // nb,),
        in_specs=in_specs + alias_specs,
        out_specs=[y_spec, y_spec, y_spec] + state_specs,
        out_shape=[y_shape, y_shape, y_shape] + _stacked_state_shapes(n_seq),
        scratch_shapes=_mixer_scratch(nb, rows),
        input_output_aliases=aliases,
        compiler_params=_compiler_params(1),
        name="sample_mixers",
    )(proj, *states_in, *_mixer_param_args(lw), *alias_in)
    return outs[:3], outs[3:]


def _merge_kernel(x_ref, mod_ref, yssd_ref, ym_ref, ypool_ref, wgl_ref, gb_ref,
                  lng_ref, lnb_ref, o_ref):
    sb, rb, _ = x_ref.shape
    n = sb * rb
    ub = _modulate(x_ref, mod_ref).astype(BF16)
    ys = [r[...].reshape(n, D_MODEL) for r in (yssd_ref, ym_ref, ypool_ref)]
    gates = [_sigmoid(_dot(ub, wgl_ref[:, i * D_MODEL:(i + 1) * D_MODEL])
                      + gb_ref[:, i * D_MODEL:(i + 1) * D_MODEL]) for i in range(3)]
    mix = _merge_rows(gates, ys, wgl_ref, col0=OFF_GL).reshape(sb, rb, D_MODEL)
    gate_a = mod_ref[:, :, 2 * D_MODEL:3 * D_MODEL]
    o_ref[...] = _layer_norm(ALPHA * x_ref[...] + (1.0 + gate_a) * mix, lng_ref[...], lnb_ref[...])


def _merge_call(x3, mods, ys, lw):
    n_seq, rows, _ = x3.shape
    sb, rb = _row_blocks(n_seq, rows)
    row_spec = pl.BlockSpec((sb, rb, D_MODEL), lambda i, j: (i, j, 0))
    vec_spec = _const_spec((1, D_MODEL))
    return pl.pallas_call(
        _merge_kernel,
        grid=(n_seq // sb, rows // rb),
        in_specs=[row_spec, pl.BlockSpec((sb, 1, 3 * D_MODEL), lambda i, j: (i, 0, 0)),
                  row_spec, row_spec, row_spec,
                  _packed_weight_spec(1), _const_spec((1, 3 * D_MODEL)), vec_spec, vec_spec],
        out_specs=row_spec,
        out_shape=jax.ShapeDtypeStruct(x3.shape, F32),
        compiler_params=_compiler_params(2),
        name="merge_norm",
    )(x3, mods, ys[0], ys[1], ys[2], lw["w_cat"], lw["gate_b"], lw["ln1_g"], lw["ln1_b"])


MOE_TILE = 512
MOE_BLOCK = 128
MOE_SLOT_BLOCKS = MOE_TILE // MOE_BLOCK + N_EGROUPS - 1
MOE_SLOTS = MOE_SLOT_BLOCKS * MOE_BLOCK
SIDE_POS, SIDE_HI, SIDE_MID, SIDE_LO = 0, 8, 16, 24


def _pad_rows(v, n):
    return jnp.concatenate([v, jnp.zeros((n - v.shape[0], v.shape[1]), v.dtype)], axis=0)


def _route_t(logits_t):
    t = logits_t.shape[1]
    row_g = lax.broadcasted_iota(jnp.int32, (SUBLANES, t), 0)
    lg = jnp.where(row_g < N_EGROUPS, logits_t[RT_G:RT_G + SUBLANES, :], -jnp.inf)
    g_max = jnp.max(lg, axis=0, keepdims=True)
    g_idx = jnp.min(jnp.where(lg == g_max, row_g, SUBLANES), axis=0, keepdims=True)
    g_prob = 1.0 / jnp.sum(jnp.exp(lg - g_max), axis=0, keepdims=True)
    row_e = lax.broadcasted_iota(jnp.int32, (N_EXPERTS, t), 0)
    le = jnp.where((row_e >> 2) == g_idx, logits_t[RT_E:RT_E + N_EXPERTS, :], -jnp.inf)
    v1 = jnp.max(le, axis=0, keepdims=True)
    i1 = jnp.min(jnp.where(le == v1, row_e, N_EXPERTS), axis=0, keepdims=True)
    le2 = jnp.where(row_e == i1, -jnp.inf, le)
    v2 = jnp.max(le2, axis=0, keepdims=True)
    i2 = jnp.min(jnp.where(le2 == v2, row_e, N_EXPERTS), axis=0, keepdims=True)
    e2 = jnp.exp(v2 - v1)
    p1 = g_prob / (1.0 + e2)
    p2 = g_prob * e2 / (1.0 + e2)
    wts = jnp.where(row_e == i1, p1, 0.0) + jnp.where(row_e == i2, p2, 0.0)
    w4 = None
    for g in range(N_EGROUPS):
        part = jnp.where(g_idx == g, wts[g * EXP_PER_GROUP:(g + 1) * EXP_PER_GROUP, :], 0.0)
        w4 = part if w4 is None else w4 + part
    return g_idx, w4


def _moe_kernel(x_ref, mod_ref, wrt_ref, brt_ref, wg_ref, wu_ref, wd_ref, lng_ref, lnb_ref, o_ref,
                sx_ref, sw_ref, so_ref):
    sb, rb, _ = x_ref.shape
    t = sb * rb
    u = _modulate(x_ref, mod_ref)
    u_hi = u.astype(BF16)
    u_lo = (u - u_hi.astype(F32)).astype(BF16)
    w = wrt_ref[...]
    w_hi = w.astype(BF16)
    w_lo = (w - w_hi.astype(F32)).astype(BF16)
    logits_t = _dot_nt(w_hi, u_hi) + _dot_nt(w_hi, u_lo) + _dot_nt(w_lo, u_hi) + brt_ref[...]
    g_idx, w4 = _route_t(logits_t)

    row_g = lax.broadcasted_iota(jnp.int32, (SUBLANES, t), 0)
    onehot_t = (row_g == g_idx).astype(F32)
    before = (lax.broadcasted_iota(jnp.int32, (t, t), 0) < lax.broadcasted_iota(jnp.int32, (t, t), 1))
    rank = _dot(onehot_t, before.astype(F32))
    cnt = jnp.sum(onehot_t, axis=1, keepdims=True)
    nblk = jnp.floor((cnt + (MOE_BLOCK - 1)) * (1.0 / MOE_BLOCK))
    sub = lax.broadcasted_iota(jnp.int32, (SUBLANES, 1), 0)
    first = jnp.zeros((SUBLANES, 1), F32)
    running = jnp.zeros((1, 1), F32)
    for g in range(1, N_EGROUPS):
        running = running + nblk[g - 1:g, :]
        first = first + jnp.where(sub == g, running, 0.0)
    pos_t = jnp.sum(onehot_t * (first * MOE_BLOCK + rank), axis=0, keepdims=True)
    blk_lane = lax.broadcasted_iota(jnp.int32, (SUBLANES, LANES), 1).astype(F32)
    in_blk = (blk_lane >= first) & (blk_lane < first + nblk)
    blk_group = jnp.sum(jnp.where(in_blk, sub.astype(F32), 0.0), axis=0, keepdims=True)

    w_hi4, w_mid4, w_lo4 = [p.astype(F32) for p in _split3(w4)]
    side_t = jnp.concatenate([_pad_rows(pos_t, SUBLANES), _pad_rows(w_hi4, SUBLANES),
                              _pad_rows(w_mid4, SUBLANES), _pad_rows(w_lo4, SUBLANES),
                              jnp.zeros((LANES - 4 * SUBLANES, t), F32)], axis=0)
    side = side_t.T

    slot_r = lax.broadcasted_iota(jnp.int32, (MOE_SLOTS, t), 0)
    send = (slot_r == pos_t.astype(jnp.int32)).astype(BF16)
    payload = jnp.concatenate([u_hi, side.astype(BF16)], axis=-1)
    sorted_rows = _dot(send, payload)
    sx_ref[...] = sorted_rows[:, 0:D_MODEL].astype(BF16)
    sw_ref[...] = sorted_rows[:, D_MODEL:D_MODEL + LANES]

    n_used = jnp.sum(nblk[:, 0:1]).astype(jnp.int32)
    for i in range(MOE_SLOT_BLOCKS):
        rows = slice(i * MOE_BLOCK, (i + 1) * MOE_BLOCK)

        def expert_block(i=i, rows=rows):
            g = blk_group[0, i].astype(jnp.int32)
            xb = sx_ref[rows, :]
            ws = sw_ref[rows, :]
            w_blk = (ws[:, SIDE_HI:SIDE_HI + EXP_PER_GROUP] + ws[:, SIDE_MID:SIDE_MID + EXP_PER_GROUP]
                     + ws[:, SIDE_LO:SIDE_LO + EXP_PER_GROUP])
            hid = []
            for r in range(EXP_PER_GROUP):
                e = g * EXP_PER_GROUP + r
                hid.append(_silu(_dot(xb, wg_ref[e])) * _dot(xb, wu_ref[e]) * w_blk[:, r:r + 1])
            so_ref[rows, :] = _dot(jnp.concatenate(hid, axis=-1), wd_ref[g]).astype(BF16)

        def empty_block(rows=rows):
            so_ref[rows, :] = jnp.zeros((MOE_BLOCK, D_MODEL), BF16)

        if i < MOE_TILE // MOE_BLOCK:
            expert_block()
        else:
            pl.when(i < n_used)(expert_block)
            pl.when(i >= n_used)(empty_block)

    slot_c = lax.broadcasted_iota(jnp.int32, (t, MOE_SLOTS), 1)
    fetch = (slot_c == side[:, SIDE_POS:SIDE_POS + 1].astype(jnp.int32)).astype(BF16)
    ffn = _dot(fetch, so_ref[...])
    gate_f = mod_ref[:, :, 2 * D_MODEL:3 * D_MODEL]
    o_ref[...] = _layer_norm(ALPHA * x_ref[...] + (1.0 + gate_f) * ffn.reshape(sb, rb, D_MODEL),
                             lng_ref[...], lnb_ref[...])


def _moe_call(x3, mods, lw):
    n_seq, rows, _ = x3.shape
    sb, rb = _row_blocks(n_seq, rows, MOE_TILE)
    row_spec = pl.BlockSpec((sb, rb, D_MODEL), lambda i, j: (i, j, 0))
    vec_spec = _const_spec((1, D_MODEL))
    return pl.pallas_call(
        _moe_kernel,
        grid=(n_seq // sb, rows // rb),
        in_specs=[row_spec, pl.BlockSpec((sb, 1, 3 * D_MODEL), lambda i, j: (i, 0, 0)),
                  _vmem_full(), _const_spec((LANES, 1)),
                  _vmem_full(), _vmem_full(), _vmem_full(), vec_spec, vec_spec],
        out_specs=row_spec,
        out_shape=jax.ShapeDtypeStruct(x3.shape, F32),
        scratch_shapes=[pltpu.VMEM((MOE_SLOTS, D_MODEL), BF16), pltpu.VMEM((MOE_SLOTS, LANES), F32),
                        pltpu.VMEM((MOE_SLOTS, D_MODEL), BF16)],
        compiler_params=_compiler_params(2),
        name="moe_norm",
    )(x3, mods, lw["w_rt_t"], lw["b_rt_col"], lw["w_e_gate"], lw["w_e_up"], lw["w_e_down"],
      lw["ln2_g"], lw["ln2_b"])


def _split_w_in(w):
    sizes = (D_MODEL, SSD_CONV_DIM, SSD_HEADS, D_MODEL, D_MODEL, D_MODEL, M_HEADS, M_HEADS, D_MODEL,
             D_MODEL, 3 * D_MODEL)
    out, off = [], 0
    for s in sizes:
        out.append(w[:, off:off + s])
        off += s
    return out


def _pad_lanes(v, width=LANES):
    return jnp.pad(v, ((0, 0), (0, width - v.shape[-1])))


def _layer_weights(l, p):
    wz, wxbc, wdt, wq, wk, wv, wi, wf, wo, wup, wgl = _split_w_in(p["w_in"][l])
    w_small = _pad_lanes(jnp.concatenate([wdt, wdt, wf, wi], axis=1))
    gate_b = p["mlstm_gate_b"][l]
    bias_row = jnp.concatenate([p["ssd_dt_bias"][l], p["ssd_dt_bias"][l], gate_b[M_HEADS:], gate_b[:M_HEADS]])
    head_params = jnp.concatenate([
        _pad_lanes(bias_row[None]), _pad_lanes(p["ssd_A_log"][l][None]), _pad_lanes(p["ssd_D"][l][None]),
        jnp.zeros((SUBLANES - 3, LANES), F32)], axis=0)
    w_rt = jnp.concatenate([_pad_lanes(p["w_rt_group"][l], RT_E), _pad_lanes(p["w_rt_expert"][l], LANES - RT_E)],
                           axis=1)
    b_rt = jnp.concatenate([_pad_lanes(p["b_rt_group"][l][None], RT_E),
                            _pad_lanes(p["b_rt_expert"][l][None], LANES - RT_E)], axis=1)
    row = lambda v: v[None]
    return dict(
        w_cat=jnp.concatenate([wz, wxbc, w_small, wq, wk, wv, wo, wup, wgl, p["w_br_ssd"][l], p["w_br_mlstm"][l],
                               p["w_br_pool"][l], p["w_out"][l]], axis=1).astype(BF16),
        conv_w=p["conv_w"][l], conv_b=row(p["conv_b"][l]), head_params=head_params,
        ssd_norm_w=row(p["ssd_norm_w"][l]), mlstm_norm_w=row(p["mlstm_norm_w"][l]),
        pool_w=p["pool_w"][l].astype(BF16), pool_scale=row(p["pool_scale"][l]),
        gate_b=row(p["gate_b"][l]),
        ln1_g=row(p["ln1_g"][l]), ln1_b=row(p["ln1_b"][l]),
        w_rt_t=w_rt.T, b_rt_col=b_rt.T,
        w_e_gate=p["w_e_gate"][l].astype(BF16), w_e_up=p["w_e_up"][l].astype(BF16),
        w_e_down=p["w_e_down"][l].reshape(N_EGROUPS, EXP_PER_GROUP * D_FF_E, D_MODEL).astype(BF16),
        ln2_g=row(p["ln2_g"][l]), ln2_b=row(p["ln2_b"][l]),
    )


def _flat_states(ssd, conv, mc, mn, mm, pool):
    return tuple(a.reshape(a.shape[:2] + shp) for a, shp in zip((ssd, conv, mc, mn, mm, pool), STATE_SHAPES))


def _unflat_states(states):
    ssd, conv, mc, mn, mm, pool = states
    d, n = ssd.shape[:2]
    return (ssd.reshape(d, n, SSD_HEADS, SSD_HEADDIM, SSD_STATE), conv,
            mc.reshape(d, n, M_HEADS, M_HEADDIM, M_HEADDIM), mn, mm.reshape(d, n, M_HEADS), pool)


def kernel(x_prompt, x_sample, state_ssd, state_conv, state_mlstm_C, state_mlstm_n, state_mlstm_m, state_pool, c_prompt, c_sample, w_ada_mix, b_ada_mix, w_in, conv_w, conv_b, ssd_A_log, ssd_dt_bias, ssd_D, ssd_norm_w, mlstm_gate_b, mlstm_norm_w, pool_w, pool_scale, gate_b, w_br_ssd, w_br_mlstm, w_br_pool, w_out, ln1_g, ln1_b, w_ada_ffn, b_ada_ffn, w_rt_group, b_rt_group, w_rt_expert, b_rt_expert, w_e_gate, w_e_up, w_e_down, ln2_g, ln2_b):
    params = dict(w_in=w_in, conv_w=conv_w, conv_b=conv_b, ssd_A_log=ssd_A_log, ssd_dt_bias=ssd_dt_bias,
                  ssd_D=ssd_D, ssd_norm_w=ssd_norm_w, mlstm_gate_b=mlstm_gate_b, mlstm_norm_w=mlstm_norm_w,
                  pool_w=pool_w, pool_scale=pool_scale, gate_b=gate_b, w_br_ssd=w_br_ssd,
                  w_br_mlstm=w_br_mlstm, w_br_pool=w_br_pool, w_out=w_out, ln1_g=ln1_g, ln1_b=ln1_b,
                  w_rt_group=w_rt_group, b_rt_group=b_rt_group, w_rt_expert=w_rt_expert,
                  b_rt_expert=b_rt_expert, w_e_gate=w_e_gate, w_e_up=w_e_up, w_e_down=w_e_down,
                  ln2_g=ln2_g, ln2_b=ln2_b)
    bp, seq, _ = x_prompt.shape
    bs, dec_seq, _ = x_sample.shape
    assert seq % CHUNK == 0 and 1 <= dec_seq <= SAMPLE_ROWS and bs % SAMPLE_SEQ_BLOCK == 0

    c_all = jnp.concatenate([c_prompt, c_sample], axis=0)
    mods_mix = _ada_call(c_all, w_ada_mix, b_ada_mix[:, None, :])
    mods_ffn = _ada_call(c_all, w_ada_ffn, b_ada_ffn[:, None, :])

    sample_in = _flat_states(state_ssd, state_conv, state_mlstm_C, state_mlstm_n, state_mlstm_m, state_pool)
    xp = x_prompt
    xs = jnp.pad(x_sample, ((0, 0), (0, SAMPLE_ROWS - dec_seq), (0, 0)))
    p_states, s_states = None, None
    for l in range(DEPTH):
        lw = _layer_weights(l, params)
        xp, p_states = _prompt_mixer_call(l, xp, mods_mix[l, :bp, None, :], p_states, lw)
        xp = _moe_call(xp, mods_ffn[l, :bp, None, :], lw)

        mods_s = mods_mix[l, bp:, None, :]
        proj = _proj_call(xs, mods_s, lw["w_cat"])
        ys, s_states = _sample_mixer_call(l, proj, sample_in, s_states, lw, tv=dec_seq)
        xs = _merge_call(xs, mods_s, ys, lw)
        xs = _moe_call(xs, mods_ffn[l, bp:, None, :], lw)
    return (xp, xs[:, :dec_seq]) + _unflat_states(p_states) + _unflat_states(s_states)
```

```python
import functools

import jax
import jax.numpy as jnp
from jax import lax
from jax.experimental import pallas as pl
from jax.experimental.pallas import tpu as pltpu

F32 = jnp.float32
BF16 = jnp.bfloat16

D_MODEL = 1024
DEPTH = 4
SSD_HEADS = 16
SSD_HEADDIM = 64
SSD_GROUPS = 2
SSD_REP = SSD_HEADS // SSD_GROUPS
SSD_STATE = 128
SSD_CONV = 4
SSD_CONV_DIM = D_MODEL + 2 * SSD_GROUPS * SSD_STATE
CHUNK = 128
M_HEADS = 4
M_HEADDIM = D_MODEL // M_HEADS
POOL_WINDOWS = (2, 4, 8, 16)
POOL_GW = D_MODEL // len(POOL_WINDOWS)
POOL_BUF = max(POOL_WINDOWS) - 1
N_EGROUPS = 4
EXP_PER_GROUP = 4
N_EXPERTS = N_EGROUPS * EXP_PER_GROUP
D_FF_E = D_MODEL // 4
ALPHA = (2 * DEPTH) ** 0.25
LN_EPS = 1e-5
RMS_EPS = 1e-6

SUBLANES = 8
LANES = 128
VMEM_LIMIT_BYTES = 56 * 1024 * 1024

OFF_Z = 0
OFF_XBC = OFF_Z + D_MODEL
OFF_SMALL = OFF_XBC + SSD_CONV_DIM
OFF_Q = OFF_SMALL + LANES
OFF_K = OFF_Q + D_MODEL
OFF_V = OFF_K + D_MODEL
OFF_O = OFF_V + D_MODEL
OFF_UP = OFF_O + D_MODEL
N_PROJ = OFF_UP + D_MODEL
SM_A, SM_DT, SM_F, SM_I, SM_END = 0, SSD_HEADS, 2 * SSD_HEADS, 2 * SSD_HEADS + M_HEADS, 2 * SSD_HEADS + 2 * M_HEADS
RT_G, RT_E = 0, 16
NEG_BIG = -1e30

ROW_TILE = 256
SAMPLE_ROWS = 8
SAMPLE_SEQ_BLOCK = 4

STATE_SHAPES = ((SSD_HEADS * SSD_HEADDIM, SSD_STATE), (SSD_CONV - 1, SSD_CONV_DIM),
                (M_HEADS * M_HEADDIM, M_HEADDIM), (M_HEADS, M_HEADDIM), (1, M_HEADS), (POOL_BUF, D_MODEL))
N_STATES = len(STATE_SHAPES)
N_MIXER_SCRATCH = 5
POOL_TOP = SUBLANES + POOL_BUF + 1


def _dot(a, b):
    return jnp.dot(a.astype(BF16), b.astype(BF16), preferred_element_type=F32)


def _dot_nt(a, b):
    return lax.dot_general(a.astype(BF16), b.astype(BF16), (((1,), (1,)), ((), ())),
                           preferred_element_type=F32)


def _dot_tn(a, b):
    return lax.dot_general(a.astype(BF16), b.astype(BF16), (((0,), (0,)), ((), ())),
                           preferred_element_type=F32)


def _split3(v):
    hi = v.astype(BF16)
    r1 = v - hi.astype(F32)
    mid = r1.astype(BF16)
    lo = (r1 - mid.astype(F32)).astype(BF16)
    return hi, mid, lo


def _select_dot(sel, v):
    n = v.shape[1]
    out = jnp.dot(sel.astype(BF16), jnp.concatenate(_split3(v), axis=1), preferred_element_type=F32)
    return out[:, 0:n] + out[:, n:2 * n] + out[:, 2 * n:3 * n]


def _select_dot_nt(sel, v):
    n = v.shape[0]
    out = lax.dot_general(sel.astype(BF16), jnp.concatenate(_split3(v), axis=0), (((1,), (1,)), ((), ())),
                          preferred_element_type=F32)
    return out[:, 0:n] + out[:, n:2 * n] + out[:, 2 * n:3 * n]


def _sigmoid(x):
    return lax.logistic(x)


def _silu(x):
    return x * _sigmoid(x)


def _softplus(x):
    return jnp.maximum(x, 0.0) + jnp.log1p(jnp.exp(-jnp.abs(x)))


def _layer_norm(x, g, b):
    mu = jnp.mean(x, axis=-1, keepdims=True)
    xc = x - mu
    var = jnp.mean(xc * xc, axis=-1, keepdims=True)
    return xc * lax.rsqrt(var + LN_EPS) * g + b


def _compiler_params(n_grid):
    return pltpu.CompilerParams(dimension_semantics=("arbitrary",) * n_grid,
                                vmem_limit_bytes=VMEM_LIMIT_BYTES)


def _vmem_full():
    return pl.BlockSpec(memory_space=pltpu.VMEM)


def _const_spec(shape):
    return pl.BlockSpec(shape, lambda *_: (0,) * len(shape))


def _packed_weight_spec(half):
    return pl.BlockSpec((D_MODEL, N_PROJ), lambda *_: (0, half))


def _ada_kernel(c_ref, w_ref, b_ref, o_ref):
    o_ref[...] = _dot(c_ref[...], w_ref[...]) + b_ref[...]


def _ada_call(c_all, w, b):
    n = c_all.shape[0]
    return pl.pallas_call(
        _ada_kernel,
        grid=(DEPTH, 3),
        in_specs=[pl.BlockSpec((n, D_MODEL), lambda l, j: (0, 0)),
                  pl.BlockSpec((None, D_MODEL, D_MODEL), lambda l, j: (l, 0, j)),
                  pl.BlockSpec((None, 1, D_MODEL), lambda l, j: (l, 0, j))],
        out_specs=pl.BlockSpec((None, n, D_MODEL), lambda l, j: (l, 0, j)),
        out_shape=jax.ShapeDtypeStruct((DEPTH, n, 3 * D_MODEL), F32),
        compiler_params=_compiler_params(2),
        name="ada_mod",
    )(c_all, w, b)


def _modulate(x_ref, mod_ref):
    x = x_ref[...]
    shift = mod_ref[:, :, 0:D_MODEL]
    scale = mod_ref[:, :, D_MODEL:2 * D_MODEL]
    u = x * (1.0 + scale) + shift
    return u.reshape(x.shape[0] * x.shape[1], D_MODEL)


def _proj_kernel(x_ref, mod_ref, w_ref, o_ref):
    sb, rb, _ = x_ref.shape
    u = _modulate(x_ref, mod_ref).astype(BF16)
    col = 0
    while col < N_PROJ:
        width = min(D_MODEL, N_PROJ - col)
        o_ref[:, :, col:col + width] = _dot(u, w_ref[:, col:col + width]).reshape(sb, rb, width)
        col += width


def _row_blocks(n_seq, rows, tile=ROW_TILE):
    if rows >= tile:
        return 1, tile
    return tile // rows, rows


def _proj_call(x3, mods, wcat):
    n_seq, rows, _ = x3.shape
    sb, rb = _row_blocks(n_seq, rows)
    return pl.pallas_call(
        _proj_kernel,
        grid=(n_seq // sb, rows // rb),
        in_specs=[pl.BlockSpec((sb, rb, D_MODEL), lambda i, j: (i, j, 0)),
                  pl.BlockSpec((sb, 1, 3 * D_MODEL), lambda i, j: (i, 0, 0)),
                  _packed_weight_spec(0)],
        out_specs=pl.BlockSpec((sb, rb, N_PROJ), lambda i, j: (i, j, 0)),
        out_shape=jax.ShapeDtypeStruct((n_seq, rows, N_PROJ), F32),
        compiler_params=_compiler_params(2),
        name="in_proj",
    )(x3, mods, wcat)


def _mixer_chunk(proj, st_in, st_out, par, scr, *, n_seq, rows, tv, pos0, after_stage=lambda name: None):
    L = n_seq * rows
    rows_log2 = rows.bit_length() - 1
    assert rows == 1 << rows_log2
    ssd_i, conv_i, mc_i, mn_i, mm_i, pool_i = st_in
    ssd_o, conv_o, mc_o, mn_o, mm_o, pool_o = st_out
    convw_ref, convb_ref, hp_ref, snw_ref, mnw_ref, poolw_ref, pscale_ref = par
    xext_ref, pext_ref, psa_ref, psb_ref, yacc_ref = scr
    seq_rows = [slice(s * rows, (s + 1) * rows) for s in range(n_seq)]

    def per_seq(fn):
        parts = [fn(s, seq_rows[s]) for s in range(n_seq)]
        return parts[0] if n_seq == 1 else jnp.concatenate(parts, axis=0)

    row_l = lax.broadcasted_iota(jnp.int32, (L, L), 0)
    col_l = lax.broadcasted_iota(jnp.int32, (L, L), 1)
    causal = row_l >= col_l
    if n_seq > 1:
        same_seq = (row_l >> rows_log2) == (col_l >> rows_log2)
        causal = causal & same_seq

    lane = lax.broadcasted_iota(jnp.int32, (L, LANES), 1)
    pre = proj(OFF_SMALL, LANES) + hp_ref[0:1, :]
    sp = _softplus(pre)
    a_row = -jnp.exp(hp_ref[1:2, :])
    pmat = jnp.where(lane < SM_DT, sp * a_row,
                     jnp.where(lane < SM_F, sp,
                               jnp.where(lane < SM_I, -_softplus(-pre),
                                         jnp.where(lane < SM_END, pre, 0.0))))
    if tv < rows:
        row = lax.broadcasted_iota(jnp.int32, (L, LANES), 0)
        pad = jnp.where(lane < SM_I, 0.0, jnp.where(lane < SM_END, NEG_BIG, 0.0))
        pmat = jnp.where((row & (rows - 1)) < tv, pmat, pad)
    eye = (lax.broadcasted_iota(jnp.int32, (LANES, LANES), 0)
           == lax.broadcasted_iota(jnp.int32, (LANES, LANES), 1))
    cum = _select_dot(causal, pmat)
    pmat_t = _select_dot_nt(eye, pmat)
    cum_t = _select_dot_nt(eye, cum)
    if n_seq > 1:
        tot = _select_dot(same_seq, pmat)
    else:
        tot = cum[L - 1:L, :]

    xext_ref[:, SUBLANES - (SSD_CONV - 1):SUBLANES, :] = conv_i[...]
    xext_ref[:, SUBLANES:SUBLANES + rows, :] = proj(OFF_XBC, SSD_CONV_DIM).reshape(n_seq, rows, SSD_CONV_DIM)
    acc = convb_ref[...]
    for k in range(SSD_CONV):
        start = SUBLANES - (SSD_CONV - 1) + k
        acc = acc + xext_ref[:, start:start + rows, :].reshape(L, SSD_CONV_DIM) * convw_ref[k:k + 1, :]
    conv_o[...] = xext_ref[:, SUBLANES + tv - (SSD_CONV - 1):SUBLANES + tv, :]
    after_stage("conv")
    xbc = _silu(acc)
    xs = xbc[:, 0:D_MODEL]
    d_row = hp_ref[2:3, :]
    gw = SSD_REP * SSD_HEADDIM
    for g in range(SSD_GROUPS):
        grp = slice(g * gw, (g + 1) * gw)
        bm = xbc[:, D_MODEL + g * SSD_STATE:D_MODEL + (g + 1) * SSD_STATE]
        cm = xbc[:, D_MODEL + (SSD_GROUPS + g) * SSD_STATE:D_MODEL + (SSD_GROUPS + g + 1) * SSD_STATE]
        cb = _dot_nt(cm, bm)
        y_state = per_seq(lambda s, rs: _dot_nt(cm[rs], ssd_i[s, grp, :]))
        xw_parts = []
        first_head = lax.broadcasted_iota(jnp.int32, (L, LANES), 1) < SSD_HEADDIM
        for r in range(0, SSD_REP, 2):
            heads = (g * SSD_REP + r, g * SSD_REP + r + 1)
            ps = slice(heads[0] * SSD_HEADDIM, heads[0] * SSD_HEADDIM + LANES)
            per_head = lambda fn: jnp.where(first_head, fn(heads[0]), fn(heads[1]))
            wmats = []
            for h in heads:
                cum_c = cum[:, SM_A + h:SM_A + h + 1]
                cum_r = cum_t[SM_A + h:SM_A + h + 1, :]
                dt_r = pmat_t[SM_DT + h:SM_DT + h + 1, :]
                seg = jnp.where(causal, cum_c - cum_r, -jnp.inf)
                wmats.append((cb * jnp.exp(seg) * dt_r).astype(BF16))
            x_p = xs[:, ps]
            x_diag = jnp.concatenate([jnp.where(first_head, x_p, 0.0), jnp.where(first_head, 0.0, x_p)], axis=0)
            decay_in = per_head(lambda h: jnp.exp(cum[:, SM_A + h:SM_A + h + 1]))
            y = (_dot(jnp.concatenate(wmats, axis=1), x_diag)
                 + y_state[:, r * SSD_HEADDIM:r * SSD_HEADDIM + LANES] * decay_in)
            yacc_ref[:, ps] = y + per_head(lambda h: d_row[:, h:h + 1]) * x_p
            xw_parts.append(x_p * per_head(
                lambda h: jnp.exp(tot[:, SM_A + h:SM_A + h + 1] - cum[:, SM_A + h:SM_A + h + 1])
                * pmat[:, SM_DT + h:SM_DT + h + 1]))
            after_stage("ssd_head")
            after_stage("ssd_head")
        xw = jnp.concatenate(xw_parts, axis=-1)
        for s in range(n_seq):
            upd = _dot_tn(xw[seq_rows[s]], bm[seq_rows[s]])
            t0 = s * rows if n_seq > 1 else 0
            for r in range(SSD_REP):
                h = g * SSD_REP + r
                hs = slice(h * SSD_HEADDIM, (h + 1) * SSD_HEADDIM)
                decay = jnp.exp(tot[t0:t0 + 1, SM_A + h:SM_A + h + 1])
                ssd_o[s, hs, :] = decay * ssd_i[s, hs, :] + upd[r * SSD_HEADDIM:(r + 1) * SSD_HEADDIM, :]
    yz = yacc_ref[...] * _silu(proj(OFF_Z, D_MODEL))
    y_ssd = yz * lax.rsqrt(jnp.mean(yz * yz, axis=-1, keepdims=True) + RMS_EPS) * snw_ref[...]
    after_stage("ssd")

    ym_parts = []
    for h in range(M_HEADS):
        hs = slice(h * M_HEADDIM, (h + 1) * M_HEADDIM)
        q_h = proj(OFF_Q + h * M_HEADDIM, M_HEADDIM)
        k_h = proj(OFF_K + h * M_HEADDIM, M_HEADDIM) * (M_HEADDIM ** -0.5)
        v_h = proj(OFF_V + h * M_HEADDIM, M_HEADDIM)
        o_h = proj(OFF_O + h * M_HEADDIM, M_HEADDIM)
        b_c = cum[:, SM_F + h:SM_F + h + 1]
        b_r = cum_t[SM_F + h:SM_F + h + 1, :]
        i_c = pmat[:, SM_I + h:SM_I + h + 1]
        i_r = pmat_t[SM_I + h:SM_I + h + 1, :]
        last_b = tot[:, SM_F + h:SM_F + h + 1]
        m_prev = per_seq(lambda s, rs: jnp.broadcast_to(mm_i[s, :, h:h + 1], (rows, 1)))
        dmat = jnp.where(causal, b_c - b_r + i_r, -jnp.inf)
        m_st = b_c + m_prev
        m = jnp.maximum(m_st, jnp.max(dmat, axis=-1, keepdims=True))
        wts = jnp.exp(dmat - m) * _dot_nt(q_h, k_h)
        ws = jnp.exp(m_st - m)
        cq = per_seq(lambda s, rs: _dot_nt(q_h[rs], mc_i[s, hs, :]))
        nq = per_seq(lambda s, rs: jnp.sum(q_h[rs] * mn_i[s, h:h + 1, :], axis=-1, keepdims=True))
        num = _dot(wts, v_h) + ws * cq
        den = jnp.sum(wts, axis=-1, keepdims=True) + ws * nq
        hc = num / jnp.maximum(jnp.abs(den), jnp.exp(-m))
        m_new = per_seq(lambda s, rs: jnp.broadcast_to(m[rs.stop - 1:rs.stop, :], (rows, 1)))
        wsrc = jnp.exp(last_b - b_c + i_c - m_new)
        wprev = jnp.exp(last_b + m_prev - m_new)
        vw = v_h * wsrc
        kw = k_h * wsrc
        for s in range(n_seq):
            rs = seq_rows[s]
            wp = wprev[rs.start:rs.start + 1, :]
            mc_o[s, hs, :] = wp * mc_i[s, hs, :] + _dot_tn(vw[rs], k_h[rs])
            mn_o[s, h:h + 1, :] = wp * mn_i[s, h:h + 1, :] + jnp.sum(kw[rs], axis=0, keepdims=True)
            mm_o[s, :, h:h + 1] = m_new[rs.start:rs.start + 1, :]
        mu = jnp.mean(hc, axis=-1, keepdims=True)
        hd = hc - mu
        var = jnp.mean(hd * hd, axis=-1, keepdims=True)
        ym_parts.append(hd * lax.rsqrt(var + LN_EPS) * mnw_ref[:, hs] * _sigmoid(o_h))
        after_stage("mlstm_head")
    y_m = jnp.concatenate(ym_parts, axis=-1)

    assert POOL_WINDOWS == tuple(2 << g for g in range(len(POOL_WINDOWS)))
    first_row = POOL_TOP - POOL_BUF
    pext_ref[:, 0:first_row, :] = jnp.zeros((n_seq, first_row, D_MODEL), F32)
    for ref in (psa_ref, psb_ref):
        ref[:, 0:SUBLANES, :] = jnp.zeros((n_seq, SUBLANES, D_MODEL), F32)
    pext_ref[:, first_row:POOL_TOP, :] = pool_i[...]
    up = proj(OFF_UP, D_MODEL)
    pext_ref[:, POOL_TOP:POOL_TOP + rows, :] = up.reshape(n_seq, rows, D_MODEL)
    span = POOL_TOP - SUBLANES + rows
    src, dst, shift = pext_ref, psa_ref, 1
    for g in range(len(POOL_WINDOWS) - 1):
        lanes = slice(g * POOL_GW, D_MODEL)
        dst[:, SUBLANES:SUBLANES + span, lanes] = (src[:, SUBLANES:SUBLANES + span, lanes]
                                                   + src[:, SUBLANES - shift:SUBLANES - shift + span, lanes])
        src, dst, shift = dst, (psb_ref if dst is psa_ref else psa_ref), 2 * shift
    last = slice((len(POOL_WINDOWS) - 1) * POOL_GW, D_MODEL)
    widest = src[:, POOL_TOP:POOL_TOP + rows, last] + src[:, POOL_TOP - shift:POOL_TOP - shift + rows, last]
    pos = (lax.broadcasted_iota(jnp.int32, (L, 1), 0) & (rows - 1)) + pos0
    yp_parts = []
    for g, w in enumerate(POOL_WINDOWS):
        gs = slice(g * POOL_GW, (g + 1) * POOL_GW)
        if g == len(POOL_WINDOWS) - 1:
            wsum = widest.reshape(L, POOL_GW)
        else:
            wsum = (psa_ref if g % 2 == 0 else psb_ref)[:, POOL_TOP:POOL_TOP + rows, gs].reshape(L, POOL_GW)
        cnt = jnp.minimum(pos, w).astype(F32)
        dlt = wsum / cnt - up[:, gs]
        yp_parts.append(_dot(dlt, poolw_ref[g]) * pscale_ref[:, gs])
        after_stage("pool")
    pool_o[...] = pext_ref[:, first_row + tv:POOL_TOP + tv, :]
    return y_ssd, y_m, jnp.concatenate(yp_parts, axis=-1)


def _merge_rows(gates, ys, w_ref, col0=0):
    merged = None
    for i, (gate, y) in enumerate(zip(gates, ys)):
        lo = OFF_BR - col0 + i * D_MODEL
        term = gate * _dot(y, w_ref[:, lo:lo + D_MODEL])
        merged = term if merged is None else merged + term
    return _dot(merged, w_ref[:, OFF_OUT - col0:OFF_OUT - col0 + D_MODEL])


N_MIXER_PARAMS = 7


def _mixer_param_specs():
    return [_const_spec((SSD_CONV, SSD_CONV_DIM)), _const_spec((1, SSD_CONV_DIM)),
            _const_spec((SUBLANES, LANES)), _const_spec((1, D_MODEL)), _const_spec((1, D_MODEL)),
            _const_spec((len(POOL_WINDOWS), POOL_GW, POOL_GW)), _const_spec((1, D_MODEL))]


def _mixer_param_args(lw):
    return (lw["conv_w"], lw["conv_b"], lw["head_params"], lw["ssd_norm_w"], lw["mlstm_norm_w"],
            lw["pool_w"], lw["pool_scale"])


def _mixer_scratch(n_seq, rows):
    pool_rows = pltpu.VMEM((n_seq, POOL_TOP + rows, D_MODEL), F32)
    return [pltpu.VMEM((n_seq, SUBLANES + rows, SSD_CONV_DIM), F32), pool_rows, pool_rows, pool_rows,
            pltpu.VMEM((n_seq * rows, D_MODEL), F32)]


def _stacked_state_shapes(n_seq):
    return [jax.ShapeDtypeStruct((DEPTH, n_seq) + shp, F32) for shp in STATE_SHAPES]


def _alias_args(prev_states, n_inputs_before, n_outputs_before):
    if prev_states is None:
        return [], [], {}
    specs = [pl.BlockSpec(memory_space=pl.ANY)] * N_STATES
    aliases = {n_inputs_before + k: n_outputs_before + k for k in range(N_STATES)}
    return list(prev_states), specs, aliases


N_PROJ_ALL = N_PROJ + 3 * D_MODEL
OFF_GL = N_PROJ
OFF_BR = N_PROJ_ALL
OFF_OUT = OFF_BR + 3 * D_MODEL
assert OFF_GL == N_PROJ and OFF_OUT + D_MODEL - OFF_GL <= N_PROJ


def _column_pieces(lo, hi, width):
    return [(off, min(width, hi - off)) for off in range(lo, hi, width)]


def _prompt_mixer_kernel(x_ref, mod_ref, xn_ref, modn_ref, wcat_ref, gb_ref, lng_ref, lnb_ref, *rest, n_alias):
    par = rest[:N_MIXER_PARAMS]
    rest = rest[N_MIXER_PARAMS + n_alias:]
    o_ref = rest[0]
    states = rest[1:1 + N_STATES]
    scr = rest[1 + N_STATES:1 + N_STATES + N_MIXER_SCRATCH]
    pscr_ref = rest[1 + N_STATES + N_MIXER_SCRATCH]
    L = x_ref.shape[0]
    b = pl.program_id(0)
    c = pl.program_id(1)

    def modulated(xr, mr):
        return (xr[...] * (1.0 + mr[:, D_MODEL:2 * D_MODEL]) + mr[:, 0:D_MODEL]).astype(BF16)

    def project_into_scratch(ub, piece):
        off, width = piece
        pscr_ref[:, off:off + width] = _dot(ub, wcat_ref[:, off:off + width])

    @pl.when(c == 0)
    def _fresh_prompt_states():
        for ref in states:
            ref[...] = jnp.zeros(ref.shape, ref.dtype)

    @pl.when(jnp.logical_and(b == 0, c == 0))
    def _first_chunk_projections():
        ub0 = modulated(x_ref, mod_ref)
        for piece in _column_pieces(0, N_PROJ, D_MODEL) + _column_pieces(OFF_GL, N_PROJ_ALL, D_MODEL):
            project_into_scratch(ub0, piece)

    ub_next = modulated(xn_ref, modn_ref)
    ready = []
    released_by = {
        "conv": [_column_pieces(OFF_XBC, OFF_Q, 896)],
        "ssd": [_column_pieces(OFF_Z, OFF_XBC, D_MODEL)],
        "mlstm_head": [[(off + h * M_HEADDIM, M_HEADDIM) for off in (OFF_Q, OFF_K, OFF_V, OFF_O)]
                       for h in range(M_HEADS)],
        "pool": [_column_pieces(OFF_UP, N_PROJ, D_MODEL)],
    }
    pieces_per_call = {"conv": 2, "ssd_head": 1}

    def after_stage(name):
        if released_by.get(name):
            ready.extend(released_by[name].pop(0))
        for _ in range(min(len(ready), pieces_per_call.get(name, len(ready)))):
            project_into_scratch(ub_next, ready.pop(0))

    gates = [_sigmoid(pscr_ref[:, OFF_GL + i * D_MODEL:OFF_GL + (i + 1) * D_MODEL]
                      + gb_ref[:, i * D_MODEL:(i + 1) * D_MODEL]) for i in range(3)]
    ready.extend(_column_pieces(OFF_GL, N_PROJ_ALL, 768))
    proj = lambda off, width: pscr_ref[:, off:off + width]
    ys = _mixer_chunk(proj, states, states, par, scr, n_seq=1, rows=L, tv=L, pos0=c * L + 1,
                      after_stage=after_stage)
    after_stage("rest")
    assert not ready and not any(released_by.values())
    mix = _merge_rows(gates, ys, wcat_ref)
    o_ref[...] = _layer_norm(ALPHA * x_ref[...] + (1.0 + mod_ref[:, 2 * D_MODEL:3 * D_MODEL]) * mix,
                             lng_ref[...], lnb_ref[...])


def _prompt_mixer_call(l, x3, mods, prev_states, lw):
    n_seq, rows, _ = x3.shape
    L = CHUNK
    nc = rows // L

    def next_chunk(b, c):
        flat = jnp.minimum(b * nc + c + 1, n_seq * nc - 1)
        return flat // nc, flat % nc

    row_spec = pl.BlockSpec((None, L, D_MODEL), lambda b, c: (b, c, 0))
    mod_spec = pl.BlockSpec((None, 1, 3 * D_MODEL), lambda b, c: (b, 0, 0))
    next_row_spec = pl.BlockSpec((None, L, D_MODEL), lambda b, c: next_chunk(b, c) + (0,))
    next_mod_spec = pl.BlockSpec((None, 1, 3 * D_MODEL), lambda b, c: (next_chunk(b, c)[0], 0, 0))
    vec_spec = _const_spec((1, D_MODEL))
    state_specs = [pl.BlockSpec((None, 1) + shp, lambda b, c: (l, b, 0, 0)) for shp in STATE_SHAPES]
    in_specs = [row_spec, mod_spec, next_row_spec, next_mod_spec,
                _vmem_full(), _const_spec((1, 3 * D_MODEL)), vec_spec, vec_spec]
    in_specs += _mixer_param_specs()
    alias_in, alias_specs, aliases = _alias_args(prev_states, len(in_specs), 1)
    outs = pl.pallas_call(
        functools.partial(_prompt_mixer_kernel, n_alias=len(alias_in)),
        grid=(n_seq, nc),
        in_specs=in_specs + alias_specs,
        out_specs=[row_spec] + state_specs,
        out_shape=[jax.ShapeDtypeStruct(x3.shape, F32)] + _stacked_state_shapes(n_seq),
        scratch_shapes=_mixer_scratch(1, L) + [pltpu.VMEM((L, N_PROJ_ALL), F32)],
        input_output_aliases=aliases,
        compiler_params=_compiler_params(2),
        name="prompt_mixers",
    )(x3, mods, x3, mods, lw["w_cat"], lw["gate_b"], lw["ln1_g"], lw["ln1_b"], *_mixer_param_args(lw), *alias_in)
    return outs[0], outs[1:]


def _sample_mixer_kernel(proj_ref, *rest, tv, n_alias):
    st_in = rest[:N_STATES]
    par = rest[N_STATES:N_STATES + N_MIXER_PARAMS]
    rest = rest[N_STATES + N_MIXER_PARAMS + n_alias:]
    y_refs = rest[:3]
    st_out = rest[3:3 + N_STATES]
    scr = rest[3 + N_STATES:]
    n_blk, rows, _ = proj_ref.shape
    proj = lambda off, width: proj_ref[:, :, off:off + width].reshape(n_blk * rows, width)
    ys = _mixer_chunk(proj, st_in, st_out, par, scr, n_seq=n_blk, rows=rows, tv=tv, pos0=1 + POOL_BUF)
    for y_ref, y in zip(y_refs, ys):
        y_ref[...] = y.reshape(n_blk, rows, D_MODEL)


def _sample_mixer_call(l, proj, states_in, prev_states, lw, *, tv):
    n_seq, rows, _ = proj.shape
    nb = SAMPLE_SEQ_BLOCK
    state_specs = [pl.BlockSpec((None, nb) + shp, lambda i: (l, i, 0, 0)) for shp in STATE_SHAPES]
    y_spec = pl.BlockSpec((nb, rows, D_MODEL), lambda i: (i, 0, 0))
    y_shape = jax.ShapeDtypeStruct((n_seq, rows, D_MODEL), F32)
    in_specs = [pl.BlockSpec((nb, rows, N_PROJ), lambda i: (i, 0, 0))] + state_specs + _mixer_param_specs()
    alias_in, alias_specs, aliases = _alias_args(prev_states, len(in_specs), 3)
    outs = pl.pallas_call(
        functools.partial(_sample_mixer_kernel, tv=tv, n_alias=len(alias_in)),
        grid=(n_seq // nb,),
        in_specs=in_specs + alias_specs,
        out_specs=[y_spec, y_spec, y_spec] + state_specs,
        out_shape=[y_shape, y_shape, y_shape] + _stacked_state_shapes(n_seq),
        scratch_shapes=_mixer_scratch(nb, rows),
        input_output_aliases=aliases,
        compiler_params=_compiler_params(1),
        name="sample_mixers",
    )(proj, *states_in, *_mixer_param_args(lw), *alias_in)
    return outs[:3], outs[3:]


def _merge_kernel(x_ref, mod_ref, yssd_ref, ym_ref, ypool_ref, wgl_ref, gb_ref,
                  lng_ref, lnb_ref, o_ref):
    sb, rb, _ = x_ref.shape
    n = sb * rb
    ub = _modulate(x_ref, mod_ref).astype(BF16)
    ys = [r[...].reshape(n, D_MODEL) for r in (yssd_ref, ym_ref, ypool_ref)]
    gates = [_sigmoid(_dot(ub, wgl_ref[:, i * D_MODEL:(i + 1) * D_MODEL])
                      + gb_ref[:, i * D_MODEL:(i + 1) * D_MODEL]) for i in range(3)]
    mix = _merge_rows(gates, ys, wgl_ref, col0=OFF_GL).reshape(sb, rb, D_MODEL)
    gate_a = mod_ref[:, :, 2 * D_MODEL:3 * D_MODEL]
    o_ref[...] = _layer_norm(ALPHA * x_ref[...] + (1.0 + gate_a) * mix, lng_ref[...], lnb_ref[...])


def _merge_call(x3, mods, ys, lw):
    n_seq, rows, _ = x3.shape
    sb, rb = _row_blocks(n_seq, rows)
    row_spec = pl.BlockSpec((sb, rb, D_MODEL), lambda i, j: (i, j, 0))
    vec_spec = _const_spec((1, D_MODEL))
    return pl.pallas_call(
        _merge_kernel,
        grid=(n_seq // sb, rows // rb),
        in_specs=[row_spec, pl.BlockSpec((sb, 1, 3 * D_MODEL), lambda i, j: (i, 0, 0)),
                  row_spec, row_spec, row_spec,
                  _packed_weight_spec(1), _const_spec((1, 3 * D_MODEL)), vec_spec, vec_spec],
        out_specs=row_spec,
        out_shape=jax.ShapeDtypeStruct(x3.shape, F32),
        compiler_params=_compiler_params(2),
        name="merge_norm",
    )(x3, mods, ys[0], ys[1], ys[2], lw["w_cat"], lw["gate_b"], lw["ln1_g"], lw["ln1_b"])


MOE_TILE = 512
MOE_BLOCK = 128
MOE_SLOT_BLOCKS = MOE_TILE // MOE_BLOCK + N_EGROUPS - 1
MOE_SLOTS = MOE_SLOT_BLOCKS * MOE_BLOCK
SIDE_POS, SIDE_HI, SIDE_MID, SIDE_LO = 0, 8, 16, 24


def _pad_rows(v, n):
    return jnp.concatenate([v, jnp.zeros((n - v.shape[0], v.shape[1]), v.dtype)], axis=0)


def _route_t(logits_t):
    t = logits_t.shape[1]
    row_g = lax.broadcasted_iota(jnp.int32, (SUBLANES, t), 0)
    lg = jnp.where(row_g < N_EGROUPS, logits_t[RT_G:RT_G + SUBLANES, :], -jnp.inf)
    g_max = jnp.max(lg, axis=0, keepdims=True)
    g_idx = jnp.min(jnp.where(lg == g_max, row_g, SUBLANES), axis=0, keepdims=True)
    g_prob = 1.0 / jnp.sum(jnp.exp(lg - g_max), axis=0, keepdims=True)
    row_e = lax.broadcasted_iota(jnp.int32, (N_EXPERTS, t), 0)
    le = jnp.where((row_e >> 2) == g_idx, logits_t[RT_E:RT_E + N_EXPERTS, :], -jnp.inf)
    v1 = jnp.max(le, axis=0, keepdims=True)
    i1 = jnp.min(jnp.where(le == v1, row_e, N_EXPERTS), axis=0, keepdims=True)
    le2 = jnp.where(row_e == i1, -jnp.inf, le)
    v2 = jnp.max(le2, axis=0, keepdims=True)
    i2 = jnp.min(jnp.where(le2 == v2, row_e, N_EXPERTS), axis=0, keepdims=True)
    e2 = jnp.exp(v2 - v1)
    p1 = g_prob / (1.0 + e2)
    p2 = g_prob * e2 / (1.0 + e2)
    wts = jnp.where(row_e == i1, p1, 0.0) + jnp.where(row_e == i2, p2, 0.0)
    w4 = None
    for g in range(N_EGROUPS):
        part = jnp.where(g_idx == g, wts[g * EXP_PER_GROUP:(g + 1) * EXP_PER_GROUP, :], 0.0)
        w4 = part if w4 is None else w4 + part
    return g_idx, w4


def _moe_kernel(x_ref, mod_ref, wrt_ref, brt_ref, wg_ref, wu_ref, wd_ref, lng_ref, lnb_ref, o_ref,
                sx_ref, sw_ref, so_ref):
    sb, rb, _ = x_ref.shape
    t = sb * rb
    u = _modulate(x_ref, mod_ref)
    u_hi = u.astype(BF16)
    u_lo = (u - u_hi.astype(F32)).astype(BF16)
    w = wrt_ref[...]
    w_hi = w.astype(BF16)
    w_lo = (w - w_hi.astype(F32)).astype(BF16)
    logits_t = _dot_nt(w_hi, u_hi) + _dot_nt(w_hi, u_lo) + _dot_nt(w_lo, u_hi) + brt_ref[...]
    g_idx, w4 = _route_t(logits_t)

    row_g = lax.broadcasted_iota(jnp.int32, (SUBLANES, t), 0)
    onehot_t = (row_g == g_idx).astype(F32)
    before = (lax.broadcasted_iota(jnp.int32, (t, t), 0) < lax.broadcasted_iota(jnp.int32, (t, t), 1))
    rank = _dot(onehot_t, before.astype(F32))
    cnt = jnp.sum(onehot_t, axis=1, keepdims=True)
    nblk = jnp.floor((cnt + (MOE_BLOCK - 1)) * (1.0 / MOE_BLOCK))
    sub = lax.broadcasted_iota(jnp.int32, (SUBLANES, 1), 0)
    first = jnp.zeros((SUBLANES, 1), F32)
    running = jnp.zeros((1, 1), F32)
    for g in range(1, N_EGROUPS):
        running = running + nblk[g - 1:g, :]
        first = first + jnp.where(sub == g, running, 0.0)
    pos_t = jnp.sum(onehot_t * (first * MOE_BLOCK + rank), axis=0, keepdims=True)
    blk_lane = lax.broadcasted_iota(jnp.int32, (SUBLANES, LANES), 1).astype(F32)
    in_blk = (blk_lane >= first) & (blk_lane < first + nblk)
    blk_group = jnp.sum(jnp.where(in_blk, sub.astype(F32), 0.0), axis=0, keepdims=True)

    w_hi4, w_mid4, w_lo4 = [p.astype(F32) for p in _split3(w4)]
    side_t = jnp.concatenate([_pad_rows(pos_t, SUBLANES), _pad_rows(w_hi4, SUBLANES),
                              _pad_rows(w_mid4, SUBLANES), _pad_rows(w_lo4, SUBLANES),
                              jnp.zeros((LANES - 4 * SUBLANES, t), F32)], axis=0)
    side = side_t.T

    slot_r = lax.broadcasted_iota(jnp.int32, (MOE_SLOTS, t), 0)
    send = (slot_r == pos_t.astype(jnp.int32)).astype(BF16)
    payload = jnp.concatenate([u_hi, side.astype(BF16)], axis=-1)
    sorted_rows = _dot(send, payload)
    sx_ref[...] = sorted_rows[:, 0:D_MODEL].astype(BF16)
    sw_ref[...] = sorted_rows[:, D_MODEL:D_MODEL + LANES]

    n_used = jnp.sum(nblk[:, 0:1]).astype(jnp.int32)
    for i in range(MOE_SLOT_BLOCKS):
        rows = slice(i * MOE_BLOCK, (i + 1) * MOE_BLOCK)

        def expert_block(i=i, rows=rows):
            g = blk_group[0, i].astype(jnp.int32)
            xb = sx_ref[rows, :]
            ws = sw_ref[rows, :]
            w_blk = (ws[:, SIDE_HI:SIDE_HI + EXP_PER_GROUP] + ws[:, SIDE_MID:SIDE_MID + EXP_PER_GROUP]
                     + ws[:, SIDE_LO:SIDE_LO + EXP_PER_GROUP])
            hid = []
            for r in range(EXP_PER_GROUP):
                e = g * EXP_PER_GROUP + r
                hid.append(_silu(_dot(xb, wg_ref[e])) * _dot(xb, wu_ref[e]) * w_blk[:, r:r + 1])
            so_ref[rows, :] = _dot(jnp.concatenate(hid, axis=-1), wd_ref[g]).astype(BF16)

        def empty_block(rows=rows):
            so_ref[rows, :] = jnp.zeros((MOE_BLOCK, D_MODEL), BF16)

        if i < MOE_TILE // MOE_BLOCK:
            expert_block()
        else:
            pl.when(i < n_used)(expert_block)
            pl.when(i >= n_used)(empty_block)

    slot_c = lax.broadcasted_iota(jnp.int32, (t, MOE_SLOTS), 1)
    fetch = (slot_c == side[:, SIDE_POS:SIDE_POS + 1].astype(jnp.int32)).astype(BF16)
    ffn = _dot(fetch, so_ref[...])
    gate_f = mod_ref[:, :, 2 * D_MODEL:3 * D_MODEL]
    o_ref[...] = _layer_norm(ALPHA * x_ref[...] + (1.0 + gate_f) * ffn.reshape(sb, rb, D_MODEL),
                             lng_ref[...], lnb_ref[...])


def _moe_call(x3, mods, lw):
    n_seq, rows, _ = x3.shape
    sb, rb = _row_blocks(n_seq, rows, MOE_TILE)
    row_spec = pl.BlockSpec((sb, rb, D_MODEL), lambda i, j: (i, j, 0))
    vec_spec = _const_spec((1, D_MODEL))
    return pl.pallas_call(
        _moe_kernel,
        grid=(n_seq // sb, rows // rb),
        in_specs=[row_spec, pl.BlockSpec((sb, 1, 3 * D_MODEL), lambda i, j: (i, 0, 0)),
                  _vmem_full(), _const_spec((LANES, 1)),
                  _vmem_full(), _vmem_full(), _vmem_full(), vec_spec, vec_spec],
        out_specs=row_spec,
        out_shape=jax.ShapeDtypeStruct(x3.shape, F32),
        scratch_shapes=[pltpu.VMEM((MOE_SLOTS, D_MODEL), BF16), pltpu.VMEM((MOE_SLOTS, LANES), F32),
                        pltpu.VMEM((MOE_SLOTS, D_MODEL), BF16)],
        compiler_params=_compiler_params(2),
        name="moe_norm",
    )(x3, mods, lw["w_rt_t"], lw["b_rt_col"], lw["w_e_gate"], lw["w_e_up"], lw["w_e_down"],
      lw["ln2_g"], lw["ln2_b"])


def _split_w_in(w):
    sizes = (D_MODEL, SSD_CONV_DIM, SSD_HEADS, D_MODEL, D_MODEL, D_MODEL, M_HEADS, M_HEADS, D_MODEL,
             D_MODEL, 3 * D_MODEL)
    out, off = [], 0
    for s in sizes:
        out.append(w[:, off:off + s])
        off += s
    return out


def _pad_lanes(v, width=LANES):
    return jnp.pad(v, ((0, 0), (0, width - v.shape[-1])))


def _layer_weights(l, p):
    wz, wxbc, wdt, wq, wk, wv, wi, wf, wo, wup, wgl = _split_w_in(p["w_in"][l])
    w_small = _pad_lanes(jnp.concatenate([wdt, wdt, wf, wi], axis=1))
    gate_b = p["mlstm_gate_b"][l]
    bias_row = jnp.concatenate([p["ssd_dt_bias"][l], p["ssd_dt_bias"][l], gate_b[M_HEADS:], gate_b[:M_HEADS]])
    head_params = jnp.concatenate([
        _pad_lanes(bias_row[None]), _pad_lanes(p["ssd_A_log"][l][None]), _pad_lanes(p["ssd_D"][l][None]),
        jnp.zeros((SUBLANES - 3, LANES), F32)], axis=0)
    w_rt = jnp.concatenate([_pad_lanes(p["w_rt_group"][l], RT_E), _pad_lanes(p["w_rt_expert"][l], LANES - RT_E)],
                           axis=1)
    b_rt = jnp.concatenate([_pad_lanes(p["b_rt_group"][l][None], RT_E),
                            _pad_lanes(p["b_rt_expert"][l][None], LANES - RT_E)], axis=1)
    row = lambda v: v[None]
    return dict(
        w_cat=jnp.concatenate([wz, wxbc, w_small, wq, wk, wv, wo, wup, wgl, p["w_br_ssd"][l], p["w_br_mlstm"][l],
                               p["w_br_pool"][l], p["w_out"][l]], axis=1).astype(BF16),
        conv_w=p["conv_w"][l], conv_b=row(p["conv_b"][l]), head_params=head_params,
        ssd_norm_w=row(p["ssd_norm_w"][l]), mlstm_norm_w=row(p["mlstm_norm_w"][l]),
        pool_w=p["pool_w"][l].astype(BF16), pool_scale=row(p["pool_scale"][l]),
        gate_b=row(p["gate_b"][l]),
        ln1_g=row(p["ln1_g"][l]), ln1_b=row(p["ln1_b"][l]),
        w_rt_t=w_rt.T, b_rt_col=b_rt.T,
        w_e_gate=p["w_e_gate"][l].astype(BF16), w_e_up=p["w_e_up"][l].astype(BF16),
        w_e_down=p["w_e_down"][l].reshape(N_EGROUPS, EXP_PER_GROUP * D_FF_E, D_MODEL).astype(BF16),
        ln2_g=row(p["ln2_g"][l]), ln2_b=row(p["ln2_b"][l]),
    )


def _flat_states(ssd, conv, mc, mn, mm, pool):
    return tuple(a.reshape(a.shape[:2] + shp) for a, shp in zip((ssd, conv, mc, mn, mm, pool), STATE_SHAPES))


def _unflat_states(states):
    ssd, conv, mc, mn, mm, pool = states
    d, n = ssd.shape[:2]
    return (ssd.reshape(d, n, SSD_HEADS, SSD_HEADDIM, SSD_STATE), conv,
            mc.reshape(d, n, M_HEADS, M_HEADDIM, M_HEADDIM), mn, mm.reshape(d, n, M_HEADS), pool)


def kernel(x_prompt, x_sample, state_ssd, state_conv, state_mlstm_C, state_mlstm_n, state_mlstm_m, state_pool, c_prompt, c_sample, w_ada_mix, b_ada_mix, w_in, conv_w, conv_b, ssd_A_log, ssd_dt_bias, ssd_D, ssd_norm_w, mlstm_gate_b, mlstm_norm_w, pool_w, pool_scale, gate_b, w_br_ssd, w_br_mlstm, w_br_pool, w_out, ln1_g, ln1_b, w_ada_ffn, b_ada_ffn, w_rt_group, b_rt_group, w_rt_expert, b_rt_expert, w_e_gate, w_e_up, w_e_down, ln2_g, ln2_b):
    params = dict(w_in=w_in, conv_w=conv_w, conv_b=conv_b, ssd_A_log=ssd_A_log, ssd_dt_bias=ssd_dt_bias,
                  ssd_D=ssd_D, ssd_norm_w=ssd_norm_w, mlstm_gate_b=mlstm_gate_b, mlstm_norm_w=mlstm_norm_w,
                  pool_w=pool_w, pool_scale=pool_scale, gate_b=gate_b, w_br_ssd=w_br_ssd,
                  w_br_mlstm=w_br_mlstm, w_br_pool=w_br_pool, w_out=w_out, ln1_g=ln1_g, ln1_b=ln1_b,
                  w_rt_group=w_rt_group, b_rt_group=b_rt_group, w_rt_expert=w_rt_expert,
                  b_rt_expert=b_rt_expert, w_e_gate=w_e_gate, w_e_up=w_e_up, w_e_down=w_e_down,
                  ln2_g=ln2_g, ln2_b=ln2_b)
    bp, seq, _ = x_prompt.shape
    bs, dec_seq, _ = x_sample.shape
    assert seq % CHUNK == 0 and 1 <= dec_seq <= SAMPLE_ROWS and bs % SAMPLE_SEQ_BLOCK == 0

    c_all = jnp.concatenate([c_prompt, c_sample], axis=0)
    mods_mix = _ada_call(c_all, w_ada_mix, b_ada_mix[:, None, :])
    mods_ffn = _ada_call(c_all, w_ada_ffn, b_ada_ffn[:, None, :])

    sample_in = _flat_states(state_ssd, state_conv, state_mlstm_C, state_mlstm_n, state_mlstm_m, state_pool)
    xp = x_prompt
    xs = jnp.pad(x_sample, ((0, 0), (0, SAMPLE_ROWS - dec_seq), (0, 0)))
    p_states, s_states = None, None
    for l in range(DEPTH):
        lw = _layer_weights(l, params)
        xp, p_states = _prompt_mixer_call(l, xp, mods_mix[l, :bp, None, :], p_states, lw)
        xp = _moe_call(xp, mods_ffn[l, :bp, None, :], lw)

        mods_s = mods_mix[l, bp:, None, :]
        proj = _proj_call(xs, mods_s, lw["w_cat"])
        ys, s_states = _sample_mixer_call(l, proj, sample_in, s_states, lw, tv=dec_seq)
        xs = _merge_call(xs, mods_s, ys, lw)
        xs = _moe_call(xs, mods_ffn[l, bp:, None, :], lw)
    return (xp, xs[:, :dec_seq]) + _unflat_states(p_states) + _unflat_states(s_states)
```

```python
import functools

import jax
import jax.numpy as jnp
from jax import lax
from jax.experimental import pallas as pl
from jax.experimental.pallas import tpu as pltpu

F32 = jnp.float32
BF16 = jnp.bfloat16

D_MODEL = 1024
DEPTH = 4
SSD_HEADS = 16
SSD_HEADDIM = 64
SSD_GROUPS = 2
SSD_REP = SSD_HEADS // SSD_GROUPS
SSD_STATE = 128
SSD_CONV = 4
SSD_CONV_DIM = D_MODEL + 2 * SSD_GROUPS * SSD_STATE
CHUNK = 128
M_HEADS = 4
M_HEADDIM = D_MODEL // M_HEADS
POOL_WINDOWS = (2, 4, 8, 16)
POOL_GW = D_MODEL // len(POOL_WINDOWS)
POOL_BUF = max(POOL_WINDOWS) - 1
N_EGROUPS = 4
EXP_PER_GROUP = 4
N_EXPERTS = N_EGROUPS * EXP_PER_GROUP
D_FF_E = D_MODEL // 4
ALPHA = (2 * DEPTH) ** 0.25
LN_EPS = 1e-5
RMS_EPS = 1e-6

SUBLANES = 8
LANES = 128
VMEM_LIMIT_BYTES = 56 * 1024 * 1024

OFF_Z = 0
OFF_XBC = OFF_Z + D_MODEL
OFF_SMALL = OFF_XBC + SSD_CONV_DIM
OFF_Q = OFF_SMALL + LANES
OFF_K = OFF_Q + D_MODEL
OFF_V = OFF_K + D_MODEL
OFF_O = OFF_V + D_MODEL
OFF_UP = OFF_O + D_MODEL
N_PROJ = OFF_UP + D_MODEL
SM_A, SM_DT, SM_F, SM_I, SM_END = 0, SSD_HEADS, 2 * SSD_HEADS, 2 * SSD_HEADS + M_HEADS, 2 * SSD_HEADS + 2 * M_HEADS
RT_G, RT_E = 0, 16
NEG_BIG = -1e30

ROW_TILE = 256
SAMPLE_ROWS = 8
SAMPLE_SEQ_BLOCK = 4

STATE_SHAPES = ((SSD_HEADS * SSD_HEADDIM, SSD_STATE), (SSD_CONV - 1, SSD_CONV_DIM),
                (M_HEADS * M_HEADDIM, M_HEADDIM), (M_HEADS, M_HEADDIM), (1, M_HEADS), (POOL_BUF, D_MODEL))
N_STATES = len(STATE_SHAPES)
N_MIXER_SCRATCH = 5
POOL_TOP = SUBLANES + POOL_BUF + 1


def _dot(a, b):
    return jnp.dot(a.astype(BF16), b.astype(BF16), preferred_element_type=F32)


def _dot_nt(a, b):
    return lax.dot_general(a.astype(BF16), b.astype(BF16), (((1,), (1,)), ((), ())),
                           preferred_element_type=F32)


def _dot_tn(a, b):
    return lax.dot_general(a.astype(BF16), b.astype(BF16), (((0,), (0,)), ((), ())),
                           preferred_element_type=F32)


def _split3(v):
    hi = v.astype(BF16)
    r1 = v - hi.astype(F32)
    mid = r1.astype(BF16)
    lo = (r1 - mid.astype(F32)).astype(BF16)
    return hi, mid, lo


def _select_dot(sel, v):
    n = v.shape[1]
    out = jnp.dot(sel.astype(BF16), jnp.concatenate(_split3(v), axis=1), preferred_element_type=F32)
    return out[:, 0:n] + out[:, n:2 * n] + out[:, 2 * n:3 * n]


def _select_dot_nt(sel, v):
    n = v.shape[0]
    out = lax.dot_general(sel.astype(BF16), jnp.concatenate(_split3(v), axis=0), (((1,), (1,)), ((), ())),
                          preferred_element_type=F32)
    return out[:, 0:n] + out[:, n:2 * n] + out[:, 2 * n:3 * n]


def _sigmoid(x):
    return lax.logistic(x)


def _silu(x):
    return x * _sigmoid(x)


def _softplus(x):
    return jnp.maximum(x, 0.0) + jnp.log1p(jnp.exp(-jnp.abs(x)))


def _layer_norm(x, g, b):
    mu = jnp.mean(x, axis=-1, keepdims=True)
    xc = x - mu
    var = jnp.mean(xc * xc, axis=-1, keepdims=True)
    return xc * lax.rsqrt(var + LN_EPS) * g + b


def _compiler_params(n_grid):
    return pltpu.CompilerParams(dimension_semantics=("arbitrary",) * n_grid,
                                vmem_limit_bytes=VMEM_LIMIT_BYTES)


def _vmem_full():
    return pl.BlockSpec(memory_space=pltpu.VMEM)


def _const_spec(shape):
    return pl.BlockSpec(shape, lambda *_: (0,) * len(shape))


def _packed_weight_spec(layer, half):
    return pl.BlockSpec((None, D_MODEL, N_PROJ), lambda *_: (layer, 0, half))


def _ada_kernel(c_ref, w_ref, b_ref, o_ref):
    o_ref[...] = _dot(c_ref[...], w_ref[...]) + b_ref[...]


def _ada_call(c_all, w, b):
    n = c_all.shape[0]
    return pl.pallas_call(
        _ada_kernel,
        grid=(DEPTH, 3),
        in_specs=[pl.BlockSpec((n, D_MODEL), lambda l, j: (0, 0)),
                  pl.BlockSpec((None, D_MODEL, D_MODEL), lambda l, j: (l, 0, j)),
                  pl.BlockSpec((None, 1, D_MODEL), lambda l, j: (l, 0, j))],
        out_specs=pl.BlockSpec((None, n, D_MODEL), lambda l, j: (l, 0, j)),
        out_shape=jax.ShapeDtypeStruct((DEPTH, n, 3 * D_MODEL), F32),
        compiler_params=_compiler_params(2),
        name="ada_mod",
    )(c_all, w, b)


def _modulate(x_ref, mod_ref):
    x = x_ref[...]
    shift = mod_ref[:, :, 0:D_MODEL]
    scale = mod_ref[:, :, D_MODEL:2 * D_MODEL]
    u = x * (1.0 + scale) + shift
    return u.reshape(x.shape[0] * x.shape[1], D_MODEL)


def _proj_kernel(x_ref, mod_ref, w_ref, o_ref):
    sb, rb, _ = x_ref.shape
    u = _modulate(x_ref, mod_ref).astype(BF16)
    col = 0
    while col < N_PROJ:
        width = min(D_MODEL, N_PROJ - col)
        o_ref[:, :, col:col + width] = _dot(u, w_ref[:, col:col + width]).reshape(sb, rb, width)
        col += width


def _row_blocks(n_seq, rows, tile=ROW_TILE):
    if rows >= tile:
        return 1, tile
    return tile // rows, rows


def _proj_call(layer, x3, mods, wcat):
    n_seq, rows, _ = x3.shape
    sb, rb = _row_blocks(n_seq, rows)
    return pl.pallas_call(
        _proj_kernel,
        grid=(n_seq // sb, rows // rb),
        in_specs=[pl.BlockSpec((sb, rb, D_MODEL), lambda i, j: (i, j, 0)),
                  pl.BlockSpec((sb, 1, 3 * D_MODEL), lambda i, j: (i, 0, 0)),
                  _packed_weight_spec(layer, 0)],
        out_specs=pl.BlockSpec((sb, rb, N_PROJ), lambda i, j: (i, j, 0)),
        out_shape=jax.ShapeDtypeStruct((n_seq, rows, N_PROJ), F32),
        compiler_params=_compiler_params(2),
        name="in_proj",
    )(x3, mods, wcat)


def _mixer_chunk(proj, st_in, st_out, par, scr, *, n_seq, rows, tv, pos0, after_stage=lambda name: None):
    L = n_seq * rows
    rows_log2 = rows.bit_length() - 1
    assert rows == 1 << rows_log2
    ssd_i, conv_i, mc_i, mn_i, mm_i, pool_i = st_in
    ssd_o, conv_o, mc_o, mn_o, mm_o, pool_o = st_out
    convw_ref, convb_ref, hp_ref, snw_ref, mnw_ref, poolw_ref, pscale_ref = par
    xext_ref, pext_ref, psa_ref, psb_ref, yacc_ref = scr
    seq_rows = [slice(s * rows, (s + 1) * rows) for s in range(n_seq)]

    def per_seq(fn):
        parts = [fn(s, seq_rows[s]) for s in range(n_seq)]
        return parts[0] if n_seq == 1 else jnp.concatenate(parts, axis=0)

    row_l = lax.broadcasted_iota(jnp.int32, (L, L), 0)
    col_l = lax.broadcasted_iota(jnp.int32, (L, L), 1)
    causal = row_l >= col_l
    if n_seq > 1:
        same_seq = (row_l >> rows_log2) == (col_l >> rows_log2)
        causal = causal & same_seq

    lane = lax.broadcasted_iota(jnp.int32, (L, LANES), 1)
    pre = proj(OFF_SMALL, LANES) + hp_ref[0:1, :]
    sp = _softplus(pre)
    a_row = -jnp.exp(hp_ref[1:2, :])
    pmat = jnp.where(lane < SM_DT, sp * a_row,
                     jnp.where(lane < SM_F, sp,
                               jnp.where(lane < SM_I, -_softplus(-pre),
                                         jnp.where(lane < SM_END, pre, 0.0))))
    if tv < rows:
        row = lax.broadcasted_iota(jnp.int32, (L, LANES), 0)
        pad = jnp.where(lane < SM_I, 0.0, jnp.where(lane < SM_END, NEG_BIG, 0.0))
        pmat = jnp.where((row & (rows - 1)) < tv, pmat, pad)
    eye = (lax.broadcasted_iota(jnp.int32, (LANES, LANES), 0)
           == lax.broadcasted_iota(jnp.int32, (LANES, LANES), 1))
    cum = _select_dot(causal, pmat)
    pmat_t = _select_dot_nt(eye, pmat)
    cum_t = _select_dot_nt(eye, cum)
    if n_seq > 1:
        tot = _select_dot(same_seq, pmat)
    else:
        tot = cum[L - 1:L, :]

    xext_ref[:, SUBLANES - (SSD_CONV - 1):SUBLANES, :] = conv_i[...]
    xext_ref[:, SUBLANES:SUBLANES + rows, :] = proj(OFF_XBC, SSD_CONV_DIM).reshape(n_seq, rows, SSD_CONV_DIM)
    acc = convb_ref[...]
    for k in range(SSD_CONV):
        start = SUBLANES - (SSD_CONV - 1) + k
        acc = acc + xext_ref[:, start:start + rows, :].reshape(L, SSD_CONV_DIM) * convw_ref[k:k + 1, :]
    conv_o[...] = xext_ref[:, SUBLANES + tv - (SSD_CONV - 1):SUBLANES + tv, :]
    after_stage("conv")
    xbc = _silu(acc)
    xs = xbc[:, 0:D_MODEL]
    d_row = hp_ref[2:3, :]
    gw = SSD_REP * SSD_HEADDIM
    for g in range(SSD_GROUPS):
        grp = slice(g * gw, (g + 1) * gw)
        bm = xbc[:, D_MODEL + g * SSD_STATE:D_MODEL + (g + 1) * SSD_STATE]
        cm = xbc[:, D_MODEL + (SSD_GROUPS + g) * SSD_STATE:D_MODEL + (SSD_GROUPS + g + 1) * SSD_STATE]
        cb = _dot_nt(cm, bm)
        y_state = per_seq(lambda s, rs: _dot_nt(cm[rs], ssd_i[s, grp, :]))
        xw_parts = []
        first_head = lax.broadcasted_iota(jnp.int32, (L, LANES), 1) < SSD_HEADDIM
        for r in range(0, SSD_REP, 2):
            heads = (g * SSD_REP + r, g * SSD_REP + r + 1)
            ps = slice(heads[0] * SSD_HEADDIM, heads[0] * SSD_HEADDIM + LANES)
            per_head = lambda fn: jnp.where(first_head, fn(heads[0]), fn(heads[1]))
            wmats = []
            for h in heads:
                cum_c = cum[:, SM_A + h:SM_A + h + 1]
                cum_r = cum_t[SM_A + h:SM_A + h + 1, :]
                dt_r = pmat_t[SM_DT + h:SM_DT + h + 1, :]
                seg = jnp.where(causal, cum_c - cum_r, -jnp.inf)
                wmats.append((cb * jnp.exp(seg) * dt_r).astype(BF16))
            x_p = xs[:, ps]
            x_diag = jnp.concatenate([jnp.where(first_head, x_p, 0.0), jnp.where(first_head, 0.0, x_p)], axis=0)
            decay_in = per_head(lambda h: jnp.exp(cum[:, SM_A + h:SM_A + h + 1]))
            y = (_dot(jnp.concatenate(wmats, axis=1), x_diag)
                 + y_state[:, r * SSD_HEADDIM:r * SSD_HEADDIM + LANES] * decay_in)
            yacc_ref[:, ps] = y + per_head(lambda h: d_row[:, h:h + 1]) * x_p
            xw_parts.append(x_p * per_head(
                lambda h: jnp.exp(tot[:, SM_A + h:SM_A + h + 1] - cum[:, SM_A + h:SM_A + h + 1])
                * pmat[:, SM_DT + h:SM_DT + h + 1]))
            after_stage("ssd_head")
            after_stage("ssd_head")
        xw = jnp.concatenate(xw_parts, axis=-1)
        for s in range(n_seq):
            upd = _dot_tn(xw[seq_rows[s]], bm[seq_rows[s]])
            t0 = s * rows if n_seq > 1 else 0
            for r in range(SSD_REP):
                h = g * SSD_REP + r
                hs = slice(h * SSD_HEADDIM, (h + 1) * SSD_HEADDIM)
                decay = jnp.exp(tot[t0:t0 + 1, SM_A + h:SM_A + h + 1])
                ssd_o[s, hs, :] = decay * ssd_i[s, hs, :] + upd[r * SSD_HEADDIM:(r + 1) * SSD_HEADDIM, :]
    yz = yacc_ref[...] * _silu(proj(OFF_Z, D_MODEL))
    y_ssd = yz * lax.rsqrt(jnp.mean(yz * yz, axis=-1, keepdims=True) + RMS_EPS) * snw_ref[...]
    after_stage("ssd")

    ym_parts = []
    for h in range(M_HEADS):
        hs = slice(h * M_HEADDIM, (h + 1) * M_HEADDIM)
        q_h = proj(OFF_Q + h * M_HEADDIM, M_HEADDIM)
        k_h = proj(OFF_K + h * M_HEADDIM, M_HEADDIM) * (M_HEADDIM ** -0.5)
        v_h = proj(OFF_V + h * M_HEADDIM, M_HEADDIM)
        o_h = proj(OFF_O + h * M_HEADDIM, M_HEADDIM)
        b_c = cum[:, SM_F + h:SM_F + h + 1]
        b_r = cum_t[SM_F + h:SM_F + h + 1, :]
        i_c = pmat[:, SM_I + h:SM_I + h + 1]
        i_r = pmat_t[SM_I + h:SM_I + h + 1, :]
        last_b = tot[:, SM_F + h:SM_F + h + 1]
        m_prev = per_seq(lambda s, rs: jnp.broadcast_to(mm_i[s, :, h:h + 1], (rows, 1)))
        dmat = jnp.where(causal, b_c - b_r + i_r, -jnp.inf)
        m_st = b_c + m_prev
        m = jnp.maximum(m_st, jnp.max(dmat, axis=-1, keepdims=True))
        wts = jnp.exp(dmat - m) * _dot_nt(q_h, k_h)
        ws = jnp.exp(m_st - m)
        cq = per_seq(lambda s, rs: _dot_nt(q_h[rs], mc_i[s, hs, :]))
        nq = per_seq(lambda s, rs: jnp.sum(q_h[rs] * mn_i[s, h:h + 1, :], axis=-1, keepdims=True))
        num = _dot(wts, v_h) + ws * cq
        den = jnp.sum(wts, axis=-1, keepdims=True) + ws * nq
        hc = num / jnp.maximum(jnp.abs(den), jnp.exp(-m))
        m_new = per_seq(lambda s, rs: jnp.broadcast_to(m[rs.stop - 1:rs.stop, :], (rows, 1)))
        wsrc = jnp.exp(last_b - b_c + i_c - m_new)
        wprev = jnp.exp(last_b + m_prev - m_new)
        vw = v_h * wsrc
        kw = k_h * wsrc
        for s in range(n_seq):
            rs = seq_rows[s]
            wp = wprev[rs.start:rs.start + 1, :]
            mc_o[s, hs, :] = wp * mc_i[s, hs, :] + _dot_tn(vw[rs], k_h[rs])
            mn_o[s, h:h + 1, :] = wp * mn_i[s, h:h + 1, :] + jnp.sum(kw[rs], axis=0, keepdims=True)
            mm_o[s, :, h:h + 1] = m_new[rs.start:rs.start + 1, :]
        mu = jnp.mean(hc, axis=-1, keepdims=True)
        hd = hc - mu
        var = jnp.mean(hd * hd, axis=-1, keepdims=True)
        ym_parts.append(hd * lax.rsqrt(var + LN_EPS) * mnw_ref[:, hs] * _sigmoid(o_h))
        after_stage("mlstm_head")
    y_m = jnp.concatenate(ym_parts, axis=-1)

    assert POOL_WINDOWS == tuple(2 << g for g in range(len(POOL_WINDOWS)))
    first_row = POOL_TOP - POOL_BUF
    pext_ref[:, 0:first_row, :] = jnp.zeros((n_seq, first_row, D_MODEL), F32)
    for ref in (psa_ref, psb_ref):
        ref[:, 0:SUBLANES, :] = jnp.zeros((n_seq, SUBLANES, D_MODEL), F32)
    pext_ref[:, first_row:POOL_TOP, :] = pool_i[...]
    up = proj(OFF_UP, D_MODEL)
    pext_ref[:, POOL_TOP:POOL_TOP + rows, :] = up.reshape(n_seq, rows, D_MODEL)
    span = POOL_TOP - SUBLANES + rows
    src, dst, shift = pext_ref, psa_ref, 1
    for g in range(len(POOL_WINDOWS) - 1):
        lanes = slice(g * POOL_GW, D_MODEL)
        dst[:, SUBLANES:SUBLANES + span, lanes] = (src[:, SUBLANES:SUBLANES + span, lanes]
                                                   + src[:, SUBLANES - shift:SUBLANES - shift + span, lanes])
        src, dst, shift = dst, (psb_ref if dst is psa_ref else psa_ref), 2 * shift
    last = slice((len(POOL_WINDOWS) - 1) * POOL_GW, D_MODEL)
    widest = src[:, POOL_TOP:POOL_TOP + rows, last] + src[:, POOL_TOP - shift:POOL_TOP - shift + rows, last]
    pos = (lax.broadcasted_iota(jnp.int32, (L, 1), 0) & (rows - 1)) + pos0
    yp_parts = []
    for g, w in enumerate(POOL_WINDOWS):
        gs = slice(g * POOL_GW, (g + 1) * POOL_GW)
        if g == len(POOL_WINDOWS) - 1:
            wsum = widest.reshape(L, POOL_GW)
        else:
            wsum = (psa_ref if g % 2 == 0 else psb_ref)[:, POOL_TOP:POOL_TOP + rows, gs].reshape(L, POOL_GW)
        cnt = jnp.minimum(pos, w).astype(F32)
        dlt = wsum / cnt - up[:, gs]
        yp_parts.append(_dot(dlt, poolw_ref[g]) * pscale_ref[:, gs])
        after_stage("pool")
    pool_o[...] = pext_ref[:, first_row + tv:POOL_TOP + tv, :]
    return y_ssd, y_m, jnp.concatenate(yp_parts, axis=-1)


def _merge_rows(gates, ys, w_ref, col0=0):
    merged = None
    for i, (gate, y) in enumerate(zip(gates, ys)):
        lo = OFF_BR - col0 + i * D_MODEL
        term = gate * _dot(y, w_ref[:, lo:lo + D_MODEL])
        merged = term if merged is None else merged + term
    return _dot(merged, w_ref[:, OFF_OUT - col0:OFF_OUT - col0 + D_MODEL])


N_MIXER_PARAMS = 7


def _mixer_param_specs():
    return [_const_spec((SSD_CONV, SSD_CONV_DIM)), _const_spec((1, SSD_CONV_DIM)),
            _const_spec((SUBLANES, LANES)), _const_spec((1, D_MODEL)), _const_spec((1, D_MODEL)),
            _const_spec((len(POOL_WINDOWS), POOL_GW, POOL_GW)), _const_spec((1, D_MODEL))]


def _mixer_param_args(lw):
    return (lw["conv_w"], lw["conv_b"], lw["head_params"], lw["ssd_norm_w"], lw["mlstm_norm_w"],
            lw["pool_w"], lw["pool_scale"])


def _mixer_scratch(n_seq, rows):
    pool_rows = pltpu.VMEM((n_seq, POOL_TOP + rows, D_MODEL), F32)
    return [pltpu.VMEM((n_seq, SUBLANES + rows, SSD_CONV_DIM), F32), pool_rows, pool_rows, pool_rows,
            pltpu.VMEM((n_seq * rows, D_MODEL), F32)]


def _stacked_state_shapes(n_seq):
    return [jax.ShapeDtypeStruct((DEPTH, n_seq) + shp, F32) for shp in STATE_SHAPES]


def _alias_args(prev_states, n_inputs_before, n_outputs_before):
    if prev_states is None:
        return [], [], {}
    specs = [pl.BlockSpec(memory_space=pl.ANY)] * N_STATES
    aliases = {n_inputs_before + k: n_outputs_before + k for k in range(N_STATES)}
    return list(prev_states), specs, aliases


N_PROJ_ALL = N_PROJ + 3 * D_MODEL
OFF_GL = N_PROJ
OFF_BR = N_PROJ_ALL
OFF_OUT = OFF_BR + 3 * D_MODEL
assert OFF_GL == N_PROJ and OFF_OUT + D_MODEL - OFF_GL <= N_PROJ


def _column_pieces(lo, hi, width):
    return [(off, min(width, hi - off)) for off in range(lo, hi, width)]


def _prompt_mixer_kernel(x_ref, mod_ref, xn_ref, modn_ref, wcat_ref, gb_ref, lng_ref, lnb_ref, *rest, n_alias):
    par = rest[:N_MIXER_PARAMS]
    rest = rest[N_MIXER_PARAMS + n_alias:]
    o_ref = rest[0]
    states = rest[1:1 + N_STATES]
    scr = rest[1 + N_STATES:1 + N_STATES + N_MIXER_SCRATCH]
    pscr_ref = rest[1 + N_STATES + N_MIXER_SCRATCH]
    L = x_ref.shape[0]
    b = pl.program_id(0)
    c = pl.program_id(1)

    def modulated(xr, mr):
        return (xr[...] * (1.0 + mr[:, D_MODEL:2 * D_MODEL]) + mr[:, 0:D_MODEL]).astype(BF16)

    def project_into_scratch(ub, piece):
        off, width = piece
        pscr_ref[:, off:off + width] = _dot(ub, wcat_ref[:, off:off + width])

    @pl.when(c == 0)
    def _fresh_prompt_states():
        for ref in states:
            ref[...] = jnp.zeros(ref.shape, ref.dtype)

    @pl.when(jnp.logical_and(b == 0, c == 0))
    def _first_chunk_projections():
        ub0 = modulated(x_ref, mod_ref)
        for piece in _column_pieces(0, N_PROJ, D_MODEL) + _column_pieces(OFF_GL, N_PROJ_ALL, D_MODEL):
            project_into_scratch(ub0, piece)

    ub_next = modulated(xn_ref, modn_ref)
    ready = []
    released_by = {
        "conv": [_column_pieces(OFF_XBC, OFF_Q, 896)],
        "ssd": [_column_pieces(OFF_Z, OFF_XBC, D_MODEL)],
        "mlstm_head": [[(off + h * M_HEADDIM, M_HEADDIM) for off in (OFF_Q, OFF_K, OFF_V, OFF_O)]
                       for h in range(M_HEADS)],
        "pool": [_column_pieces(OFF_UP, N_PROJ, D_MODEL)],
    }
    pieces_per_call = {"conv": 2, "ssd_head": 1}

    def after_stage(name):
        if released_by.get(name):
            ready.extend(released_by[name].pop(0))
        for _ in range(min(len(ready), pieces_per_call.get(name, len(ready)))):
            project_into_scratch(ub_next, ready.pop(0))

    gates = [_sigmoid(pscr_ref[:, OFF_GL + i * D_MODEL:OFF_GL + (i + 1) * D_MODEL]
                      + gb_ref[:, i * D_MODEL:(i + 1) * D_MODEL]) for i in range(3)]
    ready.extend(_column_pieces(OFF_GL, N_PROJ_ALL, 768))
    proj = lambda off, width: pscr_ref[:, off:off + width]
    ys = _mixer_chunk(proj, states, states, par, scr, n_seq=1, rows=L, tv=L, pos0=c * L + 1,
                      after_stage=after_stage)
    after_stage("rest")
    assert not ready and not any(released_by.values())
    mix = _merge_rows(gates, ys, wcat_ref)
    o_ref[...] = _layer_norm(ALPHA * x_ref[...] + (1.0 + mod_ref[:, 2 * D_MODEL:3 * D_MODEL]) * mix,
                             lng_ref[...], lnb_ref[...])


def _prompt_mixer_call(l, x3, mods, prev_states, lw):
    n_seq, rows, _ = x3.shape
    L = CHUNK
    nc = rows // L

    def next_chunk(b, c):
        flat = jnp.minimum(b * nc + c + 1, n_seq * nc - 1)
        return flat // nc, flat % nc

    row_spec = pl.BlockSpec((None, L, D_MODEL), lambda b, c: (b, c, 0))
    mod_spec = pl.BlockSpec((None, 1, 3 * D_MODEL), lambda b, c: (b, 0, 0))
    next_row_spec = pl.BlockSpec((None, L, D_MODEL), lambda b, c: next_chunk(b, c) + (0,))
    next_mod_spec = pl.BlockSpec((None, 1, 3 * D_MODEL), lambda b, c: (next_chunk(b, c)[0], 0, 0))
    vec_spec = _const_spec((1, D_MODEL))
    state_specs = [pl.BlockSpec((None, 1) + shp, lambda b, c: (l, b, 0, 0)) for shp in STATE_SHAPES]
    in_specs = [row_spec, mod_spec, next_row_spec, next_mod_spec,
                pl.BlockSpec((None, D_MODEL, OFF_OUT + D_MODEL), lambda b, c: (l, 0, 0), pipeline_mode=pl.Buffered(1)),
                _const_spec((1, 3 * D_MODEL)), vec_spec, vec_spec]
    in_specs += _mixer_param_specs()
    alias_in, alias_specs, aliases = _alias_args(prev_states, len(in_specs), 1)
    outs = pl.pallas_call(
        functools.partial(_prompt_mixer_kernel, n_alias=len(alias_in)),
        grid=(n_seq, nc),
        in_specs=in_specs + alias_specs,
        out_specs=[row_spec] + state_specs,
        out_shape=[jax.ShapeDtypeStruct(x3.shape, F32)] + _stacked_state_shapes(n_seq),
        scratch_shapes=_mixer_scratch(1, L) + [pltpu.VMEM((L, N_PROJ_ALL), F32)],
        input_output_aliases=aliases,
        compiler_params=_compiler_params(2),
        name="prompt_mixers",
    )(x3, mods, x3, mods, lw["w_cat"], lw["gate_b"], lw["ln1_g"], lw["ln1_b"], *_mixer_param_args(lw), *alias_in)
    return outs[0], outs[1:]


def _sample_mixer_kernel(proj_ref, *rest, tv, n_alias):
    st_in = rest[:N_STATES]
    par = rest[N_STATES:N_STATES + N_MIXER_PARAMS]
    rest = rest[N_STATES + N_MIXER_PARAMS + n_alias:]
    y_refs = rest[:3]
    st_out = rest[3:3 + N_STATES]
    scr = rest[3 + N_STATES:]
    n_blk, rows, _ = proj_ref.shape
    proj = lambda off, width: proj_ref[:, :, off:off + width].reshape(n_blk * rows, width)
    ys = _mixer_chunk(proj, st_in, st_out, par, scr, n_seq=n_blk, rows=rows, tv=tv, pos0=1 + POOL_BUF)
    for y_ref, y in zip(y_refs, ys):
        y_ref[...] = y.reshape(n_blk, rows, D_MODEL)


def _sample_mixer_call(l, proj, states_in, prev_states, lw, *, tv):
    n_seq, rows, _ = proj.shape
    nb = SAMPLE_SEQ_BLOCK
    state_specs = [pl.BlockSpec((None, nb) + shp, lambda i: (l, i, 0, 0)) for shp in STATE_SHAPES]
    y_spec = pl.BlockSpec((nb, rows, D_MODEL), lambda i: (i, 0, 0))
    y_shape = jax.ShapeDtypeStruct((n_seq, rows, D_MODEL), F32)
    in_specs = [pl.BlockSpec((nb, rows, N_PROJ), lambda i: (i, 0, 0))] + state_specs + _mixer_param_specs()
    alias_in, alias_specs, aliases = _alias_args(prev_states, len(in_specs), 3)
    outs = pl.pallas_call(
        functools.partial(_sample_mixer_kernel, tv=tv, n_alias=len(alias_in)),
        grid=(n_seq // nb,),
        in_specs=in_specs + alias_specs,
        out_specs=[y_spec, y_spec, y_spec] + state_specs,
        out_shape=[y_shape, y_shape, y_shape] + _stacked_state_shapes(n_seq),
        scratch_shapes=_mixer_scratch(nb, rows),
        input_output_aliases=aliases,
        compiler_params=_compiler_params(1),
        name="sample_mixers",
    )(proj, *states_in, *_mixer_param_args(lw), *alias_in)
    return outs[:3], outs[3:]


def _merge_kernel(x_ref, mod_ref, yssd_ref, ym_ref, ypool_ref, wgl_ref, gb_ref,
                  lng_ref, lnb_ref, o_ref):
    sb, rb, _ = x_ref.shape
    n = sb * rb
    ub = _modulate(x_ref, mod_ref).astype(BF16)
    ys = [r[...].reshape(n, D_MODEL) for r in (yssd_ref, ym_ref, ypool_ref)]
    gates = [_sigmoid(_dot(ub, wgl_ref[:, i * D_MODEL:(i + 1) * D_MODEL])
                      + gb_ref[:, i * D_MODEL:(i + 1) * D_MODEL]) for i in range(3)]
    mix = _merge_rows(gates, ys, wgl_ref, col0=OFF_GL).reshape(sb, rb, D_MODEL)
    gate_a = mod_ref[:, :, 2 * D_MODEL:3 * D_MODEL]
    o_ref[...] = _layer_norm(ALPHA * x_ref[...] + (1.0 + gate_a) * mix, lng_ref[...], lnb_ref[...])


def _merge_call(x3, mods, ys, lw):
    n_seq, rows, _ = x3.shape
    sb, rb = _row_blocks(n_seq, rows)
    row_spec = pl.BlockSpec((sb, rb, D_MODEL), lambda i, j: (i, j, 0))
    vec_spec = _const_spec((1, D_MODEL))
    return pl.pallas_call(
        _merge_kernel,
        grid=(n_seq // sb, rows // rb),
        in_specs=[row_spec, pl.BlockSpec((sb, 1, 3 * D_MODEL), lambda i, j: (i, 0, 0)),
                  row_spec, row_spec, row_spec,
                  _packed_weight_spec(lw["layer"], 1), _const_spec((1, 3 * D_MODEL)), vec_spec, vec_spec],
        out_specs=row_spec,
        out_shape=jax.ShapeDtypeStruct(x3.shape, F32),
        compiler_params=_compiler_params(2),
        name="merge_norm",
    )(x3, mods, ys[0], ys[1], ys[2], lw["w_cat"], lw["gate_b"], lw["ln1_g"], lw["ln1_b"])


MOE_TILE = 512
MOE_BLOCK = 128
MOE_SLOT_BLOCKS = MOE_TILE // MOE_BLOCK + N_EGROUPS - 1
MOE_SLOTS = MOE_SLOT_BLOCKS * MOE_BLOCK
SIDE_POS, SIDE_HI, SIDE_MID, SIDE_LO = 0, 8, 16, 24


def _pad_rows(v, n):
    return jnp.concatenate([v, jnp.zeros((n - v.shape[0], v.shape[1]), v.dtype)], axis=0)


def _route_t(logits_t):
    t = logits_t.shape[1]
    row_g = lax.broadcasted_iota(jnp.int32, (SUBLANES, t), 0)
    lg = jnp.where(row_g < N_EGROUPS, logits_t[RT_G:RT_G + SUBLANES, :], -jnp.inf)
    g_max = jnp.max(lg, axis=0, keepdims=True)
    g_idx = jnp.min(jnp.where(lg == g_max, row_g, SUBLANES), axis=0, keepdims=True)
    g_prob = 1.0 / jnp.sum(jnp.exp(lg - g_max), axis=0, keepdims=True)
    row_e = lax.broadcasted_iota(jnp.int32, (N_EXPERTS, t), 0)
    le = jnp.where((row_e >> 2) == g_idx, logits_t[RT_E:RT_E + N_EXPERTS, :], -jnp.inf)
    v1 = jnp.max(le, axis=0, keepdims=True)
    i1 = jnp.min(jnp.where(le == v1, row_e, N_EXPERTS), axis=0, keepdims=True)
    le2 = jnp.where(row_e == i1, -jnp.inf, le)
    v2 = jnp.max(le2, axis=0, keepdims=True)
    i2 = jnp.min(jnp.where(le2 == v2, row_e, N_EXPERTS), axis=0, keepdims=True)
    e2 = jnp.exp(v2 - v1)
    p1 = g_prob / (1.0 + e2)
    p2 = g_prob * e2 / (1.0 + e2)
    wts = jnp.where(row_e == i1, p1, 0.0) + jnp.where(row_e == i2, p2, 0.0)
    w4 = None
    for g in range(N_EGROUPS):
        part = jnp.where(g_idx == g, wts[g * EXP_PER_GROUP:(g + 1) * EXP_PER_GROUP, :], 0.0)
        w4 = part if w4 is None else w4 + part
    return g_idx, w4


def _moe_kernel(x_ref, mod_ref, wrt_ref, brt_ref, wg_ref, wu_ref, wd_ref, lng_ref, lnb_ref, o_ref,
                sx_ref, sw_ref, so_ref):
    sb, rb, _ = x_ref.shape
    t = sb * rb
    u = _modulate(x_ref, mod_ref)
    u_hi = u.astype(BF16)
    u_lo = (u - u_hi.astype(F32)).astype(BF16)
    w = wrt_ref[...]
    w_hi = w.astype(BF16)
    w_lo = (w - w_hi.astype(F32)).astype(BF16)
    logits_t = _dot_nt(w_hi, u_hi) + _dot_nt(w_hi, u_lo) + _dot_nt(w_lo, u_hi) + brt_ref[...]
    g_idx, w4 = _route_t(logits_t)

    row_g = lax.broadcasted_iota(jnp.int32, (SUBLANES, t), 0)
    onehot_t = (row_g == g_idx).astype(F32)
    before = (lax.broadcasted_iota(jnp.int32, (t, t), 0) < lax.broadcasted_iota(jnp.int32, (t, t), 1))
    rank = _dot(onehot_t, before.astype(F32))
    cnt = jnp.sum(onehot_t, axis=1, keepdims=True)
    nblk = jnp.floor((cnt + (MOE_BLOCK - 1)) * (1.0 / MOE_BLOCK))
    sub = lax.broadcasted_iota(jnp.int32, (SUBLANES, 1), 0)
    first = jnp.zeros((SUBLANES, 1), F32)
    running = jnp.zeros((1, 1), F32)
    for g in range(1, N_EGROUPS):
        running = running + nblk[g - 1:g, :]
        first = first + jnp.where(sub == g, running, 0.0)
    pos_t = jnp.sum(onehot_t * (first * MOE_BLOCK + rank), axis=0, keepdims=True)
    blk_lane = lax.broadcasted_iota(jnp.int32, (SUBLANES, LANES), 1).astype(F32)
    in_blk = (blk_lane >= first) & (blk_lane < first + nblk)
    blk_group = jnp.sum(jnp.where(in_blk, sub.astype(F32), 0.0), axis=0, keepdims=True)

    w_hi4, w_mid4, w_lo4 = [p.astype(F32) for p in _split3(w4)]
    side_t = jnp.concatenate([_pad_rows(pos_t, SUBLANES), _pad_rows(w_hi4, SUBLANES),
                              _pad_rows(w_mid4, SUBLANES), _pad_rows(w_lo4, SUBLANES),
                              jnp.zeros((LANES - 4 * SUBLANES, t), F32)], axis=0)
    side = side_t.T

    slot_r = lax.broadcasted_iota(jnp.int32, (MOE_SLOTS, t), 0)
    send = (slot_r == pos_t.astype(jnp.int32)).astype(BF16)
    payload = jnp.concatenate([u_hi, side.astype(BF16)], axis=-1)
    sorted_rows = _dot(send, payload)
    sx_ref[...] = sorted_rows[:, 0:D_MODEL].astype(BF16)
    sw_ref[...] = sorted_rows[:, D_MODEL:D_MODEL + LANES]

    n_used = jnp.sum(nblk[:, 0:1]).astype(jnp.int32)
    for i in range(MOE_SLOT_BLOCKS):
        rows = slice(i * MOE_BLOCK, (i + 1) * MOE_BLOCK)

        def expert_block(i=i, rows=rows):
            g = blk_group[0, i].astype(jnp.int32)
            xb = sx_ref[rows, :]
            ws = sw_ref[rows, :]
            w_blk = (ws[:, SIDE_HI:SIDE_HI + EXP_PER_GROUP] + ws[:, SIDE_MID:SIDE_MID + EXP_PER_GROUP]
                     + ws[:, SIDE_LO:SIDE_LO + EXP_PER_GROUP])
            hid = []
            for r in range(EXP_PER_GROUP):
                e = g * EXP_PER_GROUP + r
                hid.append(_silu(_dot(xb, wg_ref[e])) * _dot(xb, wu_ref[e]) * w_blk[:, r:r + 1])
            so_ref[rows, :] = _dot(jnp.concatenate(hid, axis=-1), wd_ref[g]).astype(BF16)

        def empty_block(rows=rows):
            so_ref[rows, :] = jnp.zeros((MOE_BLOCK, D_MODEL), BF16)

        if i < MOE_TILE // MOE_BLOCK:
            expert_block()
        else:
            pl.when(i < n_used)(expert_block)
            pl.when(i >= n_used)(empty_block)

    slot_c = lax.broadcasted_iota(jnp.int32, (t, MOE_SLOTS), 1)
    fetch = (slot_c == side[:, SIDE_POS:SIDE_POS + 1].astype(jnp.int32)).astype(BF16)
    ffn = _dot(fetch, so_ref[...])
    gate_f = mod_ref[:, :, 2 * D_MODEL:3 * D_MODEL]
    o_ref[...] = _layer_norm(ALPHA * x_ref[...] + (1.0 + gate_f) * ffn.reshape(sb, rb, D_MODEL),
                             lng_ref[...], lnb_ref[...])


def _moe_call(x3, mods, lw):
    n_seq, rows, _ = x3.shape
    sb, rb = _row_blocks(n_seq, rows, MOE_TILE)
    row_spec = pl.BlockSpec((sb, rb, D_MODEL), lambda i, j: (i, j, 0))
    vec_spec = _const_spec((1, D_MODEL))
    layer = lw["layer"]
    layer_w = lambda shape: pl.BlockSpec((None,) + shape, lambda i, j: (layer,) + (0,) * len(shape),
                                         pipeline_mode=pl.Buffered(1))
    return pl.pallas_call(
        _moe_kernel,
        grid=(n_seq // sb, rows // rb),
        in_specs=[row_spec, pl.BlockSpec((sb, 1, 3 * D_MODEL), lambda i, j: (i, 0, 0)),
                  _vmem_full(), _const_spec((LANES, 1)),
                  layer_w((N_EXPERTS, D_MODEL, D_FF_E)), layer_w((N_EXPERTS, D_MODEL, D_FF_E)),
                  layer_w((N_EGROUPS, EXP_PER_GROUP * D_FF_E, D_MODEL)), vec_spec, vec_spec],
        out_specs=row_spec,
        out_shape=jax.ShapeDtypeStruct(x3.shape, F32),
        scratch_shapes=[pltpu.VMEM((MOE_SLOTS, D_MODEL), BF16), pltpu.VMEM((MOE_SLOTS, LANES), F32),
                        pltpu.VMEM((MOE_SLOTS, D_MODEL), BF16)],
        compiler_params=_compiler_params(2),
        name="moe_norm",
    )(x3, mods, lw["w_rt_t"], lw["b_rt_col"], lw["w_e_gate"], lw["w_e_up"], lw["w_e_down"],
      lw["ln2_g"], lw["ln2_b"])


def _split_w_in(w):
    sizes = (D_MODEL, SSD_CONV_DIM, SSD_HEADS, D_MODEL, D_MODEL, D_MODEL, M_HEADS, M_HEADS, D_MODEL,
             D_MODEL, 3 * D_MODEL)
    out, off = [], 0
    for s in sizes:
        out.append(w[..., off:off + s])
        off += s
    return out


def _pad_lanes(v, width=LANES):
    return jnp.pad(v, ((0, 0),) * (v.ndim - 1) + ((0, width - v.shape[-1]),))


def _packed_weights(p):
    wz, wxbc, wdt, wq, wk, wv, wi, wf, wo, wup, wgl = _split_w_in(p["w_in"])
    w_small = _pad_lanes(jnp.concatenate([wdt, wdt, wf, wi], axis=-1))
    return jnp.concatenate([wz, wxbc, w_small, wq, wk, wv, wo, wup, wgl, p["w_br_ssd"], p["w_br_mlstm"],
                            p["w_br_pool"], p["w_out"]], axis=-1).astype(BF16)


def _layer_weights(l, p):
    gate_b = p["mlstm_gate_b"][l]
    bias_row = jnp.concatenate([p["ssd_dt_bias"][l], p["ssd_dt_bias"][l], gate_b[M_HEADS:], gate_b[:M_HEADS]])
    head_params = jnp.concatenate([
        _pad_lanes(bias_row[None]), _pad_lanes(p["ssd_A_log"][l][None]), _pad_lanes(p["ssd_D"][l][None]),
        jnp.zeros((SUBLANES - 3, LANES), F32)], axis=0)
    w_rt = jnp.concatenate([_pad_lanes(p["w_rt_group"][l], RT_E), _pad_lanes(p["w_rt_expert"][l], LANES - RT_E)],
                           axis=1)
    b_rt = jnp.concatenate([_pad_lanes(p["b_rt_group"][l][None], RT_E),
                            _pad_lanes(p["b_rt_expert"][l][None], LANES - RT_E)], axis=1)
    row = lambda v: v[None]
    return dict(
        w_cat=p["w_packed_bf16"],
        conv_w=p["conv_w"][l], conv_b=row(p["conv_b"][l]), head_params=head_params,
        ssd_norm_w=row(p["ssd_norm_w"][l]), mlstm_norm_w=row(p["mlstm_norm_w"][l]),
        pool_w=p["pool_w"][l].astype(BF16), pool_scale=row(p["pool_scale"][l]),
        gate_b=row(p["gate_b"][l]),
        ln1_g=row(p["ln1_g"][l]), ln1_b=row(p["ln1_b"][l]),
        w_rt_t=w_rt.T, b_rt_col=b_rt.T,
        layer=l, w_e_gate=p["w_e_gate_bf16"], w_e_up=p["w_e_up_bf16"], w_e_down=p["w_e_down_bf16"],
        ln2_g=row(p["ln2_g"][l]), ln2_b=row(p["ln2_b"][l]),
    )


def _flat_states(ssd, conv, mc, mn, mm, pool):
    return tuple(a.reshape(a.shape[:2] + shp) for a, shp in zip((ssd, conv, mc, mn, mm, pool), STATE_SHAPES))


def _unflat_states(states):
    ssd, conv, mc, mn, mm, pool = states
    d, n = ssd.shape[:2]
    return (ssd.reshape(d, n, SSD_HEADS, SSD_HEADDIM, SSD_STATE), conv,
            mc.reshape(d, n, M_HEADS, M_HEADDIM, M_HEADDIM), mn, mm.reshape(d, n, M_HEADS), pool)


def kernel(x_prompt, x_sample, state_ssd, state_conv, state_mlstm_C, state_mlstm_n, state_mlstm_m, state_pool, c_prompt, c_sample, w_ada_mix, b_ada_mix, w_in, conv_w, conv_b, ssd_A_log, ssd_dt_bias, ssd_D, ssd_norm_w, mlstm_gate_b, mlstm_norm_w, pool_w, pool_scale, gate_b, w_br_ssd, w_br_mlstm, w_br_pool, w_out, ln1_g, ln1_b, w_ada_ffn, b_ada_ffn, w_rt_group, b_rt_group, w_rt_expert, b_rt_expert, w_e_gate, w_e_up, w_e_down, ln2_g, ln2_b):
    params = dict(w_in=w_in, conv_w=conv_w, conv_b=conv_b, ssd_A_log=ssd_A_log, ssd_dt_bias=ssd_dt_bias,
                  ssd_D=ssd_D, ssd_norm_w=ssd_norm_w, mlstm_gate_b=mlstm_gate_b, mlstm_norm_w=mlstm_norm_w,
                  pool_w=pool_w, pool_scale=pool_scale, gate_b=gate_b, w_br_ssd=w_br_ssd,
                  w_br_mlstm=w_br_mlstm, w_br_pool=w_br_pool, w_out=w_out, ln1_g=ln1_g, ln1_b=ln1_b,
                  w_rt_group=w_rt_group, b_rt_group=b_rt_group, w_rt_expert=w_rt_expert,
                  b_rt_expert=b_rt_expert, ln2_g=ln2_g, ln2_b=ln2_b)
    params.update(w_packed_bf16=_packed_weights(params),
                  w_e_gate_bf16=w_e_gate.astype(BF16), w_e_up_bf16=w_e_up.astype(BF16),
                  w_e_down_bf16=w_e_down.reshape(DEPTH, N_EGROUPS, EXP_PER_GROUP * D_FF_E, D_MODEL).astype(BF16))
    bp, seq, _ = x_prompt.shape
    bs, dec_seq, _ = x_sample.shape
    assert seq % CHUNK == 0 and 1 <= dec_seq <= SAMPLE_ROWS and bs % SAMPLE_SEQ_BLOCK == 0

    c_all = jnp.concatenate([c_prompt, c_sample], axis=0)
    mods_mix = _ada_call(c_all, w_ada_mix, b_ada_mix[:, None, :])
    mods_ffn = _ada_call(c_all, w_ada_ffn, b_ada_ffn[:, None, :])

    sample_in = _flat_states(state_ssd, state_conv, state_mlstm_C, state_mlstm_n, state_mlstm_m, state_pool)
    xp = x_prompt
    xs = jnp.pad(x_sample, ((0, 0), (0, SAMPLE_ROWS - dec_seq), (0, 0)))
    p_states, s_states = None, None
    for l in range(DEPTH):
        lw = _layer_weights(l, params)
        xp, p_states = _prompt_mixer_call(l, xp, mods_mix[l, :bp, None, :], p_states, lw)
        xp = _moe_call(xp, mods_ffn[l, :bp, None, :], lw)

        mods_s = mods_mix[l, bp:, None, :]
        proj = _proj_call(l, xs, mods_s, lw["w_cat"])
        ys, s_states = _sample_mixer_call(l, proj, sample_in, s_states, lw, tv=dec_seq)
        xs = _merge_call(xs, mods_s, ys, lw)
        xs = _moe_call(xs, mods_ffn[l, bp:, None, :], lw)
    return (xp, xs[:, :dec_seq]) + _unflat_states(p_states) + _unflat_states(s_states)
```

```python
import functools

import jax
import jax.numpy as jnp
from jax import lax
from jax.experimental import pallas as pl
from jax.experimental.pallas import tpu as pltpu

F32 = jnp.float32
BF16 = jnp.bfloat16

D_MODEL = 1024
DEPTH = 4
SSD_HEADS = 16
SSD_HEADDIM = 64
SSD_GROUPS = 2
SSD_REP = SSD_HEADS // SSD_GROUPS
SSD_STATE = 128
SSD_CONV = 4
SSD_CONV_DIM = D_MODEL + 2 * SSD_GROUPS * SSD_STATE
CHUNK = 128
M_HEADS = 4
M_HEADDIM = D_MODEL // M_HEADS
POOL_WINDOWS = (2, 4, 8, 16)
POOL_GW = D_MODEL // len(POOL_WINDOWS)
POOL_BUF = max(POOL_WINDOWS) - 1
N_EGROUPS = 4
EXP_PER_GROUP = 4
N_EXPERTS = N_EGROUPS * EXP_PER_GROUP
D_FF_E = D_MODEL // 4
ALPHA = (2 * DEPTH) ** 0.25
LN_EPS = 1e-5
RMS_EPS = 1e-6

SUBLANES = 8
LANES = 128
VMEM_LIMIT_BYTES = 56 * 1024 * 1024

OFF_Z = 0
OFF_XBC = OFF_Z + D_MODEL
OFF_SMALL = OFF_XBC + SSD_CONV_DIM
OFF_Q = OFF_SMALL + LANES
OFF_K = OFF_Q + D_MODEL
OFF_V = OFF_K + D_MODEL
OFF_O = OFF_V + D_MODEL
OFF_UP = OFF_O + D_MODEL
N_PROJ = OFF_UP + D_MODEL
SM_A, SM_DT, SM_F, SM_I, SM_END = 0, SSD_HEADS, 2 * SSD_HEADS, 2 * SSD_HEADS + M_HEADS, 2 * SSD_HEADS + 2 * M_HEADS
RT_G, RT_E = 0, 16
NEG_BIG = -1e30

ROW_TILE = 256
SAMPLE_ROWS = 8
SAMPLE_SEQ_BLOCK = 4

STATE_SHAPES = ((SSD_HEADS * SSD_HEADDIM, SSD_STATE), (SSD_CONV - 1, SSD_CONV_DIM),
                (M_HEADS * M_HEADDIM, M_HEADDIM), (M_HEADS, M_HEADDIM), (1, M_HEADS), (POOL_BUF, D_MODEL))
N_STATES = len(STATE_SHAPES)
N_MIXER_SCRATCH = 5
POOL_TOP = SUBLANES + POOL_BUF + 1


def _dot(a, b):
    return jnp.dot(a.astype(BF16), b.astype(BF16), preferred_element_type=F32)


def _dot_nt(a, b):
    return lax.dot_general(a.astype(BF16), b.astype(BF16), (((1,), (1,)), ((), ())),
                           preferred_element_type=F32)


def _dot_tn(a, b):
    return lax.dot_general(a.astype(BF16), b.astype(BF16), (((0,), (0,)), ((), ())),
                           preferred_element_type=F32)


def _split3(v):
    hi = v.astype(BF16)
    r1 = v - hi.astype(F32)
    mid = r1.astype(BF16)
    lo = (r1 - mid.astype(F32)).astype(BF16)
    return hi, mid, lo


def _select_dot(sel, v):
    n = v.shape[1]
    out = jnp.dot(sel.astype(BF16), jnp.concatenate(_split3(v), axis=1), preferred_element_type=F32)
    return out[:, 0:n] + out[:, n:2 * n] + out[:, 2 * n:3 * n]


def _select_dot_nt(sel, v):
    n = v.shape[0]
    out = lax.dot_general(sel.astype(BF16), jnp.concatenate(_split3(v), axis=0), (((1,), (1,)), ((), ())),
                          preferred_element_type=F32)
    return out[:, 0:n] + out[:, n:2 * n] + out[:, 2 * n:3 * n]


def _sigmoid(x):
    return lax.logistic(x)


def _silu(x):
    return x * _sigmoid(x)


def _softplus(x):
    return jnp.maximum(x, 0.0) + jnp.log1p(jnp.exp(-jnp.abs(x)))


def _layer_norm(x, g, b):
    mu = jnp.mean(x, axis=-1, keepdims=True)
    xc = x - mu
    var = jnp.mean(xc * xc, axis=-1, keepdims=True)
    return xc * lax.rsqrt(var + LN_EPS) * g + b


def _compiler_params(n_grid):
    return pltpu.CompilerParams(dimension_semantics=("arbitrary",) * n_grid,
                                vmem_limit_bytes=VMEM_LIMIT_BYTES)


def _vmem_full():
    return pl.BlockSpec(memory_space=pltpu.VMEM)


def _const_spec(shape):
    return pl.BlockSpec(shape, lambda *_: (0,) * len(shape))


def _packed_weight_spec(layer, half):
    return pl.BlockSpec((None, D_MODEL, N_PROJ), lambda *_: (layer, 0, half))


def _ada_kernel(c_ref, w_ref, b_ref, o_ref):
    o_ref[...] = _dot(c_ref[...], w_ref[...]) + b_ref[...]


def _ada_call(c_all, w, b):
    n = c_all.shape[0]
    return pl.pallas_call(
        _ada_kernel,
        grid=(DEPTH, 3),
        in_specs=[pl.BlockSpec((n, D_MODEL), lambda l, j: (0, 0)),
                  pl.BlockSpec((None, D_MODEL, D_MODEL), lambda l, j: (l, 0, j)),
                  pl.BlockSpec((None, 1, D_MODEL), lambda l, j: (l, 0, j))],
        out_specs=pl.BlockSpec((None, n, D_MODEL), lambda l, j: (l, 0, j)),
        out_shape=jax.ShapeDtypeStruct((DEPTH, n, 3 * D_MODEL), F32),
        compiler_params=_compiler_params(2),
        name="ada_mod",
    )(c_all, w, b)


def _modulate(x_ref, mod_ref):
    x = x_ref[...]
    shift = mod_ref[:, :, 0:D_MODEL]
    scale = mod_ref[:, :, D_MODEL:2 * D_MODEL]
    u = x * (1.0 + scale) + shift
    return u.reshape(x.shape[0] * x.shape[1], D_MODEL)


def _proj_kernel(x_ref, mod_ref, w_ref, o_ref):
    sb, rb, _ = x_ref.shape
    u = _modulate(x_ref, mod_ref).astype(BF16)
    col = 0
    while col < N_PROJ:
        width = min(D_MODEL, N_PROJ - col)
        o_ref[:, :, col:col + width] = _dot(u, w_ref[:, col:col + width]).reshape(sb, rb, width)
        col += width


def _row_blocks(n_seq, rows, tile=ROW_TILE):
    if rows >= tile:
        return 1, tile
    return tile // rows, rows


def _proj_call(layer, x3, mods, wcat):
    n_seq, rows, _ = x3.shape
    sb, rb = _row_blocks(n_seq, rows)
    return pl.pallas_call(
        _proj_kernel,
        grid=(n_seq // sb, rows // rb),
        in_specs=[pl.BlockSpec((sb, rb, D_MODEL), lambda i, j: (i, j, 0)),
                  pl.BlockSpec((sb, 1, 3 * D_MODEL), lambda i, j: (i, 0, 0)),
                  _packed_weight_spec(layer, 0)],
        out_specs=pl.BlockSpec((sb, rb, N_PROJ), lambda i, j: (i, j, 0)),
        out_shape=jax.ShapeDtypeStruct((n_seq, rows, N_PROJ), F32),
        compiler_params=_compiler_params(2),
        name="in_proj",
    )(x3, mods, wcat)


def _mixer_chunk(proj, st_in, st_out, par, scr, *, n_seq, rows, tv, pos0, after_stage=lambda name: None):
    L = n_seq * rows
    rows_log2 = rows.bit_length() - 1
    assert rows == 1 << rows_log2
    ssd_i, conv_i, mc_i, mn_i, mm_i, pool_i = st_in
    ssd_o, conv_o, mc_o, mn_o, mm_o, pool_o = st_out
    convw_ref, convb_ref, hp_ref, snw_ref, mnw_ref, poolw_ref, pscale_ref = par
    xext_ref, pext_ref, psa_ref, psb_ref, yacc_ref = scr
    seq_rows = [slice(s * rows, (s + 1) * rows) for s in range(n_seq)]

    def per_seq(fn):
        parts = [fn(s, seq_rows[s]) for s in range(n_seq)]
        return parts[0] if n_seq == 1 else jnp.concatenate(parts, axis=0)

    row_l = lax.broadcasted_iota(jnp.int32, (L, L), 0)
    col_l = lax.broadcasted_iota(jnp.int32, (L, L), 1)
    causal = row_l >= col_l
    if n_seq > 1:
        same_seq = (row_l >> rows_log2) == (col_l >> rows_log2)
        causal = causal & same_seq

    lane = lax.broadcasted_iota(jnp.int32, (L, LANES), 1)
    pre = proj(OFF_SMALL, LANES) + hp_ref[0:1, :]
    sp = _softplus(pre)
    a_row = -jnp.exp(hp_ref[1:2, :])
    pmat = jnp.where(lane < SM_DT, sp * a_row,
                     jnp.where(lane < SM_F, sp,
                               jnp.where(lane < SM_I, -_softplus(-pre),
                                         jnp.where(lane < SM_END, pre, 0.0))))
    if tv < rows:
        row = lax.broadcasted_iota(jnp.int32, (L, LANES), 0)
        pad = jnp.where(lane < SM_I, 0.0, jnp.where(lane < SM_END, NEG_BIG, 0.0))
        pmat = jnp.where((row & (rows - 1)) < tv, pmat, pad)
    eye = (lax.broadcasted_iota(jnp.int32, (LANES, LANES), 0)
           == lax.broadcasted_iota(jnp.int32, (LANES, LANES), 1))
    cum = _select_dot(causal, pmat)
    pmat_t = _select_dot_nt(eye, pmat)
    cum_t = _select_dot_nt(eye, cum)
    if n_seq > 1:
        tot = _select_dot(same_seq, pmat)
    else:
        tot = cum[L - 1:L, :]

    xext_ref[:, SUBLANES - (SSD_CONV - 1):SUBLANES, :] = conv_i[...]
    xext_ref[:, SUBLANES:SUBLANES + rows, :] = proj(OFF_XBC, SSD_CONV_DIM).reshape(n_seq, rows, SSD_CONV_DIM)
    acc = convb_ref[...]
    for k in range(SSD_CONV):
        start = SUBLANES - (SSD_CONV - 1) + k
        acc = acc + xext_ref[:, start:start + rows, :].reshape(L, SSD_CONV_DIM) * convw_ref[k:k + 1, :]
    conv_o[...] = xext_ref[:, SUBLANES + tv - (SSD_CONV - 1):SUBLANES + tv, :]
    after_stage("conv")
    xbc = _silu(acc)
    xs = xbc[:, 0:D_MODEL]
    d_row = hp_ref[2:3, :]
    gw = SSD_REP * SSD_HEADDIM
    for g in range(SSD_GROUPS):
        grp = slice(g * gw, (g + 1) * gw)
        bm = xbc[:, D_MODEL + g * SSD_STATE:D_MODEL + (g + 1) * SSD_STATE]
        cm = xbc[:, D_MODEL + (SSD_GROUPS + g) * SSD_STATE:D_MODEL + (SSD_GROUPS + g + 1) * SSD_STATE]
        cb = _dot_nt(cm, bm)
        y_state = per_seq(lambda s, rs: _dot_nt(cm[rs], ssd_i[s, grp, :]))
        xw_parts = []
        first_head = lax.broadcasted_iota(jnp.int32, (L, LANES), 1) < SSD_HEADDIM
        for r in range(0, SSD_REP, 2):
            heads = (g * SSD_REP + r, g * SSD_REP + r + 1)
            ps = slice(heads[0] * SSD_HEADDIM, heads[0] * SSD_HEADDIM + LANES)
            per_head = lambda fn: jnp.where(first_head, fn(heads[0]), fn(heads[1]))
            wmats = []
            for h in heads:
                cum_c = cum[:, SM_A + h:SM_A + h + 1]
                cum_r = cum_t[SM_A + h:SM_A + h + 1, :]
                dt_r = pmat_t[SM_DT + h:SM_DT + h + 1, :]
                seg = jnp.where(causal, cum_c - cum_r, -jnp.inf)
                wmats.append((cb * jnp.exp(seg) * dt_r).astype(BF16))
            x_p = xs[:, ps]
            x_diag = jnp.concatenate([jnp.where(first_head, x_p, 0.0), jnp.where(first_head, 0.0, x_p)], axis=0)
            decay_in = per_head(lambda h: jnp.exp(cum[:, SM_A + h:SM_A + h + 1]))
            y = (_dot(jnp.concatenate(wmats, axis=1), x_diag)
                 + y_state[:, r * SSD_HEADDIM:r * SSD_HEADDIM + LANES] * decay_in)
            yacc_ref[:, ps] = y + per_head(lambda h: d_row[:, h:h + 1]) * x_p
            xw_parts.append(x_p * per_head(
                lambda h: jnp.exp(tot[:, SM_A + h:SM_A + h + 1] - cum[:, SM_A + h:SM_A + h + 1])
                * pmat[:, SM_DT + h:SM_DT + h + 1]))
            after_stage("ssd_head")
            after_stage("ssd_head")
        xw = jnp.concatenate(xw_parts, axis=-1)
        for s in range(n_seq):
            upd = _dot_tn(xw[seq_rows[s]], bm[seq_rows[s]])
            t0 = s * rows if n_seq > 1 else 0
            for r in range(SSD_REP):
                h = g * SSD_REP + r
                hs = slice(h * SSD_HEADDIM, (h + 1) * SSD_HEADDIM)
                decay = jnp.exp(tot[t0:t0 + 1, SM_A + h:SM_A + h + 1])
                ssd_o[s, hs, :] = decay * ssd_i[s, hs, :] + upd[r * SSD_HEADDIM:(r + 1) * SSD_HEADDIM, :]
    yz = yacc_ref[...] * _silu(proj(OFF_Z, D_MODEL))
    y_ssd = yz * lax.rsqrt(jnp.mean(yz * yz, axis=-1, keepdims=True) + RMS_EPS) * snw_ref[...]
    after_stage("ssd")

    ym_parts = []
    for h in range(M_HEADS):
        hs = slice(h * M_HEADDIM, (h + 1) * M_HEADDIM)
        q_h = proj(OFF_Q + h * M_HEADDIM, M_HEADDIM)
        k_h = proj(OFF_K + h * M_HEADDIM, M_HEADDIM) * (M_HEADDIM ** -0.5)
        v_h = proj(OFF_V + h * M_HEADDIM, M_HEADDIM)
        o_h = proj(OFF_O + h * M_HEADDIM, M_HEADDIM)
        b_c = cum[:, SM_F + h:SM_F + h + 1]
        b_r = cum_t[SM_F + h:SM_F + h + 1, :]
        i_c = pmat[:, SM_I + h:SM_I + h + 1]
        i_r = pmat_t[SM_I + h:SM_I + h + 1, :]
        last_b = tot[:, SM_F + h:SM_F + h + 1]
        m_prev = per_seq(lambda s, rs: jnp.broadcast_to(mm_i[s, :, h:h + 1], (rows, 1)))
        dmat = jnp.where(causal, b_c - b_r + i_r, -jnp.inf)
        m_st = b_c + m_prev
        m = jnp.maximum(m_st, jnp.max(dmat, axis=-1, keepdims=True))
        wts = jnp.exp(dmat - m) * _dot_nt(q_h, k_h)
        ws = jnp.exp(m_st - m)
        cq = per_seq(lambda s, rs: _dot_nt(q_h[rs], mc_i[s, hs, :]))
        nq = per_seq(lambda s, rs: jnp.sum(q_h[rs] * mn_i[s, h:h + 1, :], axis=-1, keepdims=True))
        num = _dot(wts, v_h) + ws * cq
        den = jnp.sum(wts, axis=-1, keepdims=True) + ws * nq
        hc = num / jnp.maximum(jnp.abs(den), jnp.exp(-m))
        m_new = per_seq(lambda s, rs: jnp.broadcast_to(m[rs.stop - 1:rs.stop, :], (rows, 1)))
        wsrc = jnp.exp(last_b - b_c + i_c - m_new)
        wprev = jnp.exp(last_b + m_prev - m_new)
        vw = v_h * wsrc
        kw = k_h * wsrc
        for s in range(n_seq):
            rs = seq_rows[s]
            wp = wprev[rs.start:rs.start + 1, :]
            mc_o[s, hs, :] = wp * mc_i[s, hs, :] + _dot_tn(vw[rs], k_h[rs])
            mn_o[s, h:h + 1, :] = wp * mn_i[s, h:h + 1, :] + jnp.sum(kw[rs], axis=0, keepdims=True)
            mm_o[s, :, h:h + 1] = m_new[rs.start:rs.start + 1, :]
        mu = jnp.mean(hc, axis=-1, keepdims=True)
        hd = hc - mu
        var = jnp.mean(hd * hd, axis=-1, keepdims=True)
        ym_parts.append(hd * lax.rsqrt(var + LN_EPS) * mnw_ref[:, hs] * _sigmoid(o_h))
        after_stage("mlstm_head")
    y_m = jnp.concatenate(ym_parts, axis=-1)

    assert POOL_WINDOWS == tuple(2 << g for g in range(len(POOL_WINDOWS)))
    first_row = POOL_TOP - POOL_BUF
    pext_ref[:, 0:first_row, :] = jnp.zeros((n_seq, first_row, D_MODEL), F32)
    for ref in (psa_ref, psb_ref):
        ref[:, 0:SUBLANES, :] = jnp.zeros((n_seq, SUBLANES, D_MODEL), F32)
    pext_ref[:, first_row:POOL_TOP, :] = pool_i[...]
    up = proj(OFF_UP, D_MODEL)
    pext_ref[:, POOL_TOP:POOL_TOP + rows, :] = up.reshape(n_seq, rows, D_MODEL)
    span = POOL_TOP - SUBLANES + rows
    src, dst, shift = pext_ref, psa_ref, 1
    for g in range(len(POOL_WINDOWS) - 1):
        lanes = slice(g * POOL_GW, D_MODEL)
        dst[:, SUBLANES:SUBLANES + span, lanes] = (src[:, SUBLANES:SUBLANES + span, lanes]
                                                   + src[:, SUBLANES - shift:SUBLANES - shift + span, lanes])
        src, dst, shift = dst, (psb_ref if dst is psa_ref else psa_ref), 2 * shift
    last = slice((len(POOL_WINDOWS) - 1) * POOL_GW, D_MODEL)
    widest = src[:, POOL_TOP:POOL_TOP + rows, last] + src[:, POOL_TOP - shift:POOL_TOP - shift + rows, last]
    pos = (lax.broadcasted_iota(jnp.int32, (L, 1), 0) & (rows - 1)) + pos0
    yp_parts = []
    for g, w in enumerate(POOL_WINDOWS):
        gs = slice(g * POOL_GW, (g + 1) * POOL_GW)
        if g == len(POOL_WINDOWS) - 1:
            wsum = widest.reshape(L, POOL_GW)
        else:
            wsum = (psa_ref if g % 2 == 0 else psb_ref)[:, POOL_TOP:POOL_TOP + rows, gs].reshape(L, POOL_GW)
        cnt = jnp.minimum(pos, w).astype(F32)
        dlt = wsum / cnt - up[:, gs]
        yp_parts.append(_dot(dlt, poolw_ref[g]) * pscale_ref[:, gs])
        after_stage("pool")
    pool_o[...] = pext_ref[:, first_row + tv:POOL_TOP + tv, :]
    return y_ssd, y_m, jnp.concatenate(yp_parts, axis=-1)


def _merge_rows(gates, ys, w_ref, col0=0):
    merged = None
    for i, (gate, y) in enumerate(zip(gates, ys)):
        lo = OFF_BR - col0 + i * D_MODEL
        term = gate * _dot(y, w_ref[:, lo:lo + D_MODEL])
        merged = term if merged is None else merged + term
    return _dot(merged, w_ref[:, OFF_OUT - col0:OFF_OUT - col0 + D_MODEL])


N_MIXER_PARAMS = 7


def _mixer_param_specs():
    return [_const_spec((SSD_CONV, SSD_CONV_DIM)), _const_spec((1, SSD_CONV_DIM)),
            _const_spec((SUBLANES, LANES)), _const_spec((1, D_MODEL)), _const_spec((1, D_MODEL)),
            _const_spec((len(POOL_WINDOWS), POOL_GW, POOL_GW)), _const_spec((1, D_MODEL))]


def _mixer_param_args(lw):
    return (lw["conv_w"], lw["conv_b"], lw["head_params"], lw["ssd_norm_w"], lw["mlstm_norm_w"],
            lw["pool_w"], lw["pool_scale"])


def _mixer_scratch(n_seq, rows):
    pool_rows = pltpu.VMEM((n_seq, POOL_TOP + rows, D_MODEL), F32)
    return [pltpu.VMEM((n_seq, SUBLANES + rows, SSD_CONV_DIM), F32), pool_rows, pool_rows, pool_rows,
            pltpu.VMEM((n_seq * rows, D_MODEL), F32)]


def _stacked_state_shapes(n_seq):
    return [jax.ShapeDtypeStruct((DEPTH, n_seq) + shp, F32) for shp in STATE_SHAPES]


def _alias_args(prev_states, n_inputs_before, n_outputs_before):
    if prev_states is None:
        return [], [], {}
    specs = [pl.BlockSpec(memory_space=pl.ANY)] * N_STATES
    aliases = {n_inputs_before + k: n_outputs_before + k for k in range(N_STATES)}
    return list(prev_states), specs, aliases


N_PROJ_ALL = N_PROJ + 3 * D_MODEL
OFF_GL = N_PROJ
OFF_BR = N_PROJ_ALL
OFF_OUT = OFF_BR + 3 * D_MODEL
assert OFF_GL == N_PROJ and OFF_OUT + D_MODEL - OFF_GL <= N_PROJ


def _column_pieces(lo, hi, width):
    return [(off, min(width, hi - off)) for off in range(lo, hi, width)]


def _prompt_mixer_kernel(x_ref, mod_ref, xn_ref, modn_ref, wcat_ref, gb_ref, lng_ref, lnb_ref, *rest, n_alias):
    par = rest[:N_MIXER_PARAMS]
    rest = rest[N_MIXER_PARAMS + n_alias:]
    o_ref = rest[0]
    states = rest[1:1 + N_STATES]
    scr = rest[1 + N_STATES:1 + N_STATES + N_MIXER_SCRATCH]
    pscr_ref = rest[1 + N_STATES + N_MIXER_SCRATCH]
    L = x_ref.shape[0]
    b = pl.program_id(0)
    c = pl.program_id(1)

    def modulated(xr, mr):
        return (xr[...] * (1.0 + mr[:, D_MODEL:2 * D_MODEL]) + mr[:, 0:D_MODEL]).astype(BF16)

    def project_into_scratch(ub, piece):
        off, width = piece
        pscr_ref[:, off:off + width] = _dot(ub, wcat_ref[:, off:off + width])

    @pl.when(c == 0)
    def _fresh_prompt_states():
        for ref in states:
            ref[...] = jnp.zeros(ref.shape, ref.dtype)

    @pl.when(jnp.logical_and(b == 0, c == 0))
    def _first_chunk_projections():
        ub0 = modulated(x_ref, mod_ref)
        for piece in _column_pieces(0, N_PROJ, D_MODEL) + _column_pieces(OFF_GL, N_PROJ_ALL, D_MODEL):
            project_into_scratch(ub0, piece)

    ub_next = modulated(xn_ref, modn_ref)
    ready = []
    released_by = {
        "conv": [_column_pieces(OFF_XBC, OFF_Q, 896)],
        "ssd": [_column_pieces(OFF_Z, OFF_XBC, D_MODEL)],
        "mlstm_head": [[(off + h * M_HEADDIM, M_HEADDIM) for off in (OFF_Q, OFF_K, OFF_V, OFF_O)]
                       for h in range(M_HEADS)],
        "pool": [_column_pieces(OFF_UP, N_PROJ, D_MODEL)],
    }
    pieces_per_call = {"conv": 2, "ssd_head": 1}

    def after_stage(name):
        if released_by.get(name):
            ready.extend(released_by[name].pop(0))
        for _ in range(min(len(ready), pieces_per_call.get(name, len(ready)))):
            project_into_scratch(ub_next, ready.pop(0))

    gates = [_sigmoid(pscr_ref[:, OFF_GL + i * D_MODEL:OFF_GL + (i + 1) * D_MODEL]
                      + gb_ref[:, i * D_MODEL:(i + 1) * D_MODEL]) for i in range(3)]
    ready.extend(_column_pieces(OFF_GL, N_PROJ_ALL, 768))
    proj = lambda off, width: pscr_ref[:, off:off + width]
    ys = _mixer_chunk(proj, states, states, par, scr, n_seq=1, rows=L, tv=L, pos0=c * L + 1,
                      after_stage=after_stage)
    after_stage("rest")
    assert not ready and not any(released_by.values())
    mix = _merge_rows(gates, ys, wcat_ref)
    o_ref[...] = _layer_norm(ALPHA * x_ref[...] + (1.0 + mod_ref[:, 2 * D_MODEL:3 * D_MODEL]) * mix,
                             lng_ref[...], lnb_ref[...])


def _prompt_mixer_call(l, x3, mods, prev_states, lw):
    n_seq, rows, _ = x3.shape
    L = CHUNK
    nc = rows // L

    def next_chunk(b, c):
        flat = jnp.minimum(b * nc + c + 1, n_seq * nc - 1)
        return flat // nc, flat % nc

    row_spec = pl.BlockSpec((None, L, D_MODEL), lambda b, c: (b, c, 0))
    mod_spec = pl.BlockSpec((None, 1, 3 * D_MODEL), lambda b, c: (b, 0, 0))
    next_row_spec = pl.BlockSpec((None, L, D_MODEL), lambda b, c: next_chunk(b, c) + (0,))
    next_mod_spec = pl.BlockSpec((None, 1, 3 * D_MODEL), lambda b, c: (next_chunk(b, c)[0], 0, 0))
    vec_spec = _const_spec((1, D_MODEL))
    state_specs = [pl.BlockSpec((None, 1) + shp, lambda b, c: (l, b, 0, 0)) for shp in STATE_SHAPES]
    in_specs = [row_spec, mod_spec, next_row_spec, next_mod_spec,
                pl.BlockSpec((None, D_MODEL, OFF_OUT + D_MODEL), lambda b, c: (l, 0, 0), pipeline_mode=pl.Buffered(1)),
                _const_spec((1, 3 * D_MODEL)), vec_spec, vec_spec]
    in_specs += _mixer_param_specs()
    alias_in, alias_specs, aliases = _alias_args(prev_states, len(in_specs), 1)
    outs = pl.pallas_call(
        functools.partial(_prompt_mixer_kernel, n_alias=len(alias_in)),
        grid=(n_seq, nc),
        in_specs=in_specs + alias_specs,
        out_specs=[row_spec] + state_specs,
        out_shape=[jax.ShapeDtypeStruct(x3.shape, F32)] + _stacked_state_shapes(n_seq),
        scratch_shapes=_mixer_scratch(1, L) + [pltpu.VMEM((L, N_PROJ_ALL), F32)],
        input_output_aliases=aliases,
        compiler_params=_compiler_params(2),
        name="prompt_mixers",
    )(x3, mods, x3, mods, lw["w_cat"], lw["gate_b"], lw["ln1_g"], lw["ln1_b"], *_mixer_param_args(lw), *alias_in)
    return outs[0], outs[1:]


MC_STATE = 2
MC_RING = 3


def _sample_mixer_kernel(proj_ref, *rest, tv, n_alias, layer):
    st_in = list(rest[:N_STATES])
    par = rest[N_STATES:N_STATES + N_MIXER_PARAMS]
    rest = rest[N_STATES + N_MIXER_PARAMS + n_alias:]
    y_refs = rest[:3]
    st_out = rest[3:3 + N_STATES]
    scr = rest[3 + N_STATES:3 + N_STATES + N_MIXER_SCRATCH]
    ring_ref, sem_ref = rest[3 + N_STATES + N_MIXER_SCRATCH:]
    n_blk, rows, _ = proj_ref.shape
    mc_hbm = st_in[MC_STATE]
    step = pl.program_id(0)
    n_steps = pl.num_programs(0)

    def mc_copy(k):
        slot = k % MC_RING
        return pltpu.make_async_copy(mc_hbm.at[layer, pl.ds(k * n_blk, n_blk)], ring_ref.at[slot], sem_ref.at[slot])

    @pl.when(step == 0)
    def _fill_ring():
        mc_copy(0).start()

        @pl.when(n_steps > 1)
        def _():
            mc_copy(1).start()

    @pl.when(step + 2 < n_steps)
    def _fetch_ahead():
        mc_copy(step + 2).start()

    mc_copy(step).wait()
    st_in[MC_STATE] = ring_ref.at[step % MC_RING]
    proj = lambda off, width: proj_ref[:, :, off:off + width].reshape(n_blk * rows, width)
    ys = _mixer_chunk(proj, st_in, st_out, par, scr, n_seq=n_blk, rows=rows, tv=tv, pos0=1 + POOL_BUF)
    for y_ref, y in zip(y_refs, ys):
        y_ref[...] = y.reshape(n_blk, rows, D_MODEL)


def _sample_mixer_call(l, proj, states_in, prev_states, lw, *, tv):
    n_seq, rows, _ = proj.shape
    nb = SAMPLE_SEQ_BLOCK
    state_specs = [pl.BlockSpec((None, nb) + shp, lambda i: (l, i, 0, 0)) for shp in STATE_SHAPES]
    y_spec = pl.BlockSpec((nb, rows, D_MODEL), lambda i: (i, 0, 0))
    y_shape = jax.ShapeDtypeStruct((n_seq, rows, D_MODEL), F32)
    state_in_specs = list(state_specs)
    state_in_specs[MC_STATE] = pl.BlockSpec(memory_space=pl.ANY)
    in_specs = [pl.BlockSpec((nb, rows, N_PROJ), lambda i: (i, 0, 0))] + state_in_specs + _mixer_param_specs()
    alias_in, alias_specs, aliases = _alias_args(prev_states, len(in_specs), 3)
    outs = pl.pallas_call(
        functools.partial(_sample_mixer_kernel, tv=tv, n_alias=len(alias_in), layer=l),
        grid=(n_seq // nb,),
        in_specs=in_specs + alias_specs,
        out_specs=[y_spec, y_spec, y_spec] + state_specs,
        out_shape=[y_shape, y_shape, y_shape] + _stacked_state_shapes(n_seq),
        scratch_shapes=_mixer_scratch(nb, rows) + [pltpu.VMEM((MC_RING, nb) + STATE_SHAPES[MC_STATE], F32),
                                                   pltpu.SemaphoreType.DMA((MC_RING,))],
        input_output_aliases=aliases,
        compiler_params=_compiler_params(1),
        name="sample_mixers",
    )(proj, *states_in, *_mixer_param_args(lw), *alias_in)
    return outs[:3], outs[3:]


def _merge_kernel(x_ref, mod_ref, yssd_ref, ym_ref, ypool_ref, wgl_ref, gb_ref,
                  lng_ref, lnb_ref, o_ref):
    sb, rb, _ = x_ref.shape
    n = sb * rb
    ub = _modulate(x_ref, mod_ref).astype(BF16)
    ys = [r[...].reshape(n, D_MODEL) for r in (yssd_ref, ym_ref, ypool_ref)]
    gates = [_sigmoid(_dot(ub, wgl_ref[:, i * D_MODEL:(i + 1) * D_MODEL])
                      + gb_ref[:, i * D_MODEL:(i + 1) * D_MODEL]) for i in range(3)]
    mix = _merge_rows(gates, ys, wgl_ref, col0=OFF_GL).reshape(sb, rb, D_MODEL)
    gate_a = mod_ref[:, :, 2 * D_MODEL:3 * D_MODEL]
    o_ref[...] = _layer_norm(ALPHA * x_ref[...] + (1.0 + gate_a) * mix, lng_ref[...], lnb_ref[...])


def _merge_call(x3, mods, ys, lw):
    n_seq, rows, _ = x3.shape
    sb, rb = _row_blocks(n_seq, rows)
    row_spec = pl.BlockSpec((sb, rb, D_MODEL), lambda i, j: (i, j, 0))
    vec_spec = _const_spec((1, D_MODEL))
    return pl.pallas_call(
        _merge_kernel,
        grid=(n_seq // sb, rows // rb),
        in_specs=[row_spec, pl.BlockSpec((sb, 1, 3 * D_MODEL), lambda i, j: (i, 0, 0)),
                  row_spec, row_spec, row_spec,
                  _packed_weight_spec(lw["layer"], 1), _const_spec((1, 3 * D_MODEL)), vec_spec, vec_spec],
        out_specs=row_spec,
        out_shape=jax.ShapeDtypeStruct(x3.shape, F32),
        compiler_params=_compiler_params(2),
        name="merge_norm",
    )(x3, mods, ys[0], ys[1], ys[2], lw["w_cat"], lw["gate_b"], lw["ln1_g"], lw["ln1_b"])


MOE_TILE = 512
MOE_BLOCK = 128
MOE_SLOT_BLOCKS = MOE_TILE // MOE_BLOCK + N_EGROUPS - 1
MOE_SLOTS = MOE_SLOT_BLOCKS * MOE_BLOCK
SIDE_POS, SIDE_HI, SIDE_MID, SIDE_LO = 0, 8, 16, 24


def _pad_rows(v, n):
    return jnp.concatenate([v, jnp.zeros((n - v.shape[0], v.shape[1]), v.dtype)], axis=0)


def _route_t(logits_t):
    t = logits_t.shape[1]
    row_g = lax.broadcasted_iota(jnp.int32, (SUBLANES, t), 0)
    lg = jnp.where(row_g < N_EGROUPS, logits_t[RT_G:RT_G + SUBLANES, :], -jnp.inf)
    g_max = jnp.max(lg, axis=0, keepdims=True)
    g_idx = jnp.min(jnp.where(lg == g_max, row_g, SUBLANES), axis=0, keepdims=True)
    g_prob = 1.0 / jnp.sum(jnp.exp(lg - g_max), axis=0, keepdims=True)
    row_e = lax.broadcasted_iota(jnp.int32, (N_EXPERTS, t), 0)
    le = jnp.where((row_e >> 2) == g_idx, logits_t[RT_E:RT_E + N_EXPERTS, :], -jnp.inf)
    v1 = jnp.max(le, axis=0, keepdims=True)
    i1 = jnp.min(jnp.where(le == v1, row_e, N_EXPERTS), axis=0, keepdims=True)
    le2 = jnp.where(row_e == i1, -jnp.inf, le)
    v2 = jnp.max(le2, axis=0, keepdims=True)
    i2 = jnp.min(jnp.where(le2 == v2, row_e, N_EXPERTS), axis=0, keepdims=True)
    e2 = jnp.exp(v2 - v1)
    p1 = g_prob / (1.0 + e2)
    p2 = g_prob * e2 / (1.0 + e2)
    wts = jnp.where(row_e == i1, p1, 0.0) + jnp.where(row_e == i2, p2, 0.0)
    w4 = None
    for g in range(N_EGROUPS):
        part = jnp.where(g_idx == g, wts[g * EXP_PER_GROUP:(g + 1) * EXP_PER_GROUP, :], 0.0)
        w4 = part if w4 is None else w4 + part
    return g_idx, w4


def _moe_kernel(x_ref, mod_ref, wrt_ref, brt_ref, wg_ref, wu_ref, wd_ref, lng_ref, lnb_ref, o_ref,
                sx_ref, sw_ref, so_ref):
    sb, rb, _ = x_ref.shape
    t = sb * rb
    u = _modulate(x_ref, mod_ref)
    u_hi = u.astype(BF16)
    u_lo = (u - u_hi.astype(F32)).astype(BF16)
    w = wrt_ref[...]
    w_hi = w.astype(BF16)
    w_lo = (w - w_hi.astype(F32)).astype(BF16)
    logits_t = _dot_nt(w_hi, u_hi) + _dot_nt(w_hi, u_lo) + _dot_nt(w_lo, u_hi) + brt_ref[...]
    g_idx, w4 = _route_t(logits_t)

    row_g = lax.broadcasted_iota(jnp.int32, (SUBLANES, t), 0)
    onehot_t = (row_g == g_idx).astype(F32)
    before = (lax.broadcasted_iota(jnp.int32, (t, t), 0) < lax.broadcasted_iota(jnp.int32, (t, t), 1))
    rank = _dot(onehot_t, before.astype(F32))
    cnt = jnp.sum(onehot_t, axis=1, keepdims=True)
    nblk = jnp.floor((cnt + (MOE_BLOCK - 1)) * (1.0 / MOE_BLOCK))
    sub = lax.broadcasted_iota(jnp.int32, (SUBLANES, 1), 0)
    first = jnp.zeros((SUBLANES, 1), F32)
    running = jnp.zeros((1, 1), F32)
    for g in range(1, N_EGROUPS):
        running = running + nblk[g - 1:g, :]
        first = first + jnp.where(sub == g, running, 0.0)
    pos_t = jnp.sum(onehot_t * (first * MOE_BLOCK + rank), axis=0, keepdims=True)
    blk_lane = lax.broadcasted_iota(jnp.int32, (SUBLANES, LANES), 1).astype(F32)
    in_blk = (blk_lane >= first) & (blk_lane < first + nblk)
    blk_group = jnp.sum(jnp.where(in_blk, sub.astype(F32), 0.0), axis=0, keepdims=True)

    w_hi4, w_mid4, w_lo4 = [p.astype(F32) for p in _split3(w4)]
    side_t = jnp.concatenate([_pad_rows(pos_t, SUBLANES), _pad_rows(w_hi4, SUBLANES),
                              _pad_rows(w_mid4, SUBLANES), _pad_rows(w_lo4, SUBLANES),
                              jnp.zeros((LANES - 4 * SUBLANES, t), F32)], axis=0)
    side = side_t.T

    slot_r = lax.broadcasted_iota(jnp.int32, (MOE_SLOTS, t), 0)
    send = (slot_r == pos_t.astype(jnp.int32)).astype(BF16)
    payload = jnp.concatenate([u_hi, side.astype(BF16)], axis=-1)
    sorted_rows = _dot(send, payload)
    sx_ref[...] = sorted_rows[:, 0:D_MODEL].astype(BF16)
    sw_ref[...] = sorted_rows[:, D_MODEL:D_MODEL + LANES]

    n_used = jnp.sum(nblk[:, 0:1]).astype(jnp.int32)
    for i in range(MOE_SLOT_BLOCKS):
        rows = slice(i * MOE_BLOCK, (i + 1) * MOE_BLOCK)

        def expert_block(i=i, rows=rows):
            g = blk_group[0, i].astype(jnp.int32)
            xb = sx_ref[rows, :]
            ws = sw_ref[rows, :]
            w_blk = (ws[:, SIDE_HI:SIDE_HI + EXP_PER_GROUP] + ws[:, SIDE_MID:SIDE_MID + EXP_PER_GROUP]
                     + ws[:, SIDE_LO:SIDE_LO + EXP_PER_GROUP])
            hid = []
            for r in range(EXP_PER_GROUP):
                e = g * EXP_PER_GROUP + r
                hid.append(_silu(_dot(xb, wg_ref[e])) * _dot(xb, wu_ref[e]) * w_blk[:, r:r + 1])
            so_ref[rows, :] = _dot(jnp.concatenate(hid, axis=-1), wd_ref[g]).astype(BF16)

        def empty_block(rows=rows):
            so_ref[rows, :] = jnp.zeros((MOE_BLOCK, D_MODEL), BF16)

        if i < MOE_TILE // MOE_BLOCK:
            expert_block()
        else:
            pl.when(i < n_used)(expert_block)
            pl.when(i >= n_used)(empty_block)

    slot_c = lax.broadcasted_iota(jnp.int32, (t, MOE_SLOTS), 1)
    fetch = (slot_c == side[:, SIDE_POS:SIDE_POS + 1].astype(jnp.int32)).astype(BF16)
    ffn = _dot(fetch, so_ref[...])
    gate_f = mod_ref[:, :, 2 * D_MODEL:3 * D_MODEL]
    o_ref[...] = _layer_norm(ALPHA * x_ref[...] + (1.0 + gate_f) * ffn.reshape(sb, rb, D_MODEL),
                             lng_ref[...], lnb_ref[...])


def _moe_call(x3, mods, lw):
    n_seq, rows, _ = x3.shape
    sb, rb = _row_blocks(n_seq, rows, MOE_TILE)
    row_spec = pl.BlockSpec((sb, rb, D_MODEL), lambda i, j: (i, j, 0))
    vec_spec = _const_spec((1, D_MODEL))
    layer = lw["layer"]
    layer_w = lambda shape: pl.BlockSpec((None,) + shape, lambda i, j: (layer,) + (0,) * len(shape),
                                         pipeline_mode=pl.Buffered(1))
    return pl.pallas_call(
        _moe_kernel,
        grid=(n_seq // sb, rows // rb),
        in_specs=[row_spec, pl.BlockSpec((sb, 1, 3 * D_MODEL), lambda i, j: (i, 0, 0)),
                  _vmem_full(), _const_spec((LANES, 1)),
                  layer_w((N_EXPERTS, D_MODEL, D_FF_E)), layer_w((N_EXPERTS, D_MODEL, D_FF_E)),
                  layer_w((N_EGROUPS, EXP_PER_GROUP * D_FF_E, D_MODEL)), vec_spec, vec_spec],
        out_specs=row_spec,
        out_shape=jax.ShapeDtypeStruct(x3.shape, F32),
        scratch_shapes=[pltpu.VMEM((MOE_SLOTS, D_MODEL), BF16), pltpu.VMEM((MOE_SLOTS, LANES), F32),
                        pltpu.VMEM((MOE_SLOTS, D_MODEL), BF16)],
        compiler_params=_compiler_params(2),
        name="moe_norm",
    )(x3, mods, lw["w_rt_t"], lw["b_rt_col"], lw["w_e_gate"], lw["w_e_up"], lw["w_e_down"],
      lw["ln2_g"], lw["ln2_b"])


def _split_w_in(w):
    sizes = (D_MODEL, SSD_CONV_DIM, SSD_HEADS, D_MODEL, D_MODEL, D_MODEL, M_HEADS, M_HEADS, D_MODEL,
             D_MODEL, 3 * D_MODEL)
    out, off = [], 0
    for s in sizes:
        out.append(w[..., off:off + s])
        off += s
    return out


def _pad_lanes(v, width=LANES):
    return jnp.pad(v, ((0, 0),) * (v.ndim - 1) + ((0, width - v.shape[-1]),))


def _packed_weights(p):
    wz, wxbc, wdt, wq, wk, wv, wi, wf, wo, wup, wgl = _split_w_in(p["w_in"])
    w_small = _pad_lanes(jnp.concatenate([wdt, wdt, wf, wi], axis=-1))
    return jnp.concatenate([wz, wxbc, w_small, wq, wk, wv, wo, wup, wgl, p["w_br_ssd"], p["w_br_mlstm"],
                            p["w_br_pool"], p["w_out"]], axis=-1).astype(BF16)


def _layer_weights(l, p):
    gate_b = p["mlstm_gate_b"][l]
    bias_row = jnp.concatenate([p["ssd_dt_bias"][l], p["ssd_dt_bias"][l], gate_b[M_HEADS:], gate_b[:M_HEADS]])
    head_params = jnp.concatenate([
        _pad_lanes(bias_row[None]), _pad_lanes(p["ssd_A_log"][l][None]), _pad_lanes(p["ssd_D"][l][None]),
        jnp.zeros((SUBLANES - 3, LANES), F32)], axis=0)
    w_rt = jnp.concatenate([_pad_lanes(p["w_rt_group"][l], RT_E), _pad_lanes(p["w_rt_expert"][l], LANES - RT_E)],
                           axis=1)
    b_rt = jnp.concatenate([_pad_lanes(p["b_rt_group"][l][None], RT_E),
                            _pad_lanes(p["b_rt_expert"][l][None], LANES - RT_E)], axis=1)
    row = lambda v: v[None]
    return dict(
        w_cat=p["w_packed_bf16"],
        conv_w=p["conv_w"][l], conv_b=row(p["conv_b"][l]), head_params=head_params,
        ssd_norm_w=row(p["ssd_norm_w"][l]), mlstm_norm_w=row(p["mlstm_norm_w"][l]),
        pool_w=p["pool_w"][l].astype(BF16), pool_scale=row(p["pool_scale"][l]),
        gate_b=row(p["gate_b"][l]),
        ln1_g=row(p["ln1_g"][l]), ln1_b=row(p["ln1_b"][l]),
        w_rt_t=w_rt.T, b_rt_col=b_rt.T,
        layer=l, w_e_gate=p["w_e_gate_bf16"], w_e_up=p["w_e_up_bf16"], w_e_down=p["w_e_down_bf16"],
        ln2_g=row(p["ln2_g"][l]), ln2_b=row(p["ln2_b"][l]),
    )


def _flat_states(ssd, conv, mc, mn, mm, pool):
    return tuple(a.reshape(a.shape[:2] + shp) for a, shp in zip((ssd, conv, mc, mn, mm, pool), STATE_SHAPES))


def _unflat_states(states):
    ssd, conv, mc, mn, mm, pool = states
    d, n = ssd.shape[:2]
    return (ssd.reshape(d, n, SSD_HEADS, SSD_HEADDIM, SSD_STATE), conv,
            mc.reshape(d, n, M_HEADS, M_HEADDIM, M_HEADDIM), mn, mm.reshape(d, n, M_HEADS), pool)


def kernel(x_prompt, x_sample, state_ssd, state_conv, state_mlstm_C, state_mlstm_n, state_mlstm_m, state_pool, c_prompt, c_sample, w_ada_mix, b_ada_mix, w_in, conv_w, conv_b, ssd_A_log, ssd_dt_bias, ssd_D, ssd_norm_w, mlstm_gate_b, mlstm_norm_w, pool_w, pool_scale, gate_b, w_br_ssd, w_br_mlstm, w_br_pool, w_out, ln1_g, ln1_b, w_ada_ffn, b_ada_ffn, w_rt_group, b_rt_group, w_rt_expert, b_rt_expert, w_e_gate, w_e_up, w_e_down, ln2_g, ln2_b):
    params = dict(w_in=w_in, conv_w=conv_w, conv_b=conv_b, ssd_A_log=ssd_A_log, ssd_dt_bias=ssd_dt_bias,
                  ssd_D=ssd_D, ssd_norm_w=ssd_norm_w, mlstm_gate_b=mlstm_gate_b, mlstm_norm_w=mlstm_norm_w,
                  pool_w=pool_w, pool_scale=pool_scale, gate_b=gate_b, w_br_ssd=w_br_ssd,
                  w_br_mlstm=w_br_mlstm, w_br_pool=w_br_pool, w_out=w_out, ln1_g=ln1_g, ln1_b=ln1_b,
                  w_rt_group=w_rt_group, b_rt_group=b_rt_group, w_rt_expert=w_rt_expert,
                  b_rt_expert=b_rt_expert, ln2_g=ln2_g, ln2_b=ln2_b)
    params.update(w_packed_bf16=_packed_weights(params),
                  w_e_gate_bf16=w_e_gate.astype(BF16), w_e_up_bf16=w_e_up.astype(BF16),
                  w_e_down_bf16=w_e_down.reshape(DEPTH, N_EGROUPS, EXP_PER_GROUP * D_FF_E, D_MODEL).astype(BF16))
    bp, seq, _ = x_prompt.shape
    bs, dec_seq, _ = x_sample.shape
    assert seq % CHUNK == 0 and 1 <= dec_seq <= SAMPLE_ROWS and bs % SAMPLE_SEQ_BLOCK == 0

    c_all = jnp.concatenate([c_prompt, c_sample], axis=0)
    mods_mix = _ada_call(c_all, w_ada_mix, b_ada_mix[:, None, :])
    mods_ffn = _ada_call(c_all, w_ada_ffn, b_ada_ffn[:, None, :])

    sample_in = _flat_states(state_ssd, state_conv, state_mlstm_C, state_mlstm_n, state_mlstm_m, state_pool)
    xp = x_prompt
    xs = jnp.pad(x_sample, ((0, 0), (0, SAMPLE_ROWS - dec_seq), (0, 0)))
    p_states, s_states = None, None
    for l in range(DEPTH):
        lw = _layer_weights(l, params)
        xp, p_states = _prompt_mixer_call(l, xp, mods_mix[l, :bp, None, :], p_states, lw)
        xp = _moe_call(xp, mods_ffn[l, :bp, None, :], lw)

        mods_s = mods_mix[l, bp:, None, :]
        proj = _proj_call(l, xs, mods_s, lw["w_cat"])
        ys, s_states = _sample_mixer_call(l, proj, sample_in, s_states, lw, tv=dec_seq)
        xs = _merge_call(xs, mods_s, ys, lw)
        xs = _moe_call(xs, mods_ffn[l, bp:, None, :], lw)
    return (xp, xs[:, :dec_seq]) + _unflat_states(p_states) + _unflat_states(s_states)
```
